```python
import jax
import jax.numpy as jnp
from jax import lax
import numpy as np


D_MODEL = 2048
BATCH = 1
SEQ = 16384
DEPTH = 1

SB_HEADS = 16
SB_HEAD_DIM = 64
SB_WIDTH = SB_HEADS * SB_HEAD_DIM
SB_BLOCK = 128
RWKV_HEADS = 16
RWKV_HEAD_DIM = 64
RWKV_WIDTH = RWKV_HEADS * RWKV_HEAD_DIM
DECAY_LORA = 64
ICLR_LORA = 64
GATE_LORA = 160
N_SHIFT_MIX = 6
GN_EPS = 64e-5
IN_COLS = 3 * SB_WIDTH + 4 * RWKV_WIDTH + 2 * D_MODEL
IN_SPLITS = (SB_WIDTH, 2 * SB_WIDTH, 3 * SB_WIDTH,
             3 * SB_WIDTH + RWKV_WIDTH, 3 * SB_WIDTH + 2 * RWKV_WIDTH,
             3 * SB_WIDTH + 3 * RWKV_WIDTH, 3 * SB_WIDTH + 4 * RWKV_WIDTH,
             3 * SB_WIDTH + 4 * RWKV_WIDTH + D_MODEL)
PEER_HEADS = 8
PEER_TOPK = 16
N_KEYS = 128
N_EXPERTS = N_KEYS * N_KEYS
PEER_KEY_DIM = 256
PEER_CHUNK = 128
PLE_DIM = 256
LN_EPS = 1e-5
DEEPNORM_ALPHA = (2 * DEPTH) ** 0.25
DEEPNORM_BETA = (8 * DEPTH) ** -0.25

kernel_name = 'hybrid_stickbreak_rwkv7_peer_block'


def _layer_norm(x, g, b):
    xf = x.astype(jnp.float32)
    mu = jnp.mean(xf, axis=-1, keepdims=True)
    var = jnp.mean(jnp.square(xf - mu), axis=-1, keepdims=True)
    return ((xf - mu) * lax.rsqrt(var + LN_EPS) * g + b).astype(x.dtype)


def _shift_lerp(z, mu):
    prev = jnp.pad(z, ((0, 0), (1, 0), (0, 0)))[:, :-1]
    return z + (prev - z) * mu


def _stick_breaking_attention(q, k, v):
    b, t, h, dh = q.shape
    n_blocks = t // SB_BLOCK
    scale = dh ** -0.5
    q_blocks = q.reshape(b, n_blocks, SB_BLOCK, h, dh).transpose(1, 0, 3, 2, 4)
    k_t = k.transpose(0, 2, 1, 3)
    v_t = v.transpose(0, 2, 1, 3)
    key_pos = jnp.arange(t)
    starts = jnp.arange(n_blocks) * SB_BLOCK

    def block(args):
        q_blk, start = args
        z = jnp.einsum('bhqd,bhkd->bhqk', q_blk, k_t).astype(jnp.float32) * scale
        q_pos = start + jnp.arange(SB_BLOCK)
        causal = key_pos[None, :] < q_pos[:, None]
        log_keep = jnp.where(causal, jax.nn.log_sigmoid(-z), 0.0)
        log_later = lax.cumsum(log_keep, axis=3, reverse=True) - log_keep
        w = jnp.where(causal, jnp.exp(jax.nn.log_sigmoid(z) + log_later), 0.0)
        return jnp.einsum('bhqk,bhkd->bhqd', w.astype(v_t.dtype), v_t)

    out = lax.map(block, (q_blocks, starts))
    return out.transpose(1, 0, 3, 2, 4).reshape(b, t, h * dh)


def _rwkv7_scan(r, log_w, k, v, a_vec, b_vec):
    bsz, _, h, n = r.shape

    def step(state, inp):
        r_t, lw_t, k_t, v_t, a_t, b_t = inp
        sa = jnp.einsum('bhvk,bhk->bhv', state, a_t)
        state = (state * jnp.exp(lw_t)[:, :, None, :]
                 + sa[..., None] * b_t[:, :, None, :]
                 + v_t[..., None] * k_t[:, :, None, :])
        return state, jnp.einsum('bhvk,bhk->bhv', state, r_t)

    xs = tuple(z.astype(jnp.float32).transpose(1, 0, 2, 3) for z in (r, log_w, k, v, a_vec, b_vec))
    _, y = lax.scan(step, jnp.zeros((bsz, h, n, n), jnp.float32), xs)
    return y.transpose(1, 0, 2, 3)


def _rwkv7_time_mix(r_in, k_in, v_in, c_in, mu, w0, w_lora1, w_lora2, a0, a_lora1, a_lora2,
                    g_lora1, g_lora2, k_k, k_a, r_k, lnx_w, lnx_b):
    bsz, t, _ = r_in.shape

    def heads(z):
        return z.reshape(bsz, t, RWKV_HEADS, RWKV_HEAD_DIM)

    r = _shift_lerp(r_in, mu[0])
    k = _shift_lerp(k_in, mu[1])
    v = _shift_lerp(v_in, mu[2])
    xw = _shift_lerp(c_in, mu[3])
    xa = _shift_lerp(c_in, mu[4])
    xg = _shift_lerp(c_in, mu[5])
    w = -jax.nn.softplus(-(w0 + jnp.tanh(xw @ w_lora1) @ w_lora2)) - 0.5
    log_decay = -jnp.exp(w.astype(jnp.float32))
    a = jax.nn.sigmoid(a0 + (xa @ a_lora1) @ a_lora2)
    g = jax.nn.sigmoid(xg @ g_lora1) @ g_lora2
    kk = heads(k * k_k).astype(jnp.float32)
    kk = kk / jnp.maximum(jnp.sqrt(jnp.sum(kk * kk, axis=-1, keepdims=True)), 1e-12)
    k = k * (1.0 + (a - 1.0) * k_a)
    rh, kh, vh, ah = heads(r), heads(k), heads(v), heads(a)
    y = _rwkv7_scan(rh, heads(log_decay), kh, vh, -kk, kk * ah)
    mean = jnp.mean(y, axis=-1, keepdims=True)
    var = jnp.mean(jnp.square(y - mean), axis=-1, keepdims=True)
    y = ((y - mean) * lax.rsqrt(var + GN_EPS)).reshape(bsz, t, RWKV_WIDTH) * lnx_w + lnx_b
    bonus = jnp.sum(rh * kh * r_k, axis=-1, keepdims=True) * vh
    y = y.astype(r_in.dtype) + bonus.reshape(bsz, t, RWKV_WIDTH)
    return y * g


def _peer_ffn(x, w_q, sub_keys, expert_u, expert_v):
    bsz, t, d = x.shape
    q = (x @ w_q).reshape(bsz, t, PEER_HEADS, 2, PEER_KEY_DIM // 2)
    s = jnp.einsum('bthcd,cnd->bthcn', q, sub_keys).astype(jnp.float32)
    top_s, top_i = lax.top_k(s, PEER_TOPK)
    cand_s = top_s[..., 0, :, None] + top_s[..., 1, None, :]
    cand_i = top_i[..., 0, :, None] * N_KEYS + top_i[..., 1, None, :]
    cand_s = cand_s.reshape(bsz, t, PEER_HEADS, PEER_TOPK * PEER_TOPK)
    cand_i = cand_i.reshape(bsz, t, PEER_HEADS, PEER_TOPK * PEER_TOPK)
    best_s, best_pos = lax.top_k(cand_s, PEER_TOPK)
    expert_idx = jnp.take_along_axis(cand_i, best_pos, axis=-1)
    gate = jax.nn.softmax(best_s, axis=-1)
    n_chunks = (bsz * t) // PEER_CHUNK
    n_sel = PEER_HEADS * PEER_TOPK
    xs = x.reshape(n_chunks, PEER_CHUNK, d)
    idx = expert_idx.reshape(n_chunks, PEER_CHUNK, n_sel)
    gs = gate.reshape(n_chunks, PEER_CHUNK, n_sel).astype(x.dtype)

    def chunk(args):
        xc, ic, gc = args
        u = jnp.take(expert_u, ic, axis=0)
        act = jax.nn.gelu(jnp.einsum('cd,cjd->cj', xc, u), approximate=False) * gc
        return jnp.einsum('cj,cjd->cd', act, jnp.take(expert_v, ic, axis=0))

    return lax.map(chunk, (xs, idx, gs)).reshape(bsz, t, d)


def setup_inputs(seed: int = 0) -> dict:
    key = jax.random.key(seed)
    ks = jax.random.split(key, 32)
    f32 = jnp.float32
    L = DEPTH

    def nrm(k, shape, scale):
        return jax.random.normal(k, shape, f32) * scale

    return {
        'x': nrm(ks[0], (BATCH, SEQ, D_MODEL), 1.0),
        'p': nrm(ks[1], (DEPTH, BATCH, SEQ, PLE_DIM), 1.0),
        'w_in': nrm(ks[2], (L, D_MODEL, IN_COLS), D_MODEL ** -0.5),
        'rwkv_mu': jax.random.uniform(ks[3], (L, N_SHIFT_MIX, RWKV_WIDTH), f32),
        'rwkv_w0': jax.random.uniform(ks[4], (L, RWKV_WIDTH), f32, -2.0, 1.0),
        'rwkv_w_lora1': nrm(ks[5], (L, RWKV_WIDTH, DECAY_LORA), RWKV_WIDTH ** -0.5),
        'rwkv_w_lora2': nrm(ks[6], (L, DECAY_LORA, RWKV_WIDTH), 0.1 * DECAY_LORA ** -0.5),
        'rwkv_a0': nrm(ks[7], (L, RWKV_WIDTH), 0.1),
        'rwkv_a_lora1': nrm(ks[8], (L, RWKV_WIDTH, ICLR_LORA), RWKV_WIDTH ** -0.5),
        'rwkv_a_lora2': nrm(ks[9], (L, ICLR_LORA, RWKV_WIDTH), 0.5 * ICLR_LORA ** -0.5),
        'rwkv_g_lora1': nrm(ks[10], (L, RWKV_WIDTH, GATE_LORA), RWKV_WIDTH ** -0.5),
        'rwkv_g_lora2': nrm(ks[11], (L, GATE_LORA, RWKV_WIDTH), GATE_LORA ** -0.5),
        'rwkv_k_k': 0.85 + nrm(ks[12], (L, RWKV_WIDTH), 0.05),
        'rwkv_k_a': 1.0 + nrm(ks[13], (L, RWKV_WIDTH), 0.05),
        'rwkv_r_k': nrm(ks[14], (L, RWKV_HEADS, RWKV_HEAD_DIM), 0.1),
        'rwkv_lnx_w': 1.0 + nrm(ks[15], (L, RWKV_WIDTH), 0.02),
        'rwkv_lnx_b': nrm(ks[16], (L, RWKV_WIDTH), 0.02),
        'w_up_a': nrm(ks[17], (L, SB_WIDTH, D_MODEL), SB_WIDTH ** -0.5),
        'w_up_b': nrm(ks[18], (L, RWKV_WIDTH, D_MODEL), RWKV_WIDTH ** -0.5),
        'w_o': nrm(ks[19], (L, D_MODEL, D_MODEL), DEEPNORM_BETA * D_MODEL ** -0.5),
        'ln1_w': 1.0 + nrm(ks[20], (L, D_MODEL), 0.02),
        'ln1_b': nrm(ks[21], (L, D_MODEL), 0.02),
        'peer_w_q': nrm(ks[22], (L, D_MODEL, PEER_HEADS * PEER_KEY_DIM), D_MODEL ** -0.5),
        'peer_sub_keys': nrm(ks[23], (L, 2, N_KEYS, PEER_KEY_DIM // 2), (PEER_KEY_DIM // 2) ** -0.5),
        'peer_u': nrm(ks[24], (L, N_EXPERTS, D_MODEL), D_MODEL ** -0.5),
        'peer_v': nrm(ks[25], (L, N_EXPERTS, D_MODEL), DEEPNORM_BETA),
        'ple_w_gate': nrm(ks[26], (L, D_MODEL, D_MODEL), D_MODEL ** -0.5),
        'ple_w_proj': nrm(ks[27], (L, PLE_DIM, D_MODEL), DEEPNORM_BETA * PLE_DIM ** -0.5),
        'ln2_w': 1.0 + nrm(ks[28], (L, D_MODEL), 0.02),
        'ln2_b': nrm(ks[29], (L, D_MODEL), 0.02),
    }


def reference(x, p, w_in, rwkv_mu, rwkv_w0, rwkv_w_lora1, rwkv_w_lora2, rwkv_a0, rwkv_a_lora1,
              rwkv_a_lora2, rwkv_g_lora1, rwkv_g_lora2, rwkv_k_k, rwkv_k_a, rwkv_r_k, rwkv_lnx_w,
              rwkv_lnx_b, w_up_a, w_up_b, w_o, ln1_w, ln1_b, peer_w_q, peer_sub_keys, peer_u, peer_v,
              ple_w_gate, ple_w_proj, ln2_w, ln2_b):
    bsz, t, _ = x.shape
    for i in range(DEPTH):
        proj = x @ w_in[i]
        sb_q, sb_k, sb_v, rw_r, rw_k, rw_v, rw_c, gate_a, gate_b = jnp.split(proj, IN_SPLITS, axis=-1)
        y_a = _stick_breaking_attention(
            sb_q.reshape(bsz, t, SB_HEADS, SB_HEAD_DIM),
            sb_k.reshape(bsz, t, SB_HEADS, SB_HEAD_DIM),
            sb_v.reshape(bsz, t, SB_HEADS, SB_HEAD_DIM)) @ w_up_a[i]
        y_b = _rwkv7_time_mix(
            rw_r, rw_k, rw_v, rw_c, rwkv_mu[i], rwkv_w0[i], rwkv_w_lora1[i], rwkv_w_lora2[i],
            rwkv_a0[i], rwkv_a_lora1[i], rwkv_a_lora2[i], rwkv_g_lora1[i], rwkv_g_lora2[i],
            rwkv_k_k[i], rwkv_k_a[i], rwkv_r_k[i], rwkv_lnx_w[i], rwkv_lnx_b[i]) @ w_up_b[i]
        mixed = (jax.nn.sigmoid(gate_a) * y_a + jax.nn.sigmoid(gate_b) * y_b) @ w_o[i]
        x = _layer_norm(DEEPNORM_ALPHA * x + mixed, ln1_w[i], ln1_b[i])
        ffn = _peer_ffn(x, peer_w_q[i], peer_sub_keys[i], peer_u[i], peer_v[i])
        ple = jax.nn.sigmoid(x @ ple_w_gate[i]) * (p[i] @ ple_w_proj[i])
        x = _layer_norm(DEEPNORM_ALPHA * x + ffn + ple, ln2_w[i], ln2_b[i])
    return x
```

```python
import functools

import jax
import jax.numpy as jnp
from jax import lax
from jax.experimental import pallas as pl
from jax.experimental.pallas import tpu as pltpu

F32 = jnp.float32
BF16 = jnp.bfloat16
I32 = jnp.int32

D_MODEL = 2048
HEAD_DIM = 64
N_HEADS = 16
MIX_WIDTH = N_HEADS * HEAD_DIM
PEER_HEADS = 8
PEER_TOPK = 16
N_KEYS = 128
N_EXPERTS = N_KEYS * N_KEYS
N_SEL = PEER_HEADS * PEER_TOPK
GN_EPS = 64e-5
LN_EPS = 1e-5
DEEPNORM_ALPHA = 2.0 ** 0.25

LANES = 128
VMEM_LIMIT = 56 * 1024 * 1024

MM_TM, MM_TN = 512, 1024
SB_BLK = 128
SB_DONE = -100.0
RW_TM = 256
RW_C = 64
RW_NCH = 2
MIX_TM = 256
ROUTE_TM = 256
PH_TM, PH_TN = 512, 1024
PG_TT = 64
PV_TM, PV_TK = 512, 2048
FIN_TM = 256

NN = (((1,), (0,)), ((), ()))
NT = (((1,), (1,)), ((), ()))
TN = (((0,), (0,)), ((), ()))


def _cparams(*sem):
    return pltpu.CompilerParams(dimension_semantics=tuple(sem), vmem_limit_bytes=VMEM_LIMIT)


def _dg(a, b, dims=NN):
    return lax.dot_general(a, b, dims, preferred_element_type=F32)


def _split2(x):
    hi = x.astype(BF16)
    lo = (x - hi.astype(F32)).astype(BF16)
    return hi, lo


def _split3(x):
    h1 = x.astype(BF16)
    r1 = x - h1.astype(F32)
    h2 = r1.astype(BF16)
    h3 = (r1 - h2.astype(F32)).astype(BF16)
    return h1, h2, h3


def _mm1(a, b, dims=NN):
    return _dg(a.astype(BF16), b.astype(BF16), dims)


def _mm3(a, b, dims=NN):
    ah, al = _split2(a)
    bh, bl = _split2(b)
    return _dg(ah, bh, dims) + _dg(ah, bl, dims) + _dg(al, bh, dims)


def _mm_exact_rhs(x, m_bf16):
    h1, h2, h3 = _split3(x)
    return _dg(h1, m_bf16) + _dg(h2, m_bf16) + _dg(h3, m_bf16)


def _mm_exact_lhs(m_bf16, x):
    h1, h2, h3 = _split3(x)
    return _dg(m_bf16, h1) + _dg(m_bf16, h2) + _dg(m_bf16, h3)


def _softplus(z):
    return jnp.maximum(z, 0.0) + jnp.log(1.0 + jnp.exp(-jnp.abs(z)))


def _sigmoid(z):
    return 1.0 / (1.0 + jnp.exp(-z))


def _layer_norm(x, g, b):
    mu = jnp.mean(x, axis=-1, keepdims=True)
    d = x - mu
    var = jnp.mean(d * d, axis=-1, keepdims=True)
    return d * lax.rsqrt(var + LN_EPS) * g + b


def _proj_kernel(a_ref, b_ref, o_ref):
    o_ref[...] = _dg(a_ref[...].astype(BF16), b_ref[...]).astype(o_ref.dtype)


def _proj(a, b, out_dtype):
    m, k = a.shape
    n = b.shape[1]
    tm, tn = min(MM_TM, m), min(MM_TN, n)
    return pl.pallas_call(
        _proj_kernel,
        grid=(m // tm, n // tn),
        in_specs=[pl.BlockSpec((tm, k), lambda i, j: (i, 0)),
                  pl.BlockSpec((k, tn), lambda i, j: (0, j))],
        out_specs=pl.BlockSpec((tm, tn), lambda i, j: (i, j)),
        out_shape=jax.ShapeDtypeStruct((m, n), out_dtype),
        compiler_params=_cparams("parallel", "parallel"),
    )(a, b)


def _sb_kernel(q_ref, k_ref, v_ref, o_ref, acc_ref, carry_ref):
    i = pl.program_id(1)
    blk = q_ref.shape[0]
    lane = lax.broadcasted_iota(I32, (1, LANES), 1)
    r_io = lax.broadcasted_iota(I32, (blk, blk), 0)
    c_io = lax.broadcasted_iota(I32, (blk, blk), 1)
    causal = c_io < r_io
    later_mat = jnp.where(r_io > c_io, 1.0, 0.0).astype(BF16)
    ones_mat = jnp.ones((blk, LANES), BF16)
    q = q_ref[...]
    scale = jnp.asarray(HEAD_DIM ** -0.5, BF16)
    zero = jnp.zeros_like(q)
    qh = [jnp.where((lane < HEAD_DIM) == (h == 0), q, zero) * scale for h in range(2)]
    acc_ref[...] = jnp.zeros_like(acc_ref)
    carry_ref[...] = jnp.zeros_like(carry_ref)

    def step(kb, diag):
        start = pl.multiple_of(kb * blk, blk)
        kblk = k_ref[pl.ds(start, blk), :]
        vblk = v_ref[pl.ds(start, blk), :]
        for h in range(2):
            z = _dg(qh[h], kblk, NT)
            sp = _softplus(z)
            log_keep = -sp
            if diag:
                log_keep = jnp.where(causal, log_keep, 0.0)
            hi, lo = _split2(log_keep)
            carry = carry_ref[h]
            log_later = _dg(hi, later_mat) + _dg(lo, later_mat) + carry
            w = jnp.exp(z - sp + log_later)
            if diag:
                w = jnp.where(causal, w, 0.0)
            acc_ref[h] += _dg(w.astype(BF16), vblk)
            carry_ref[h] = carry + _dg(hi, ones_mat) + _dg(lo, ones_mat)

    step(i, True)

    def cond(kb):
        return jnp.logical_and(kb >= 0, jnp.max(carry_ref[...]) > SB_DONE)

    def body(kb):
        step(kb, False)
        return kb - 1

    lax.while_loop(cond, body, i - 1)
    o_ref[...] = jnp.where(lane < HEAD_DIM, acc_ref[0], acc_ref[1]).astype(o_ref.dtype)


def _sb_attention(qkv):
    t = qkv.shape[0]
    blk = min(SB_BLK, t)
    n_pairs = MIX_WIDTH // LANES
    return pl.pallas_call(
        _sb_kernel,
        grid=(n_pairs, t // blk),
        in_specs=[pl.BlockSpec((blk, LANES), lambda hp, i: (i, hp)),
                  pl.BlockSpec((t, LANES), lambda hp, i: (0, n_pairs + hp)),
                  pl.BlockSpec((t, LANES), lambda hp, i: (0, 2 * n_pairs + hp))],
        out_specs=pl.BlockSpec((blk, LANES), lambda hp, i: (i, hp)),
        out_shape=jax.ShapeDtypeStruct((t, MIX_WIDTH), BF16),
        scratch_shapes=[pltpu.VMEM((2, blk, LANES), F32), pltpu.VMEM((2, blk, LANES), F32)],
        compiler_params=_cparams("parallel", "parallel"),
    )(qkv, qkv, qkv)


def _rwkv_pre_kernel(cur_ref, prev_ref, mu_ref, w0_ref, wl1_ref, wl2_ref, a0_ref, al1_ref, al2_ref,
                     gl1_ref, gl2_ref, r_ref, k_ref, v_ref, lw_ref, a_ref, g_ref):
    i = pl.program_id(0)
    tm = cur_ref.shape[0]
    wd = MIX_WIDTH
    first_row = lax.broadcasted_iota(I32, (tm, wd), 0) == 0
    keep_prev = jnp.where(i == 0, 0.0, 1.0)

    def shifted(col):
        z = cur_ref[:, col * wd:(col + 1) * wd]
        last = prev_ref[7:8, col * wd:(col + 1) * wd] * keep_prev
        prev = jnp.where(first_row, last, pltpu.roll(z, 1, 0))
        return z, prev - z

    z, d = shifted(0)
    r_ref[...] = z + d * mu_ref[0:1, :]
    z, d = shifted(1)
    k_ref[...] = z + d * mu_ref[1:2, :]
    z, d = shifted(2)
    v_ref[...] = z + d * mu_ref[2:3, :]
    z, d = shifted(3)
    xw = z + d * mu_ref[3:4, :]
    xa = z + d * mu_ref[4:5, :]
    xg = z + d * mu_ref[5:6, :]
    w = w0_ref[...] + _mm1(jnp.tanh(_mm1(xw, wl1_ref[...])), wl2_ref[...])
    w = -_softplus(-w) - 0.5
    lw_ref[...] = -jnp.exp(w)
    a_ref[...] = _sigmoid(a0_ref[...] + _mm1(_mm1(xa, al1_ref[...]), al2_ref[...]))
    g_ref[...] = _mm1(_sigmoid(_mm1(xg, gl1_ref[...])), gl2_ref[...])


def _pad_to(x, axis, size):
    pad = [(0, 0)] * x.ndim
    pad[axis] = (0, size - x.shape[axis])
    return jnp.pad(x, pad)


def _rwkv_pre(rw, mu, w0, wl1, wl2, a0, al1, al2, gl1, gl2):
    t = rw.shape[0]
    tm = min(RW_TM, t)
    wd = MIX_WIDTH
    lo = LANES

    def lora_pair(l1, l2):
        n = -(-l1.shape[1] // lo) * lo
        return _pad_to(l1, 1, n).astype(BF16), _pad_to(l2, 0, n).astype(BF16)

    wl1, wl2 = lora_pair(wl1, wl2)
    al1, al2 = lora_pair(al1, al2)
    gl1, gl2 = lora_pair(gl1, gl2)
    full = lambda x: pl.BlockSpec(x.shape, lambda i: (0,) * x.ndim)
    row = lambda: pl.BlockSpec((tm, wd), lambda i: (i, 0))
    consts = (mu, w0.reshape(1, wd), wl1, wl2, a0.reshape(1, wd), al1, al2, gl1, gl2)
    return pl.pallas_call(
        _rwkv_pre_kernel,
        grid=(t // tm,),
        in_specs=[pl.BlockSpec((tm, 4 * wd), lambda i: (i, 0)),
                  pl.BlockSpec((8, 4 * wd), lambda i: (jnp.maximum(i * (tm // 8) - 1, 0), 0))]
                 + [full(c) for c in consts],
        out_specs=[row() for _ in range(6)],
        out_shape=[jax.ShapeDtypeStruct((t, wd), F32) for _ in range(6)],
        compiler_params=_cparams("parallel"),
    )(rw, rw, *consts)


def _rwkv_chunk_kernel(r_ref, k_ref, v_ref, lw_ref, a_ref, kkp_ref, kap_ref, rkp_ref,
                       rp_ref, y0_ref, bon_ref, p_ref, q_ref):
    c_len = RW_C
    n2 = 2 * c_len
    n_chunks = r_ref.shape[0] // c_len
    r2 = lax.broadcasted_iota(I32, (n2, LANES), 0)
    c2 = lax.broadcasted_iota(I32, (n2, LANES), 1)
    same = (r2 < c_len) == (c2 < HEAD_DIM)
    t_r = jnp.where(r2 < c_len, r2, r2 - c_len)
    t_c = jnp.where(c2 < HEAD_DIM, c2, c2 - HEAD_DIM)
    strict = jnp.logical_and(same, t_r > t_c)
    incl = jnp.logical_and(same, t_r >= t_c)
    eye = jnp.where(r2 == c2, 1.0, 0.0).astype(F32)
    group_ones = jnp.where(same, 1.0, 0.0).astype(BF16)
    lr = lax.broadcasted_iota(I32, (c_len, c_len), 0)
    lc = lax.broadcasted_iota(I32, (c_len, c_len), 1)
    cum_mat = jnp.where(lr >= lc, 1.0, 0.0).astype(BF16)
    kkp, kap, rkp = kkp_ref[...], kap_ref[...], rkp_ref[...]

    def stack(z):
        return jnp.where(same, jnp.concatenate([z, z], axis=0), 0.0)

    def unstack(zs):
        return zs[:c_len] + zs[c_len:]

    for c in range(n_chunks):
        rows = pl.ds(c * c_len, c_len)
        r, kr, v, lw, a = r_ref[rows, :], k_ref[rows, :], v_ref[rows, :], lw_ref[rows, :], a_ref[rows, :]
        kk = kr * kkp
        kk = kk / jnp.maximum(jnp.sqrt(_mm_exact_rhs(kk * kk, group_ones)), 1e-12)
        km = kr * (1.0 + (a - 1.0) * kap)
        bv = kk * a
        cum = _mm_exact_lhs(cum_mat, lw)
        last = cum[c_len - 1:c_len, :]
        at = stack(-kk * jnp.exp(cum - lw))
        rt = stack(r * jnp.exp(cum))
        g_inv = jnp.exp(-cum)
        bt = stack(bv * g_inv)
        kt = stack(km * g_inv)
        g_end = jnp.exp(last - cum)
        bd = stack(bv * g_end)
        kd = stack(km * g_end)
        vs = stack(v)
        m_ab = jnp.where(strict, _mm1(at, bt, NT), 0.0)
        m_ak = jnp.where(strict, _mm1(at, kt, NT), 0.0)
        m_rb = jnp.where(incl, _mm1(rt, bt, NT), 0.0)
        m_rk = jnp.where(incl, _mm1(rt, kt, NT), 0.0)
        tinv = eye + m_ab
        npow = m_ab
        for _ in range(5):
            npow = _mm3(npow, npow)
            tinv = tinv + _mm3(tinv, npow)
        ap = _mm3(tinv, at)
        u0 = _mm3(tinv, _mm1(m_ak, vs))
        rp_ref[rows, :] = unstack(rt + _mm1(m_rb, ap))
        y0_ref[rows, :] = unstack(_mm1(m_rb, u0) + _mm1(m_rk, vs))
        prow = pl.ds(c * n2, n2)
        p_ref[prow, :] = eye * jnp.exp(last) + _mm1(bd, ap, TN)
        q_ref[prow, :] = _mm1(bd, u0, TN) + _mm1(kd, vs, TN)
        bon_ref[rows, :] = _mm_exact_rhs(r * km * rkp, group_ones) * v


def _rwkv_chunk(r, k, v, lw, a, k_k, k_a, r_k):
    t = r.shape[0]
    rb = min(RW_C * RW_NCH, t)
    n_pairs = MIX_WIDTH // LANES
    row = lambda: pl.BlockSpec((rb, LANES), lambda i, hp: (i, hp))
    par = lambda: pl.BlockSpec((1, LANES), lambda i, hp: (0, hp))
    mat = lambda: pl.BlockSpec((2 * rb, LANES), lambda i, hp: (i, hp))
    return pl.pallas_call(
        _rwkv_chunk_kernel,
        grid=(t // rb, n_pairs),
        in_specs=[row() for _ in range(5)] + [par() for _ in range(3)],
        out_specs=[row(), row(), row(), mat(), mat()],
        out_shape=[jax.ShapeDtypeStruct((t, MIX_WIDTH), F32)] * 3
                  + [jax.ShapeDtypeStruct((2 * t, MIX_WIDTH), F32)] * 2,
        compiler_params=_cparams("parallel", "parallel"),
    )(r, k, v, lw, a, k_k.reshape(1, -1), k_a.reshape(1, -1), r_k.reshape(1, -1))


def _rwkv_state_kernel(rp_ref, y0_ref, bon_ref, g_ref, p_ref, q_ref, lnw_ref, lnb_ref, o_ref, st_ref):
    @pl.when(pl.program_id(0) == 0)
    def _():
        st_ref[...] = jnp.zeros_like(st_ref)

    r2 = lax.broadcasted_iota(I32, (LANES, LANES), 0)
    c2 = lax.broadcasted_iota(I32, (LANES, LANES), 1)
    group_ones = jnp.where((r2 < HEAD_DIM) == (c2 < HEAD_DIM), 1.0, 0.0).astype(BF16)
    inv_n = 1.0 / HEAD_DIM
    for hp in range(MIX_WIDTH // LANES):
        ls = slice(hp * LANES, (hp + 1) * LANES)
        st = st_ref[hp]
        y = _mm3(rp_ref[:, ls], st) + y0_ref[:, ls]
        st_ref[hp] = _mm3(p_ref[:, ls], st) + q_ref[:, ls]
        mean = _mm_exact_rhs(y, group_ones) * inv_n
        d = y - mean
        var = _mm_exact_rhs(d * d, group_ones) * inv_n
        yn = d * lax.rsqrt(var + GN_EPS) * lnw_ref[:, ls] + lnb_ref[:, ls]
        o_ref[:, ls] = ((yn + bon_ref[:, ls]) * g_ref[:, ls]).astype(o_ref.dtype)


def _rwkv_state(rp, y0, bon, g, p, q, lnx_w, lnx_b):
    t = rp.shape[0]
    c_len = RW_C
    wd = MIX_WIDTH
    row = lambda: pl.BlockSpec((c_len, wd), lambda i: (i, 0))
    mat = lambda: pl.BlockSpec((2 * c_len, wd), lambda i: (i, 0))
    par = lambda: pl.BlockSpec((1, wd), lambda i: (0, 0))
    return pl.pallas_call(
        _rwkv_state_kernel,
        grid=(t // c_len,),
        in_specs=[row(), row(), row(), row(), mat(), mat(), par(), par()],
        out_specs=row(),
        out_shape=jax.ShapeDtypeStruct((t, wd), BF16),
        scratch_shapes=[pltpu.VMEM((wd // LANES, LANES, LANES), F32)],
        compiler_params=_cparams("arbitrary"),
    )(rp, y0, bon, g, p, q, lnx_w.reshape(1, wd), lnx_b.reshape(1, wd))


def _mix_kernel(ya_ref, yb_ref, gate_ref, x_ref, wa_ref, wb_ref, wo_ref, lnw_ref, lnb_ref, x1_ref, x1b_ref):
    ya = _dg(ya_ref[...], wa_ref[...])
    yb = _dg(yb_ref[...], wb_ref[...])
    ga = _sigmoid(gate_ref[:, :D_MODEL])
    gb = _sigmoid(gate_ref[:, D_MODEL:])
    mixed = _dg((ga * ya + gb * yb).astype(BF16), wo_ref[...])
    x1 = _layer_norm(DEEPNORM_ALPHA * x_ref[...] + mixed, lnw_ref[...], lnb_ref[...])
    x1_ref[...] = x1
    x1b_ref[...] = x1.astype(BF16)


def _const_spec(x):
    return pl.BlockSpec(x.shape, lambda *_: (0,) * x.ndim, pipeline_mode=pl.Buffered(1))


def _mix(ya, yb, gates, x, w_up_a, w_up_b, w_o, ln_w, ln_b):
    t = x.shape[0]
    tm = min(MIX_TM, t)
    consts = (w_up_a, w_up_b, w_o, ln_w.reshape(1, -1), ln_b.reshape(1, -1))
    return pl.pallas_call(
        _mix_kernel,
        grid=(t // tm,),
        in_specs=[pl.BlockSpec((tm, MIX_WIDTH), lambda i: (i, 0)),
                  pl.BlockSpec((tm, MIX_WIDTH), lambda i: (i, 0)),
                  pl.BlockSpec((tm, 2 * D_MODEL), lambda i: (i, 0)),
                  pl.BlockSpec((tm, D_MODEL), lambda i: (i, 0))] + [_const_spec(c) for c in consts],
        out_specs=[pl.BlockSpec((tm, D_MODEL), lambda i: (i, 0))] * 2,
        out_shape=[jax.ShapeDtypeStruct((t, D_MODEL), F32), jax.ShapeDtypeStruct((t, D_MODEL), BF16)],
        compiler_params=_cparams("parallel"),
    )(ya, yb, gates, x, *consts)


def _topk_rows(s, k):
    n = s.shape[0]
    iota = lax.broadcasted_iota(I32, s.shape, 0)
    vals, idxs = [], []
    for _ in range(k):
        m = jnp.max(s, axis=0, keepdims=True)
        ix = jnp.min(jnp.where(s == m, iota, n), axis=0, keepdims=True)
        vals.append(m)
        idxs.append(ix)
        s = jnp.where(iota == ix, -jnp.inf, s)
    return jnp.concatenate(vals, axis=0), jnp.concatenate(idxs, axis=0)


def _route_kernel(x_ref, wq_ref, keys_ref, e0_ref, e1_ref, gate_ref):
    q = _dg(x_ref[...], wq_ref[...])
    qb = q.astype(BF16)
    half = N_KEYS
    e0s, e1s, gates = [], [], []
    for h in range(PEER_HEADS):
        tops = []
        for c in range(2):
            qs = qb[:, (2 * h + c) * half:(2 * h + c + 1) * half]
            s = _dg(keys_ref[c], qs, NT)
            tops.append(_topk_rows(s, PEER_TOPK))
        (s0, i0), (s1, i1) = tops
        cand = jnp.concatenate([s0[a:a + 1, :] + s1 for a in range(PEER_TOPK)], axis=0)
        best, pos = _topk_rows(cand, PEER_TOPK)
        a_sel = jnp.right_shift(pos, 4)
        b_sel = jnp.bitwise_and(pos, PEER_TOPK - 1)
        e0 = jnp.zeros_like(pos)
        e1 = jnp.zeros_like(pos)
        for j in range(PEER_TOPK):
            e0 = jnp.where(a_sel == j, i0[j:j + 1, :], e0)
            e1 = jnp.where(b_sel == j, i1[j:j + 1, :], e1)
        ex = jnp.exp(best - best[0:1, :])
        gates.append(ex / jnp.sum(ex, axis=0, keepdims=True))
        e0s.append(e0)
        e1s.append(e1)
    e0_ref[...] = jnp.concatenate(e0s, axis=0).astype(F32).T.astype(I32)
    e1_ref[...] = jnp.concatenate(e1s, axis=0).astype(F32).T.astype(I32)
    gate_ref[...] = jnp.concatenate(gates, axis=0).T


def _route(x1b, w_q, sub_keys):
    t = x1b.shape[0]
    tm = min(ROUTE_TM, t)
    out = lambda: pl.BlockSpec((tm, N_SEL), lambda i: (i, 0))
    return pl.pallas_call(
        _route_kernel,
        grid=(t // tm,),
        in_specs=[pl.BlockSpec((tm, D_MODEL), lambda i: (i, 0)), _const_spec(w_q), _const_spec(sub_keys)],
        out_specs=[out(), out(), out()],
        out_shape=[jax.ShapeDtypeStruct((t, N_SEL), I32), jax.ShapeDtypeStruct((t, N_SEL), I32),
                   jax.ShapeDtypeStruct((t, N_SEL), F32)],
        compiler_params=_cparams("parallel"),
    )(x1b, w_q, sub_keys)


def _gelu_exact(x):
    return 0.5 * x * (1.0 + lax.erf(x * (2.0 ** -0.5)))


def _peer_score_kernel(x_ref, u_ref, e0_ref, e1_ref, gate_ref, o_ref, acc_ref):
    j = pl.program_id(1)

    @pl.when(j == 0)
    def _():
        acc_ref[...] = jnp.zeros_like(acc_ref)

    h = _dg(x_ref[...], u_ref[...], NT)
    e0, e1 = e0_ref[...], e1_ref[...]
    blocks = u_ref.shape[0] // N_KEYS
    acc = acc_ref[...]
    for b in range(blocks):
        hb = h[:, b * N_KEYS:(b + 1) * N_KEYS]
        picked = jnp.take_along_axis(hb, e1, axis=1)
        acc = acc + jnp.where(e0 == j * blocks + b, picked, 0.0)
    acc_ref[...] = acc

    @pl.when(j == pl.num_programs(1) - 1)
    def _():
        o_ref[...] = _gelu_exact(acc_ref[...]) * gate_ref[...]


def _peer_score(x1b, u_bf16, e0, e1, gate):
    t = x1b.shape[0]
    tm = min(PH_TM, t)
    tok = lambda: pl.BlockSpec((tm, N_SEL), lambda i, j: (i, 0))
    return pl.pallas_call(
        _peer_score_kernel,
        grid=(t // tm, N_EXPERTS // PH_TN),
        in_specs=[pl.BlockSpec((tm, D_MODEL), lambda i, j: (i, 0)),
                  pl.BlockSpec((PH_TN, D_MODEL), lambda i, j: (j, 0)), tok(), tok(), tok()],
        out_specs=tok(),
        out_shape=jax.ShapeDtypeStruct((t, N_SEL), F32),
        scratch_shapes=[pltpu.VMEM((tm, N_SEL), F32)],
        compiler_params=_cparams("parallel", "arbitrary"),
    )(x1b, u_bf16, e0, e1, gate)


def _peer_gate_kernel(e0_ref, e1_ref, act_ref, o_ref):
    sub = lax.broadcasted_iota(I32, (N_KEYS, N_SEL), 0)

    def body(t, carry):
        row = pl.ds(t, 1)
        left = jnp.where(e0_ref[row, :] == sub, act_ref[row, :], 0.0).astype(BF16)
        right = jnp.where(e1_ref[row, :] == sub, 1.0, 0.0).astype(BF16)
        o_ref[t] = _dg(left, right, NT).astype(o_ref.dtype)
        return carry

    lax.fori_loop(0, o_ref.shape[0], body, 0, unroll=4)


def _peer_gate(e0, e1, act):
    t = e0.shape[0]
    tt = min(PG_TT, t)
    tok = lambda: pl.BlockSpec((tt, N_SEL), lambda i: (i, 0))
    return pl.pallas_call(
        _peer_gate_kernel,
        grid=(t // tt,),
        in_specs=[tok(), tok(), tok()],
        out_specs=pl.BlockSpec((tt, N_KEYS, N_KEYS), lambda i: (i, 0, 0)),
        out_shape=jax.ShapeDtypeStruct((t, N_KEYS, N_KEYS), BF16),
        compiler_params=_cparams("parallel"),
    )(e0, e1, act)


def _peer_value_kernel(g_ref, v_ref, o_ref):
    @pl.when(pl.program_id(1) == 0)
    def _():
        o_ref[...] = jnp.zeros_like(o_ref)

    o_ref[...] += _dg(g_ref[...], v_ref[...])


def _peer_value(g2, v_bf16):
    t = g2.shape[0]
    tm = min(PV_TM, t)
    return pl.pallas_call(
        _peer_value_kernel,
        grid=(t // tm, N_EXPERTS // PV_TK),
        in_specs=[pl.BlockSpec((tm, PV_TK), lambda i, k: (i, k)),
                  pl.BlockSpec((PV_TK, D_MODEL), lambda i, k: (k, 0))],
        out_specs=pl.BlockSpec((tm, D_MODEL), lambda i, k: (i, 0)),
        out_shape=jax.ShapeDtypeStruct((t, D_MODEL), F32),
        compiler_params=_cparams("parallel", "arbitrary"),
    )(g2, v_bf16)


def _final_kernel(x1_ref, x1b_ref, ffn_ref, p_ref, wg_ref, wp_ref, lnw_ref, lnb_ref, o_ref):
    ple = _sigmoid(_dg(x1b_ref[...], wg_ref[...])) * _dg(p_ref[...].astype(BF16), wp_ref[...])
    o_ref[...] = _layer_norm(DEEPNORM_ALPHA * x1_ref[...] + ffn_ref[...] + ple, lnw_ref[...], lnb_ref[...])


def _final(x1, x1b, ffn, p, w_gate, w_proj, ln_w, ln_b):
    t = x1.shape[0]
    tm = min(FIN_TM, t)
    consts = (w_gate, w_proj, ln_w.reshape(1, -1), ln_b.reshape(1, -1))
    row = lambda w: pl.BlockSpec((tm, w), lambda i: (i, 0))
    return pl.pallas_call(
        _final_kernel,
        grid=(t // tm,),
        in_specs=[row(D_MODEL), row(D_MODEL), row(D_MODEL), row(p.shape[1])] + [_const_spec(c) for c in consts],
        out_specs=row(D_MODEL),
        out_shape=jax.ShapeDtypeStruct((t, D_MODEL), F32),
        compiler_params=_cparams("parallel"),
    )(x1, x1b, ffn, p, *consts)


def _layer(x, p, w_in, rwkv_mu, rwkv_w0, rwkv_w_lora1, rwkv_w_lora2, rwkv_a0, rwkv_a_lora1,
           rwkv_a_lora2, rwkv_g_lora1, rwkv_g_lora2, rwkv_k_k, rwkv_k_a, rwkv_r_k, rwkv_lnx_w,
           rwkv_lnx_b, w_up_a, w_up_b, w_o, ln1_w, ln1_b, peer_w_q, peer_sub_keys, peer_u, peer_v,
           ple_w_gate, ple_w_proj, ln2_w, ln2_b):
    bf = lambda w: w.astype(BF16)
    n_sb = 3 * MIX_WIDTH
    n_rw = 4 * MIX_WIDTH
    w_in = bf(w_in)
    qkv = _proj(x, w_in[:, :n_sb], BF16)
    rw = _proj(x, w_in[:, n_sb:n_sb + n_rw], F32)
    gates = _proj(x, w_in[:, n_sb + n_rw:], F32)
    y_a = _sb_attention(qkv)
    r, k, v, lw, a, g = _rwkv_pre(rw, rwkv_mu, rwkv_w0, rwkv_w_lora1, rwkv_w_lora2, rwkv_a0,
                                  rwkv_a_lora1, rwkv_a_lora2, rwkv_g_lora1, rwkv_g_lora2)
    rp, y0, bon, pm, qm = _rwkv_chunk(r, k, v, lw, a, rwkv_k_k, rwkv_k_a, rwkv_r_k.reshape(-1))
    y_b = _rwkv_state(rp, y0, bon, g, pm, qm, rwkv_lnx_w, rwkv_lnx_b)
    x1, x1b = _mix(y_a, y_b, gates, x, bf(w_up_a), bf(w_up_b), bf(w_o), ln1_w, ln1_b)
    e0, e1, gate = _route(x1b, bf(peer_w_q), bf(peer_sub_keys))
    act = _peer_score(x1b, bf(peer_u), e0, e1, gate)
    gmat = _peer_gate(e0, e1, act)
    ffn = _peer_value(gmat.reshape(gmat.shape[0], N_EXPERTS), bf(peer_v))
    return _final(x1, x1b, ffn, p, bf(ple_w_gate), bf(ple_w_proj), ln2_w, ln2_b)


def kernel(x, p, w_in, rwkv_mu, rwkv_w0, rwkv_w_lora1, rwkv_w_lora2, rwkv_a0, rwkv_a_lora1, rwkv_a_lora2, rwkv_g_lora1, rwkv_g_lora2, rwkv_k_k, rwkv_k_a, rwkv_r_k, rwkv_lnx_w, rwkv_lnx_b, w_up_a, w_up_b, w_o, ln1_w, ln1_b, peer_w_q, peer_sub_keys, peer_u, peer_v, ple_w_gate, ple_w_proj, ln2_w, ln2_b):
    bsz, t, d = x.shape
    depth = w_in.shape[0]
    xs = x.reshape(bsz * t, d)
    assert bsz == 1, "token shift / attention / scan treat the flattened rows as one sequence"
    for i in range(depth):
        xs = _layer(xs, p[i].reshape(bsz * t, -1), w_in[i], rwkv_mu[i], rwkv_w0[i], rwkv_w_lora1[i],
                    rwkv_w_lora2[i], rwkv_a0[i], rwkv_a_lora1[i], rwkv_a_lora2[i], rwkv_g_lora1[i],
                    rwkv_g_lora2[i], rwkv_k_k[i], rwkv_k_a[i], rwkv_r_k[i], rwkv_lnx_w[i], rwkv_lnx_b[i],
                    w_up_a[i], w_up_b[i], w_o[i], ln1_w[i], ln1_b[i], peer_w_q[i], peer_sub_keys[i],
                    peer_u[i], peer_v[i], ple_w_gate[i], ple_w_proj[i], ln2_w[i], ln2_b[i])
    return xs.reshape(bsz, t, d)
```

```python
import functools

import jax
import jax.numpy as jnp
from jax import lax
from jax.experimental import pallas as pl
from jax.experimental.pallas import tpu as pltpu

F32 = jnp.float32
BF16 = jnp.bfloat16
I32 = jnp.int32

D_MODEL = 2048
HEAD_DIM = 64
N_HEADS = 16
MIX_WIDTH = N_HEADS * HEAD_DIM
PEER_HEADS = 8
PEER_TOPK = 16
N_KEYS = 128
N_EXPERTS = N_KEYS * N_KEYS
N_SEL = PEER_HEADS * PEER_TOPK
GN_EPS = 64e-5
LN_EPS = 1e-5
DEEPNORM_ALPHA = 2.0 ** 0.25

LANES = 128
VMEM_LIMIT = 56 * 1024 * 1024

MM_TM, MM_TN = 512, 1024
SB_BLK = 128
SB_WIN = 2 * LANES
SB_DONE = -88.0
RW_TM = 256
RW_C = 64
RW_NCH = 4
RW_NCS = 4
MIX_TM = 256
ROUTE_TM = 256
PH_TM, PH_TN = 512, 1024
PH_SUB = 256
PG_TT = 128
PG_GRP = 16
PV_TM, PV_NE0 = 512, 16
FIN_TM = 256

assert 2 * RW_C == LANES and RW_C == HEAD_DIM

NN = (((1,), (0,)), ((), ()))
NT = (((1,), (1,)), ((), ()))
TN = (((0,), (0,)), ((), ()))


def _cparams(*sem):
    return pltpu.CompilerParams(dimension_semantics=tuple(sem), vmem_limit_bytes=VMEM_LIMIT)


def _dg(a, b, dims=NN):
    return lax.dot_general(a, b, dims, preferred_element_type=F32)


def _split2(x):
    hi = x.astype(BF16)
    lo = (x - hi.astype(F32)).astype(BF16)
    return hi, lo


def _mm1(a, b, dims=NN):
    return _dg(a.astype(BF16), b.astype(BF16), dims)


def _mm_sum_rhs(x, m_bf16):
    hi, lo = _split2(x)
    return _dg(hi, m_bf16) + _dg(lo, m_bf16)


def _mm_sum_lhs(m_bf16, x):
    hi, lo = _split2(x)
    return _dg(m_bf16, hi) + _dg(m_bf16, lo)


def _softplus(z):
    return jnp.maximum(z, 0.0) + jnp.log(1.0 + jnp.exp(-jnp.abs(z)))


def _sigmoid(z):
    return 1.0 / (1.0 + jnp.exp(-z))


def _layer_norm(x, g, b):
    mu = jnp.mean(x, axis=-1, keepdims=True)
    d = x - mu
    var = jnp.mean(d * d, axis=-1, keepdims=True)
    return d * lax.rsqrt(var + LN_EPS) * g + b


def _proj_kernel(a_ref, b_ref, o_ref):
    o_ref[...] = _dg(a_ref[...].astype(BF16), b_ref[...]).astype(o_ref.dtype)


def _proj(a, b, out_dtype):
    m, k = a.shape
    n = b.shape[1]
    tm, tn = min(MM_TM, m), min(MM_TN, n)
    return pl.pallas_call(
        _proj_kernel,
        grid=(m // tm, n // tn),
        in_specs=[pl.BlockSpec((tm, k), lambda i, j: (i, 0)),
                  pl.BlockSpec((k, tn), lambda i, j: (0, j))],
        out_specs=pl.BlockSpec((tm, tn), lambda i, j: (i, j)),
        out_shape=jax.ShapeDtypeStruct((m, n), out_dtype),
        compiler_params=_cparams("parallel", "parallel"),
    )(a, b)


def _sb_kernel(q_ref, k_ref, v_ref, o_ref, acc_ref, carry_ref):
    i = pl.program_id(1)
    blk = q_ref.shape[0]
    win = SB_WIN
    lane = lax.broadcasted_iota(I32, (1, LANES), 1)
    q_pos = i * blk + lax.broadcasted_iota(I32, (blk, win), 0)
    col = lax.broadcasted_iota(I32, (blk, win), 1)
    r_io = lax.broadcasted_iota(I32, (win, win), 0)
    c_io = lax.broadcasted_iota(I32, (win, win), 1)
    later_mat = jnp.where(r_io > c_io, 1.0, 0.0).astype(BF16)
    ones_mat = jnp.ones((win, LANES), BF16)
    q = q_ref[...]
    scale = jnp.asarray(HEAD_DIM ** -0.5, BF16)
    zero = jnp.zeros_like(q)
    qh = [jnp.where((lane < HEAD_DIM) == (h == 0), q, zero) * scale for h in range(2)]
    acc_ref[...] = jnp.zeros_like(acc_ref)
    carry_ref[...] = jnp.zeros_like(carry_ref)

    def step(end):
        start = pl.multiple_of(jnp.maximum(end - win, 0), blk)
        kwin = k_ref[pl.ds(start, win), :]
        vwin = v_ref[pl.ds(start, win), :]
        valid = (start + col) < jnp.minimum(q_pos, end)
        for h in range(2):
            z = _dg(qh[h], kwin, NT)
            sp = _softplus(z)
            log_keep = jnp.where(valid, -sp, 0.0)
            hi, lo = _split2(log_keep)
            carry = carry_ref[h]
            log_later = _dg(hi, later_mat) + _dg(lo, later_mat) + jnp.concatenate([carry] * (win // LANES), axis=1)
            w = jnp.where(valid, jnp.exp(z - sp + log_later), 0.0)
            acc_ref[h] += _dg(w.astype(BF16), vwin)
            carry_ref[h] = carry + _dg(hi, ones_mat) + _dg(lo, ones_mat)
        return start

    def cond(end):
        return jnp.logical_and(end > 0, jnp.max(carry_ref[...]) > SB_DONE)

    lax.while_loop(cond, step, step((i + 1) * blk))
    o_ref[...] = jnp.where(lane < HEAD_DIM, acc_ref[0], acc_ref[1]).astype(o_ref.dtype)


def _sb_attention(qkv):
    t = qkv.shape[0]
    blk = min(SB_BLK, t)
    n_pairs = MIX_WIDTH // LANES
    return pl.pallas_call(
        _sb_kernel,
        grid=(n_pairs, t // blk),
        in_specs=[pl.BlockSpec((blk, LANES), lambda hp, i: (i, hp)),
                  pl.BlockSpec((t, LANES), lambda hp, i: (0, n_pairs + hp)),
                  pl.BlockSpec((t, LANES), lambda hp, i: (0, 2 * n_pairs + hp))],
        out_specs=pl.BlockSpec((blk, LANES), lambda hp, i: (i, hp)),
        out_shape=jax.ShapeDtypeStruct((t, MIX_WIDTH), BF16),
        scratch_shapes=[pltpu.VMEM((2, blk, LANES), F32), pltpu.VMEM((2, blk, LANES), F32)],
        compiler_params=_cparams("parallel", "parallel"),
    )(qkv, qkv, qkv)


def _rwkv_pre_kernel(cur_ref, prev_ref, mu_ref, w0_ref, wl1_ref, wl2_ref, a0_ref, al1_ref, al2_ref,
                     gl1_ref, gl2_ref, r_ref, k_ref, v_ref, lw_ref, a_ref, g_ref):
    i = pl.program_id(0)
    tm = cur_ref.shape[0]
    wd = MIX_WIDTH
    first_row = lax.broadcasted_iota(I32, (tm, wd), 0) == 0
    keep_prev = jnp.where(i == 0, 0.0, 1.0)

    def shifted(col):
        z = cur_ref[:, col * wd:(col + 1) * wd]
        last = prev_ref[7:8, col * wd:(col + 1) * wd] * keep_prev
        prev = jnp.where(first_row, last, pltpu.roll(z, 1, 0))
        return z, prev - z

    z, d = shifted(0)
    r_ref[...] = z + d * mu_ref[0:1, :]
    z, d = shifted(1)
    k_ref[...] = z + d * mu_ref[1:2, :]
    z, d = shifted(2)
    v_ref[...] = z + d * mu_ref[2:3, :]
    z, d = shifted(3)
    xw = z + d * mu_ref[3:4, :]
    xa = z + d * mu_ref[4:5, :]
    xg = z + d * mu_ref[5:6, :]
    w = w0_ref[...] + _mm1(jnp.tanh(_mm1(xw, wl1_ref[...])), wl2_ref[...])
    w = -_softplus(-w) - 0.5
    lw_ref[...] = -jnp.exp(w)
    a_ref[...] = _sigmoid(a0_ref[...] + _mm1(_mm1(xa, al1_ref[...]), al2_ref[...]))
    g_ref[...] = _mm1(_sigmoid(_mm1(xg, gl1_ref[...])), gl2_ref[...])


def _pad_to(x, axis, size):
    pad = [(0, 0)] * x.ndim
    pad[axis] = (0, size - x.shape[axis])
    return jnp.pad(x, pad)


def _rwkv_pre(rw, mu, w0, wl1, wl2, a0, al1, al2, gl1, gl2):
    t = rw.shape[0]
    tm = min(RW_TM, t)
    wd = MIX_WIDTH
    lo = LANES

    def lora_pair(l1, l2):
        n = -(-l1.shape[1] // lo) * lo
        return _pad_to(l1, 1, n).astype(BF16), _pad_to(l2, 0, n).astype(BF16)

    wl1, wl2 = lora_pair(wl1, wl2)
    al1, al2 = lora_pair(al1, al2)
    gl1, gl2 = lora_pair(gl1, gl2)
    full = lambda x: pl.BlockSpec(x.shape, lambda i: (0,) * x.ndim)
    row = lambda: pl.BlockSpec((tm, wd), lambda i: (i, 0))
    consts = (mu, w0.reshape(1, wd), wl1, wl2, a0.reshape(1, wd), al1, al2, gl1, gl2)
    return pl.pallas_call(
        _rwkv_pre_kernel,
        grid=(t // tm,),
        in_specs=[pl.BlockSpec((tm, 4 * wd), lambda i: (i, 0)),
                  pl.BlockSpec((8, 4 * wd), lambda i: (jnp.maximum(i * (tm // 8) - 1, 0), 0))]
                 + [full(c) for c in consts],
        out_specs=[row() for _ in range(6)],
        out_shape=[jax.ShapeDtypeStruct((t, wd), F32) for _ in range(6)],
        compiler_params=_cparams("parallel"),
    )(rw, rw, *consts)


def _rwkv_chunk_kernel(r_ref, k_ref, v_ref, lw_ref, a_ref, kkp_ref, kap_ref, rkp_ref,
                       rp_ref, y0_ref, bon_ref, p_ref, q_ref):
    c_len = RW_C
    n2 = 2 * c_len
    n_chunks = r_ref.shape[0] // c_len
    r2 = lax.broadcasted_iota(I32, (n2, LANES), 0)
    c2 = lax.broadcasted_iota(I32, (n2, LANES), 1)
    same = (r2 < c_len) == (c2 < HEAD_DIM)
    t_r = jnp.where(r2 < c_len, r2, r2 - c_len)
    t_c = jnp.where(c2 < HEAD_DIM, c2, c2 - HEAD_DIM)
    strict = jnp.logical_and(same, t_r > t_c)
    incl = jnp.logical_and(same, t_r >= t_c)
    eye = jnp.where(r2 == c2, 1.0, 0.0).astype(F32)
    group_ones = jnp.where(same, 1.0, 0.0).astype(BF16)
    lr = lax.broadcasted_iota(I32, (c_len, c_len), 0)
    lc = lax.broadcasted_iota(I32, (c_len, c_len), 1)
    cum_mat = jnp.where(lr >= lc, 1.0, 0.0).astype(BF16)
    kkp, kap, rkp = kkp_ref[...], kap_ref[...], rkp_ref[...]

    def stack(z):
        return jnp.where(same, jnp.concatenate([z, z], axis=0), 0.0)

    def unstack(zs):
        return zs[:c_len] + zs[c_len:]

    for c in range(n_chunks):
        rows = pl.ds(c * c_len, c_len)
        r, kr, v, lw, a = r_ref[rows, :], k_ref[rows, :], v_ref[rows, :], lw_ref[rows, :], a_ref[rows, :]
        kk = kr * kkp
        kk = kk / jnp.maximum(jnp.sqrt(_mm_sum_rhs(kk * kk, group_ones)), 1e-12)
        km = kr * (1.0 + (a - 1.0) * kap)
        bv = kk * a
        cum = _mm_sum_lhs(cum_mat, lw)
        last = cum[c_len - 1:c_len, :]
        at = stack(-kk * jnp.exp(cum - lw))
        rt = stack(r * jnp.exp(cum))
        g_inv = jnp.exp(-cum)
        g_end = jnp.exp(last - cum)
        at_b, rt_b = at.astype(BF16), rt.astype(BF16)
        btkt = jnp.concatenate([stack(bv * g_inv), stack(km * g_inv)], axis=0).astype(BF16)
        bd_b = stack(bv * g_end).astype(BF16)
        kd_b = stack(km * g_end).astype(BF16)
        vs_b = stack(v).astype(BF16)
        m_a = _dg(at_b, btkt, NT)
        m_r = _dg(rt_b, btkt, NT)
        m_ab = jnp.where(strict, m_a[:, :n2], 0.0)
        m_ak = jnp.where(strict, m_a[:, n2:], 0.0).astype(BF16)
        m_rb = jnp.where(incl, m_r[:, :n2], 0.0).astype(BF16)
        m_rk = jnp.where(incl, m_r[:, n2:], 0.0).astype(BF16)
        tinv = eye + m_ab
        npow = m_ab.astype(BF16)
        for _ in range(5):
            sq = _dg(npow, npow)
            npow = sq.astype(BF16)
            tinv = tinv + _dg(tinv.astype(BF16), npow)
        au = _dg(tinv.astype(BF16),
                 jnp.concatenate([at_b, _dg(m_ak, vs_b).astype(BF16)], axis=1))
        au_b = au.astype(BF16)
        ry = _dg(m_rb, au_b)
        rp_ref[rows, :] = unstack(rt + ry[:, :LANES]).astype(rp_ref.dtype)
        y0_ref[rows, :] = unstack(ry[:, LANES:] + _dg(m_rk, vs_b))
        pq = _dg(bd_b, au_b, TN)
        prow = pl.ds(c * n2, n2)
        p_ref[prow, :] = (eye * jnp.exp(last) + pq[:, :LANES]).astype(p_ref.dtype)
        q_ref[prow, :] = pq[:, LANES:] + _dg(kd_b, vs_b, TN)
        bon_ref[rows, :] = _mm_sum_rhs(r * km * rkp, group_ones) * v


def _rwkv_chunk(r, k, v, lw, a, k_k, k_a, r_k):
    t = r.shape[0]
    rb = min(RW_C * RW_NCH, t)
    n_pairs = MIX_WIDTH // LANES
    row = lambda: pl.BlockSpec((rb, LANES), lambda i, hp: (i, hp))
    par = lambda: pl.BlockSpec((1, LANES), lambda i, hp: (0, hp))
    mat = lambda: pl.BlockSpec((2 * rb, LANES), lambda i, hp: (i, hp))
    return pl.pallas_call(
        _rwkv_chunk_kernel,
        grid=(t // rb, n_pairs),
        in_specs=[row() for _ in range(5)] + [par() for _ in range(3)],
        out_specs=[row(), row(), row(), mat(), mat()],
        out_shape=[jax.ShapeDtypeStruct((t, MIX_WIDTH), BF16), jax.ShapeDtypeStruct((t, MIX_WIDTH), F32),
                   jax.ShapeDtypeStruct((t, MIX_WIDTH), F32), jax.ShapeDtypeStruct((2 * t, MIX_WIDTH), BF16),
                   jax.ShapeDtypeStruct((2 * t, MIX_WIDTH), F32)],
        compiler_params=_cparams("parallel", "parallel"),
    )(r, k, v, lw, a, k_k.reshape(1, -1), k_a.reshape(1, -1), r_k.reshape(1, -1))


def _rwkv_state_kernel(rp_ref, y0_ref, p_ref, q_ref, o_ref, st_ref):
    @pl.when(pl.program_id(0) == 0)
    def _():
        st_ref[...] = jnp.zeros_like(st_ref)

    c_len = RW_C
    for hp in range(MIX_WIDTH // LANES):
        ls = slice(hp * LANES, (hp + 1) * LANES)
        st = st_ref[hp]
        for c in range(rp_ref.shape[0] // c_len):
            rows = slice(c * c_len, (c + 1) * c_len)
            mrows = slice(2 * c * c_len, 2 * (c + 1) * c_len)
            st_b = st.astype(BF16)
            o_ref[rows, ls] = _dg(rp_ref[rows, ls], st_b) + y0_ref[rows, ls]
            st = _dg(p_ref[mrows, ls], st_b) + q_ref[mrows, ls]
        st_ref[hp] = st


def _rwkv_state(rp, y0, p, q):
    t = rp.shape[0]
    rb = min(RW_C * RW_NCS, t)
    wd = MIX_WIDTH
    row = lambda: pl.BlockSpec((rb, wd), lambda i: (i, 0))
    mat = lambda: pl.BlockSpec((2 * rb, wd), lambda i: (i, 0))
    return pl.pallas_call(
        _rwkv_state_kernel,
        grid=(t // rb,),
        in_specs=[row(), row(), mat(), mat()],
        out_specs=row(),
        out_shape=jax.ShapeDtypeStruct((t, wd), F32),
        scratch_shapes=[pltpu.VMEM((wd // LANES, LANES, LANES), F32)],
        compiler_params=_cparams("arbitrary"),
    )(rp, y0, p, q)


def _mix_kernel(ya_ref, y_ref, bon_ref, g_ref, gate_ref, x_ref, gnw_ref, gnb_ref, wa_ref, wb_ref, wo_ref,
                lnw_ref, lnb_ref, x1_ref, x1b_ref):
    r2 = lax.broadcasted_iota(I32, (LANES, LANES), 0)
    c2 = lax.broadcasted_iota(I32, (LANES, LANES), 1)
    group_ones = jnp.where((r2 < HEAD_DIM) == (c2 < HEAD_DIM), 1.0, 0.0).astype(BF16)
    inv_n = 1.0 / HEAD_DIM
    pieces = []
    for hp in range(MIX_WIDTH // LANES):
        ls = slice(hp * LANES, (hp + 1) * LANES)
        y = y_ref[:, ls]
        d = y - _mm_sum_rhs(y, group_ones) * inv_n
        var = _mm_sum_rhs(d * d, group_ones) * inv_n
        yn = d * lax.rsqrt(var + GN_EPS) * gnw_ref[:, ls] + gnb_ref[:, ls]
        pieces.append(((yn + bon_ref[:, ls]) * g_ref[:, ls]).astype(BF16))
    ya = _dg(ya_ref[...], wa_ref[...])
    yb = _dg(jnp.concatenate(pieces, axis=1), wb_ref[...])
    ga = _sigmoid(gate_ref[:, :D_MODEL])
    gb = _sigmoid(gate_ref[:, D_MODEL:])
    mixed = _dg((ga * ya + gb * yb).astype(BF16), wo_ref[...])
    x1 = _layer_norm(DEEPNORM_ALPHA * x_ref[...] + mixed, lnw_ref[...], lnb_ref[...])
    x1_ref[...] = x1
    x1b_ref[...] = x1.astype(BF16)


def _const_spec(x):
    return pl.BlockSpec(x.shape, lambda *_: (0,) * x.ndim, pipeline_mode=pl.Buffered(1))


def _mix(ya, y, bon, g, gates, x, gn_w, gn_b, w_up_a, w_up_b, w_o, ln_w, ln_b):
    t = x.shape[0]
    tm = min(MIX_TM, t)
    consts = (gn_w.reshape(1, -1), gn_b.reshape(1, -1), w_up_a, w_up_b, w_o, ln_w.reshape(1, -1), ln_b.reshape(1, -1))
    row = lambda w: pl.BlockSpec((tm, w), lambda i: (i, 0))
    return pl.pallas_call(
        _mix_kernel,
        grid=(t // tm,),
        in_specs=[row(MIX_WIDTH), row(MIX_WIDTH), row(MIX_WIDTH), row(MIX_WIDTH), row(2 * D_MODEL), row(D_MODEL)]
                 + [_const_spec(c) for c in consts],
        out_specs=[pl.BlockSpec((tm, D_MODEL), lambda i: (i, 0))] * 2,
        out_shape=[jax.ShapeDtypeStruct((t, D_MODEL), F32), jax.ShapeDtypeStruct((t, D_MODEL), BF16)],
        compiler_params=_cparams("parallel"),
    )(ya, y, bon, g, gates, x, *consts)


def _topk_rows(s, ids, k):
    big = jnp.asarray(1e9, F32)
    vals, idxs = [], []
    for _ in range(k):
        m = jnp.max(s, axis=0, keepdims=True)
        ix = jnp.min(jnp.where(s == m, ids, big), axis=0, keepdims=True)
        vals.append(m)
        idxs.append(ix)
        s = jnp.where(ids == ix, -jnp.inf, s)
    return jnp.concatenate(vals, axis=0), jnp.concatenate(idxs, axis=0)


def _route_kernel(x_ref, wq_ref, keys_ref, e0_ref, e1_ref, gate_ref):
    tm = x_ref.shape[0]
    topk = PEER_TOPK
    q = _dg(x_ref[...], wq_ref[...])
    qb = q.astype(BF16)
    half = N_KEYS
    key_ids = lax.broadcasted_iota(I32, (N_KEYS, tm), 0).astype(F32)
    sub8 = lax.broadcasted_iota(I32, (8, tm), 0).astype(F32)
    sub16 = lax.broadcasted_iota(I32, (topk, tm), 0).astype(F32)
    cand_ids = jnp.concatenate([sub16] + [a * topk + sub8 for a in range(1, 8)] + [(sub8 + 8.0) * topk], axis=0)
    e0s, e1s, gates = [], [], []
    for h in range(PEER_HEADS):
        tops = []
        for c in range(2):
            qs = qb[:, (2 * h + c) * half:(2 * h + c + 1) * half]
            s = _dg(keys_ref[c], qs, NT)
            tops.append(_topk_rows(s, key_ids, topk))
        (s0, i0), (s1, i1) = tops
        cand = jnp.concatenate([s0[0:1, :] + s1] + [s0[a:a + 1, :] + s1[:8, :] for a in range(1, 8)]
                               + [s0[8:, :] + s1[0:1, :]], axis=0)
        best, pos = _topk_rows(cand, cand_ids, topk)
        a_sel = jnp.floor(pos * (1.0 / topk))
        b_sel = pos - a_sel * topk
        e0 = jnp.zeros_like(pos)
        e1 = jnp.zeros_like(pos)
        for j in range(topk):
            e0 = jnp.where(a_sel == j, i0[j:j + 1, :], e0)
            e1 = jnp.where(b_sel == j, i1[j:j + 1, :], e1)
        ex = jnp.exp(best - best[0:1, :])
        gates.append(ex / jnp.sum(ex, axis=0, keepdims=True))
        e0s.append(e0)
        e1s.append(e1)
    e0_ref[...] = jnp.concatenate(e0s, axis=0).T.astype(I32)
    e1_ref[...] = jnp.concatenate(e1s, axis=0).T.astype(I32)
    gate_ref[...] = jnp.concatenate(gates, axis=0).T


def _route(x1b, w_q, sub_keys):
    t = x1b.shape[0]
    tm = min(ROUTE_TM, t)
    out = lambda: pl.BlockSpec((tm, N_SEL), lambda i: (i, 0))
    return pl.pallas_call(
        _route_kernel,
        grid=(t // tm,),
        in_specs=[pl.BlockSpec((tm, D_MODEL), lambda i: (i, 0)), _const_spec(w_q), _const_spec(sub_keys)],
        out_specs=[out(), out(), out()],
        out_shape=[jax.ShapeDtypeStruct((t, N_SEL), I32), jax.ShapeDtypeStruct((t, N_SEL), I32),
                   jax.ShapeDtypeStruct((t, N_SEL), F32)],
        compiler_params=_cparams("parallel"),
    )(x1b, w_q, sub_keys)


def _gelu_exact(x):
    return 0.5 * x * (1.0 + lax.erf(x * (2.0 ** -0.5)))


def _peer_score_kernel(x_ref, u_ref, e0_ref, e1_ref, gate_ref, o_ref, acc_ref):
    j = pl.program_id(1)

    @pl.when(j == 0)
    def _():
        acc_ref[...] = jnp.zeros_like(acc_ref)

    x = x_ref[...]
    e0, e1 = e0_ref[...], e1_ref[...]
    blocks = u_ref.shape[0] // N_KEYS
    per_dot = PH_SUB // N_KEYS
    acc = acc_ref[...]
    for s in range(u_ref.shape[0] // PH_SUB):
        h = _dg(x, u_ref[s * PH_SUB:(s + 1) * PH_SUB, :], NT)
        for b in range(per_dot):
            picked = jnp.take_along_axis(h[:, b * N_KEYS:(b + 1) * N_KEYS], e1, axis=1)
            acc = acc + jnp.where(e0 == j * blocks + s * per_dot + b, picked, 0.0)
    acc_ref[...] = acc

    @pl.when(j == pl.num_programs(1) - 1)
    def _():
        o_ref[...] = _gelu_exact(acc_ref[...]) * gate_ref[...]


def _peer_score(x1b, u_bf16, e0, e1, gate):
    t = x1b.shape[0]
    tm = min(PH_TM, t)
    tok = lambda: pl.BlockSpec((tm, N_SEL), lambda i, j: (i, 0))
    return pl.pallas_call(
        _peer_score_kernel,
        grid=(t // tm, N_EXPERTS // PH_TN),
        in_specs=[pl.BlockSpec((tm, D_MODEL), lambda i, j: (i, 0)),
                  pl.BlockSpec((PH_TN, D_MODEL), lambda i, j: (j, 0)), tok(), tok(), tok()],
        out_specs=tok(),
        out_shape=jax.ShapeDtypeStruct((t, N_SEL), F32),
        scratch_shapes=[pltpu.VMEM((tm, N_SEL), F32)],
        compiler_params=_cparams("parallel", "arbitrary"),
    )(x1b, u_bf16, e0, e1, gate)


def _peer_gate_kernel(e0_ref, e1_ref, act_ref, o_ref):
    sub = lax.broadcasted_iota(I32, (N_KEYS, N_SEL), 0)

    def body(grp, carry):
        mats = []
        for u in range(PG_GRP):
            row = pl.ds(grp * PG_GRP + u, 1)
            left = jnp.where(e0_ref[row, :] == sub, act_ref[row, :], 0.0).astype(BF16)
            right = jnp.where(e1_ref[row, :] == sub, 1.0, 0.0).astype(BF16)
            mats.append(_dg(left, right, NT))
        o_ref[grp] = jnp.swapaxes(jnp.stack(mats, axis=0), 0, 1).astype(o_ref.dtype)
        return carry

    lax.fori_loop(0, o_ref.shape[0], body, 0)


def _peer_gate(e0, e1, act):
    t = e0.shape[0]
    tt = min(PG_TT, t)
    tok = lambda: pl.BlockSpec((tt, N_SEL), lambda i: (i, 0))
    return pl.pallas_call(
        _peer_gate_kernel,
        grid=(t // tt,),
        in_specs=[tok(), tok(), tok()],
        out_specs=pl.BlockSpec((tt // PG_GRP, N_KEYS, PG_GRP, N_KEYS), lambda i: (i, 0, 0, 0)),
        out_shape=jax.ShapeDtypeStruct((t // PG_GRP, N_KEYS, PG_GRP, N_KEYS), BF16),
        compiler_params=_cparams("parallel"),
    )(e0, e1, act)


def _peer_value_kernel(g_ref, v_ref, o_ref):
    @pl.when(pl.program_id(1) == 0)
    def _():
        o_ref[...] = jnp.zeros_like(o_ref)

    tm = o_ref.shape[0]
    acc = None
    for e in range(0, g_ref.shape[1], 2):
        lhs = jnp.concatenate([g_ref[:, e, :, :].reshape(tm, N_KEYS), g_ref[:, e + 1, :, :].reshape(tm, N_KEYS)], axis=1)
        part = _dg(lhs, v_ref[e * N_KEYS:(e + 2) * N_KEYS, :])
        acc = part if acc is None else acc + part
    o_ref[...] += acc


def _peer_value(g4, v_bf16):
    t = g4.shape[0] * PG_GRP
    tm = min(PV_TM, t)
    return pl.pallas_call(
        _peer_value_kernel,
        grid=(t // tm, N_KEYS // PV_NE0),
        in_specs=[pl.BlockSpec((tm // PG_GRP, PV_NE0, PG_GRP, N_KEYS), lambda i, k: (i, k, 0, 0)),
                  pl.BlockSpec((PV_NE0 * N_KEYS, D_MODEL), lambda i, k: (k, 0))],
        out_specs=pl.BlockSpec((tm, D_MODEL), lambda i, k: (i, 0)),
        out_shape=jax.ShapeDtypeStruct((t, D_MODEL), F32),
        compiler_params=_cparams("parallel", "arbitrary"),
    )(g4, v_bf16)


def _final_kernel(x1_ref, x1b_ref, ffn_ref, p_ref, wg_ref, wp_ref, lnw_ref, lnb_ref, o_ref):
    ple = _sigmoid(_dg(x1b_ref[...], wg_ref[...])) * _dg(p_ref[...].astype(BF16), wp_ref[...])
    o_ref[...] = _layer_norm(DEEPNORM_ALPHA * x1_ref[...] + ffn_ref[...] + ple, lnw_ref[...], lnb_ref[...])


def _final(x1, x1b, ffn, p, w_gate, w_proj, ln_w, ln_b):
    t = x1.shape[0]
    tm = min(FIN_TM, t)
    consts = (w_gate, w_proj, ln_w.reshape(1, -1), ln_b.reshape(1, -1))
    row = lambda w: pl.BlockSpec((tm, w), lambda i: (i, 0))
    return pl.pallas_call(
        _final_kernel,
        grid=(t // tm,),
        in_specs=[row(D_MODEL), row(D_MODEL), row(D_MODEL), row(p.shape[1])] + [_const_spec(c) for c in consts],
        out_specs=row(D_MODEL),
        out_shape=jax.ShapeDtypeStruct((t, D_MODEL), F32),
        compiler_params=_cparams("parallel"),
    )(x1, x1b, ffn, p, *consts)


def _layer(x, p, w_in, rwkv_mu, rwkv_w0, rwkv_w_lora1, rwkv_w_lora2, rwkv_a0, rwkv_a_lora1,
           rwkv_a_lora2, rwkv_g_lora1, rwkv_g_lora2, rwkv_k_k, rwkv_k_a, rwkv_r_k, rwkv_lnx_w,
           rwkv_lnx_b, w_up_a, w_up_b, w_o, ln1_w, ln1_b, peer_w_q, peer_sub_keys, peer_u, peer_v,
           ple_w_gate, ple_w_proj, ln2_w, ln2_b):
    bf = lambda w: w.astype(BF16)
    n_sb = 3 * MIX_WIDTH
    n_rw = 4 * MIX_WIDTH
    w_in = bf(w_in)
    qkv = _proj(x, w_in[:, :n_sb], BF16)
    rw = _proj(x, w_in[:, n_sb:n_sb + n_rw], F32)
    gates = _proj(x, w_in[:, n_sb + n_rw:], F32)
    y_a = _sb_attention(qkv)
    r, k, v, lw, a, g = _rwkv_pre(rw, rwkv_mu, rwkv_w0, rwkv_w_lora1, rwkv_w_lora2, rwkv_a0,
                                  rwkv_a_lora1, rwkv_a_lora2, rwkv_g_lora1, rwkv_g_lora2)
    rp, y0, bon, pm, qm = _rwkv_chunk(r, k, v, lw, a, rwkv_k_k, rwkv_k_a, rwkv_r_k.reshape(-1))
    y_b = _rwkv_state(rp, y0, pm, qm)
    x1, x1b = _mix(y_a, y_b, bon, g, gates, x, rwkv_lnx_w, rwkv_lnx_b, bf(w_up_a), bf(w_up_b), bf(w_o),
                   ln1_w, ln1_b)
    e0, e1, gate = _route(x1b, bf(peer_w_q), bf(peer_sub_keys))
    act = _peer_score(x1b, bf(peer_u), e0, e1, gate)
    ffn = _peer_value(_peer_gate(e0, e1, act), bf(peer_v))
    return _final(x1, x1b, ffn, p, bf(ple_w_gate), bf(ple_w_proj), ln2_w, ln2_b)


def kernel(x, p, w_in, rwkv_mu, rwkv_w0, rwkv_w_lora1, rwkv_w_lora2, rwkv_a0, rwkv_a_lora1, rwkv_a_lora2, rwkv_g_lora1, rwkv_g_lora2, rwkv_k_k, rwkv_k_a, rwkv_r_k, rwkv_lnx_w, rwkv_lnx_b, w_up_a, w_up_b, w_o, ln1_w, ln1_b, peer_w_q, peer_sub_keys, peer_u, peer_v, ple_w_gate, ple_w_proj, ln2_w, ln2_b):
    bsz, t, d = x.shape
    depth = w_in.shape[0]
    xs = x.reshape(bsz * t, d)
    assert bsz == 1, "token shift / attention / scan treat the flattened rows as one sequence"
    for i in range(depth):
        xs = _layer(xs, p[i].reshape(bsz * t, -1), w_in[i], rwkv_mu[i], rwkv_w0[i], rwkv_w_lora1[i],
                    rwkv_w_lora2[i], rwkv_a0[i], rwkv_a_lora1[i], rwkv_a_lora2[i], rwkv_g_lora1[i],
                    rwkv_g_lora2[i], rwkv_k_k[i], rwkv_k_a[i], rwkv_r_k[i], rwkv_lnx_w[i], rwkv_lnx_b[i],
                    w_up_a[i], w_up_b[i], w_o[i], ln1_w[i], ln1_b[i], peer_w_q[i], peer_sub_keys[i],
                    peer_u[i], peer_v[i], ple_w_gate[i], ple_w_proj[i], ln2_w[i], ln2_b[i])
    return xs.reshape(bsz, t, d)
```

```python
import functools

import jax
import jax.numpy as jnp
from jax import lax
from jax.experimental import pallas as pl
from jax.experimental.pallas import tpu as pltpu

F32 = jnp.float32
BF16 = jnp.bfloat16
I32 = jnp.int32

D_MODEL = 2048
HEAD_DIM = 64
N_HEADS = 16
MIX_WIDTH = N_HEADS * HEAD_DIM
PEER_HEADS = 8
PEER_TOPK = 16
N_KEYS = 128
N_EXPERTS = N_KEYS * N_KEYS
N_SEL = PEER_HEADS * PEER_TOPK
GN_EPS = 64e-5
LN_EPS = 1e-5
DEEPNORM_ALPHA = 2.0 ** 0.25

LANES = 128
VMEM_LIMIT = 56 * 1024 * 1024

MM_TM, MM_TN = 1024, 1024
SB_BLK = 128
SB_SUBS = 2
SB_WIN = 2 * LANES
SB_DONE = -88.0
RW_TM = 256
RW_C = 64
RW_NCH = 4
RW_NCS = 4
MIX_TM = 256
ROUTE_TM = 256
PH_TM, PH_TN = 512, 2048
PH_SUB = 256
PG_TT = 128
PG_GRP = 16
PV_TM, PV_NE0 = 512, 16
FIN_TM = 256

assert 2 * RW_C == LANES and RW_C == HEAD_DIM

NN = (((1,), (0,)), ((), ()))
NT = (((1,), (1,)), ((), ()))
TN = (((0,), (0,)), ((), ()))


def _cparams(*sem):
    return pltpu.CompilerParams(dimension_semantics=tuple(sem), vmem_limit_bytes=VMEM_LIMIT)


def _dg(a, b, dims=NN):
    return lax.dot_general(a, b, dims, preferred_element_type=F32)


def _split2(x):
    hi = x.astype(BF16)
    lo = (x - hi.astype(F32)).astype(BF16)
    return hi, lo


def _mm1(a, b, dims=NN):
    return _dg(a.astype(BF16), b.astype(BF16), dims)


def _mm_sum_rhs(x, m_bf16):
    hi, lo = _split2(x)
    return _dg(hi, m_bf16) + _dg(lo, m_bf16)


def _mm_sum_lhs(m_bf16, x):
    hi, lo = _split2(x)
    return _dg(m_bf16, hi) + _dg(m_bf16, lo)


def _softplus(z):
    return jnp.maximum(z, 0.0) + jnp.log(1.0 + jnp.exp(-jnp.abs(z)))


def _sigmoid(z):
    return 1.0 / (1.0 + jnp.exp(-z))


def _layer_norm(x, g, b):
    mu = jnp.mean(x, axis=-1, keepdims=True)
    d = x - mu
    var = jnp.mean(d * d, axis=-1, keepdims=True)
    return d * lax.rsqrt(var + LN_EPS) * g + b


def _proj_kernel(a_ref, b_ref, o_ref):
    o_ref[...] = _dg(a_ref[...].astype(BF16), b_ref[...]).astype(o_ref.dtype)


def _proj(a, b, out_dtype):
    m, k = a.shape
    n = b.shape[1]
    tm, tn = min(MM_TM, m), min(MM_TN, n)
    return pl.pallas_call(
        _proj_kernel,
        grid=(m // tm, n // tn),
        in_specs=[pl.BlockSpec((tm, k), lambda i, j: (i, 0)),
                  pl.BlockSpec((k, tn), lambda i, j: (0, j))],
        out_specs=pl.BlockSpec((tm, tn), lambda i, j: (i, j)),
        out_shape=jax.ShapeDtypeStruct((m, n), out_dtype),
        compiler_params=_cparams("parallel", "parallel"),
    )(a, b)


def _sb_kernel(q_ref, k_ref, v_ref, o_ref, acc_ref, carry_ref):
    i = pl.program_id(1)
    blk = SB_BLK
    n_sub = q_ref.shape[0] // blk
    win = SB_WIN
    lane = lax.broadcasted_iota(I32, (1, LANES), 1)
    row = lax.broadcasted_iota(I32, (blk, win), 0)
    col = lax.broadcasted_iota(I32, (blk, win), 1)
    r_io = lax.broadcasted_iota(I32, (win, win), 0)
    c_io = lax.broadcasted_iota(I32, (win, win), 1)
    later_mat = jnp.where(r_io > c_io, 1.0, 0.0).astype(BF16)
    ones_mat = jnp.ones((win, LANES), BF16)
    scale = jnp.asarray(HEAD_DIM ** -0.5, BF16)
    chains = [(s, h) for s in range(n_sub) for h in range(2)]
    qh = []
    for s, h in chains:
        q = q_ref[s * blk:(s + 1) * blk, :]
        qh.append(jnp.where((lane < HEAD_DIM) == (h == 0), q, jnp.zeros_like(q)) * scale)
    first_end = [(i * n_sub + s + 1) * blk for s in range(n_sub)]
    acc_ref[...] = jnp.zeros_like(acc_ref)
    carry_ref[...] = jnp.zeros_like(carry_ref)

    def step(it):
        kwin, vwin, valid = [], [], []
        for s in range(n_sub):
            end = first_end[s] - it * win
            start = pl.multiple_of(jnp.maximum(end - win, 0), blk)
            kwin.append(k_ref[pl.ds(start, win), :])
            vwin.append(v_ref[pl.ds(start, win), :])
            valid.append((start + col) < jnp.minimum(first_end[s] - blk + row, end))
        z = [_dg(qh[c], kwin[s], NT) for c, (s, h) in enumerate(chains)]
        sp = [_softplus(x) for x in z]
        parts = [_split2(jnp.where(valid[s], -sp[c], 0.0)) for c, (s, h) in enumerate(chains)]
        carry = [carry_ref[c] for c in range(len(chains))]
        log_later = [_dg(hi, later_mat) + _dg(lo, later_mat) + jnp.concatenate([carry[c]] * (win // LANES), axis=1)
                     for c, (hi, lo) in enumerate(parts)]
        w = [jnp.where(valid[s], jnp.exp(z[c] - sp[c] + log_later[c]), 0.0).astype(BF16)
             for c, (s, h) in enumerate(chains)]
        pv = [_dg(w[c], vwin[s]) for c, (s, h) in enumerate(chains)]
        rowsum = [_dg(hi, ones_mat) + _dg(lo, ones_mat) for hi, lo in parts]
        for c in range(len(chains)):
            acc_ref[c] += pv[c]
            carry_ref[c] = carry[c] + rowsum[c]
        return it + 1

    def cond(it):
        alive = None
        for s in range(n_sub):
            more = jnp.logical_and(first_end[s] - it * win > 0, jnp.max(carry_ref[2 * s:2 * s + 2]) > SB_DONE)
            alive = more if alive is None else jnp.logical_or(alive, more)
        return alive

    lax.while_loop(cond, step, step(0))
    for s in range(n_sub):
        o_ref[s * blk:(s + 1) * blk, :] = jnp.where(lane < HEAD_DIM, acc_ref[2 * s], acc_ref[2 * s + 1]).astype(o_ref.dtype)


def _sb_attention(qkv):
    t = qkv.shape[0]
    blk = SB_BLK * SB_SUBS
    n_pairs = MIX_WIDTH // LANES
    return pl.pallas_call(
        _sb_kernel,
        grid=(n_pairs, t // blk),
        in_specs=[pl.BlockSpec((blk, LANES), lambda hp, i: (i, hp)),
                  pl.BlockSpec((t, LANES), lambda hp, i: (0, n_pairs + hp)),
                  pl.BlockSpec((t, LANES), lambda hp, i: (0, 2 * n_pairs + hp))],
        out_specs=pl.BlockSpec((blk, LANES), lambda hp, i: (i, hp)),
        out_shape=jax.ShapeDtypeStruct((t, MIX_WIDTH), BF16),
        scratch_shapes=[pltpu.VMEM((2 * SB_SUBS, SB_BLK, LANES), F32), pltpu.VMEM((2 * SB_SUBS, SB_BLK, LANES), F32)],
        compiler_params=_cparams("parallel", "parallel"),
    )(qkv, qkv, qkv)


def _rwkv_pre_kernel(cur_ref, prev_ref, mu_ref, w0_ref, wl1_ref, wl2_ref, a0_ref, al1_ref, al2_ref,
                     gl1_ref, gl2_ref, r_ref, k_ref, v_ref, lw_ref, a_ref, g_ref):
    i = pl.program_id(0)
    tm = cur_ref.shape[0]
    wd = MIX_WIDTH
    first_row = lax.broadcasted_iota(I32, (tm, wd), 0) == 0
    keep_prev = jnp.where(i == 0, 0.0, 1.0)

    def shifted(col):
        z = cur_ref[:, col * wd:(col + 1) * wd]
        last = prev_ref[7:8, col * wd:(col + 1) * wd] * keep_prev
        prev = jnp.where(first_row, last, pltpu.roll(z, 1, 0))
        return z, prev - z

    z, d = shifted(0)
    r_ref[...] = z + d * mu_ref[0:1, :]
    z, d = shifted(1)
    k_ref[...] = z + d * mu_ref[1:2, :]
    z, d = shifted(2)
    v_ref[...] = z + d * mu_ref[2:3, :]
    z, d = shifted(3)
    xw = z + d * mu_ref[3:4, :]
    xa = z + d * mu_ref[4:5, :]
    xg = z + d * mu_ref[5:6, :]
    w = w0_ref[...] + _mm1(jnp.tanh(_mm1(xw, wl1_ref[...])), wl2_ref[...])
    w = -_softplus(-w) - 0.5
    lw_ref[...] = -jnp.exp(w)
    a_ref[...] = _sigmoid(a0_ref[...] + _mm1(_mm1(xa, al1_ref[...]), al2_ref[...]))
    g_ref[...] = _mm1(_sigmoid(_mm1(xg, gl1_ref[...])), gl2_ref[...])


def _pad_to(x, axis, size):
    pad = [(0, 0)] * x.ndim
    pad[axis] = (0, size - x.shape[axis])
    return jnp.pad(x, pad)


def _rwkv_pre(rw, mu, w0, wl1, wl2, a0, al1, al2, gl1, gl2):
    t = rw.shape[0]
    tm = min(RW_TM, t)
    wd = MIX_WIDTH
    lo = LANES

    def lora_pair(l1, l2):
        n = -(-l1.shape[1] // lo) * lo
        return _pad_to(l1, 1, n).astype(BF16), _pad_to(l2, 0, n).astype(BF16)

    wl1, wl2 = lora_pair(wl1, wl2)
    al1, al2 = lora_pair(al1, al2)
    gl1, gl2 = lora_pair(gl1, gl2)
    full = lambda x: pl.BlockSpec(x.shape, lambda i: (0,) * x.ndim)
    row = lambda: pl.BlockSpec((tm, wd), lambda i: (i, 0))
    consts = (mu, w0.reshape(1, wd), wl1, wl2, a0.reshape(1, wd), al1, al2, gl1, gl2)
    return pl.pallas_call(
        _rwkv_pre_kernel,
        grid=(t // tm,),
        in_specs=[pl.BlockSpec((tm, 4 * wd), lambda i: (i, 0)),
                  pl.BlockSpec((8, 4 * wd), lambda i: (jnp.maximum(i * (tm // 8) - 1, 0), 0))]
                 + [full(c) for c in consts],
        out_specs=[row() for _ in range(6)],
        out_shape=[jax.ShapeDtypeStruct((t, wd), F32) for _ in range(6)],
        compiler_params=_cparams("parallel"),
    )(rw, rw, *consts)


def _rwkv_chunk_kernel(r_ref, k_ref, v_ref, lw_ref, a_ref, kkp_ref, kap_ref, rkp_ref,
                       rp_ref, y0_ref, bon_ref, p_ref, q_ref):
    c_len = RW_C
    n2 = 2 * c_len
    n_chunks = r_ref.shape[0] // c_len
    r2 = lax.broadcasted_iota(I32, (n2, LANES), 0)
    c2 = lax.broadcasted_iota(I32, (n2, LANES), 1)
    same = (r2 < c_len) == (c2 < HEAD_DIM)
    t_r = jnp.where(r2 < c_len, r2, r2 - c_len)
    t_c = jnp.where(c2 < HEAD_DIM, c2, c2 - HEAD_DIM)
    strict = jnp.logical_and(same, t_r > t_c)
    incl = jnp.logical_and(same, t_r >= t_c)
    eye = jnp.where(r2 == c2, 1.0, 0.0).astype(F32)
    group_ones = jnp.where(same, 1.0, 0.0).astype(BF16)
    lr = lax.broadcasted_iota(I32, (c_len, c_len), 0)
    lc = lax.broadcasted_iota(I32, (c_len, c_len), 1)
    cum_mat = jnp.where(lr >= lc, 1.0, 0.0).astype(BF16)
    kkp, kap, rkp = kkp_ref[...], kap_ref[...], rkp_ref[...]

    def stack(z):
        return jnp.where(same, jnp.concatenate([z, z], axis=0), 0.0)

    def unstack(zs):
        return zs[:c_len] + zs[c_len:]

    cs = range(n_chunks)
    rows = [pl.ds(c * c_len, c_len) for c in cs]
    r = [r_ref[rw, :] for rw in rows]
    kr = [k_ref[rw, :] for rw in rows]
    v = [v_ref[rw, :] for rw in rows]
    lw = [lw_ref[rw, :] for rw in rows]
    a = [a_ref[rw, :] for rw in rows]
    kk = [x * kkp for x in kr]
    cum = [_mm_sum_lhs(cum_mat, x) for x in lw]
    ssq = [_mm_sum_rhs(x * x, group_ones) for x in kk]
    km = [kr[c] * (1.0 + (a[c] - 1.0) * kap) for c in cs]
    bsum = [_mm_sum_rhs(r[c] * km[c] * rkp, group_ones) for c in cs]
    kk = [kk[c] / jnp.maximum(jnp.sqrt(ssq[c]), 1e-12) for c in cs]
    bv = [kk[c] * a[c] for c in cs]
    last = [x[c_len - 1:c_len, :] for x in cum]
    rt = [stack(r[c] * jnp.exp(cum[c])) for c in cs]
    at_b = [stack(-kk[c] * jnp.exp(cum[c] - lw[c])).astype(BF16) for c in cs]
    rt_b = [x.astype(BF16) for x in rt]
    g_inv = [jnp.exp(-x) for x in cum]
    btkt = [jnp.concatenate([stack(bv[c] * g_inv[c]), stack(km[c] * g_inv[c])], axis=0).astype(BF16) for c in cs]
    m_a = [_dg(at_b[c], btkt[c], NT) for c in cs]
    m_r = [_dg(rt_b[c], btkt[c], NT) for c in cs]
    vs_b = [stack(x).astype(BF16) for x in v]
    g_end = [jnp.exp(last[c] - cum[c]) for c in cs]
    bd_b = [stack(bv[c] * g_end[c]).astype(BF16) for c in cs]
    kd_b = [stack(km[c] * g_end[c]).astype(BF16) for c in cs]
    m_ab = [jnp.where(strict, x[:, :n2], 0.0) for x in m_a]
    m_ak = [jnp.where(strict, x[:, n2:], 0.0).astype(BF16) for x in m_a]
    m_rb = [jnp.where(incl, x[:, :n2], 0.0).astype(BF16) for x in m_r]
    m_rk = [jnp.where(incl, x[:, n2:], 0.0).astype(BF16) for x in m_r]
    akv = [_dg(m_ak[c], vs_b[c]) for c in cs]
    rkv = [_dg(m_rk[c], vs_b[c]) for c in cs]
    kdv = [_dg(kd_b[c], vs_b[c], TN) for c in cs]
    tinv = [eye + x for x in m_ab]
    npow = [x.astype(BF16) for x in m_ab]
    for _ in range(5):
        npow = [_dg(x, x).astype(BF16) for x in npow]
        tinv = [tinv[c] + _dg(tinv[c].astype(BF16), npow[c]) for c in cs]
    au_b = [_dg(tinv[c].astype(BF16), jnp.concatenate([at_b[c], akv[c].astype(BF16)], axis=1)).astype(BF16)
            for c in cs]
    ry = [_dg(m_rb[c], au_b[c]) for c in cs]
    pq = [_dg(bd_b[c], au_b[c], TN) for c in cs]
    for c in cs:
        prow = pl.ds(c * n2, n2)
        rp_ref[rows[c], :] = unstack(rt[c] + ry[c][:, :LANES]).astype(rp_ref.dtype)
        y0_ref[rows[c], :] = unstack(ry[c][:, LANES:] + rkv[c])
        p_ref[prow, :] = (eye * jnp.exp(last[c]) + pq[c][:, :LANES]).astype(p_ref.dtype)
        q_ref[prow, :] = pq[c][:, LANES:] + kdv[c]
        bon_ref[rows[c], :] = bsum[c] * v[c]


def _rwkv_chunk(r, k, v, lw, a, k_k, k_a, r_k):
    t = r.shape[0]
    rb = min(RW_C * RW_NCH, t)
    n_pairs = MIX_WIDTH // LANES
    row = lambda: pl.BlockSpec((rb, LANES), lambda i, hp: (i, hp))
    par = lambda: pl.BlockSpec((1, LANES), lambda i, hp: (0, hp))
    mat = lambda: pl.BlockSpec((2 * rb, LANES), lambda i, hp: (i, hp))
    return pl.pallas_call(
        _rwkv_chunk_kernel,
        grid=(t // rb, n_pairs),
        in_specs=[row() for _ in range(5)] + [par() for _ in range(3)],
        out_specs=[row(), row(), row(), mat(), mat()],
        out_shape=[jax.ShapeDtypeStruct((t, MIX_WIDTH), BF16), jax.ShapeDtypeStruct((t, MIX_WIDTH), F32),
                   jax.ShapeDtypeStruct((t, MIX_WIDTH), F32), jax.ShapeDtypeStruct((2 * t, MIX_WIDTH), BF16),
                   jax.ShapeDtypeStruct((2 * t, MIX_WIDTH), F32)],
        compiler_params=_cparams("parallel", "parallel"),
    )(r, k, v, lw, a, k_k.reshape(1, -1), k_a.reshape(1, -1), r_k.reshape(1, -1))


def _rwkv_state_kernel(rp_ref, y0_ref, p_ref, q_ref, o_ref, st_ref):
    @pl.when(pl.program_id(0) == 0)
    def _():
        st_ref[...] = jnp.zeros_like(st_ref)

    c_len = RW_C
    pairs = range(MIX_WIDTH // LANES)
    lanes = [slice(hp * LANES, (hp + 1) * LANES) for hp in pairs]
    st = [st_ref[hp] for hp in pairs]
    for c in range(rp_ref.shape[0] // c_len):
        rows = slice(c * c_len, (c + 1) * c_len)
        mrows = slice(2 * c * c_len, 2 * (c + 1) * c_len)
        st_b = [x.astype(BF16) for x in st]
        st = [_dg(p_ref[mrows, lanes[hp]], st_b[hp]) + q_ref[mrows, lanes[hp]] for hp in pairs]
        for hp in pairs:
            o_ref[rows, lanes[hp]] = _dg(rp_ref[rows, lanes[hp]], st_b[hp]) + y0_ref[rows, lanes[hp]]
    for hp in pairs:
        st_ref[hp] = st[hp]


def _rwkv_state(rp, y0, p, q):
    t = rp.shape[0]
    rb = min(RW_C * RW_NCS, t)
    wd = MIX_WIDTH
    row = lambda: pl.BlockSpec((rb, wd), lambda i: (i, 0))
    mat = lambda: pl.BlockSpec((2 * rb, wd), lambda i: (i, 0))
    return pl.pallas_call(
        _rwkv_state_kernel,
        grid=(t // rb,),
        in_specs=[row(), row(), mat(), mat()],
        out_specs=row(),
        out_shape=jax.ShapeDtypeStruct((t, wd), F32),
        scratch_shapes=[pltpu.VMEM((wd // LANES, LANES, LANES), F32)],
        compiler_params=_cparams("arbitrary"),
    )(rp, y0, p, q)


def _mix_kernel(ya_ref, y_ref, bon_ref, g_ref, gate_ref, x_ref, gnw_ref, gnb_ref, wa_ref, wb_ref, wo_ref,
                lnw_ref, lnb_ref, x1_ref, x1b_ref):
    r2 = lax.broadcasted_iota(I32, (LANES, LANES), 0)
    c2 = lax.broadcasted_iota(I32, (LANES, LANES), 1)
    group_ones = jnp.where((r2 < HEAD_DIM) == (c2 < HEAD_DIM), 1.0, 0.0).astype(BF16)
    inv_n = 1.0 / HEAD_DIM
    pieces = []
    for hp in range(MIX_WIDTH // LANES):
        ls = slice(hp * LANES, (hp + 1) * LANES)
        y = y_ref[:, ls]
        d = y - _mm_sum_rhs(y, group_ones) * inv_n
        var = _mm_sum_rhs(d * d, group_ones) * inv_n
        yn = d * lax.rsqrt(var + GN_EPS) * gnw_ref[:, ls] + gnb_ref[:, ls]
        pieces.append(((yn + bon_ref[:, ls]) * g_ref[:, ls]).astype(BF16))
    ya = _dg(ya_ref[...], wa_ref[...])
    yb = _dg(jnp.concatenate(pieces, axis=1), wb_ref[...])
    ga = _sigmoid(gate_ref[:, :D_MODEL])
    gb = _sigmoid(gate_ref[:, D_MODEL:])
    mixed = _dg((ga * ya + gb * yb).astype(BF16), wo_ref[...])
    x1 = _layer_norm(DEEPNORM_ALPHA * x_ref[...] + mixed, lnw_ref[...], lnb_ref[...])
    x1_ref[...] = x1
    x1b_ref[...] = x1.astype(BF16)


def _const_spec(x):
    return pl.BlockSpec(x.shape, lambda *_: (0,) * x.ndim, pipeline_mode=pl.Buffered(1))


def _mix(ya, y, bon, g, gates, x, gn_w, gn_b, w_up_a, w_up_b, w_o, ln_w, ln_b):
    t = x.shape[0]
    tm = min(MIX_TM, t)
    consts = (gn_w.reshape(1, -1), gn_b.reshape(1, -1), w_up_a, w_up_b, w_o, ln_w.reshape(1, -1), ln_b.reshape(1, -1))
    row = lambda w: pl.BlockSpec((tm, w), lambda i: (i, 0))
    return pl.pallas_call(
        _mix_kernel,
        grid=(t // tm,),
        in_specs=[row(MIX_WIDTH), row(MIX_WIDTH), row(MIX_WIDTH), row(MIX_WIDTH), row(2 * D_MODEL), row(D_MODEL)]
                 + [_const_spec(c) for c in consts],
        out_specs=[pl.BlockSpec((tm, D_MODEL), lambda i: (i, 0))] * 2,
        out_shape=[jax.ShapeDtypeStruct((t, D_MODEL), F32), jax.ShapeDtypeStruct((t, D_MODEL), BF16)],
        compiler_params=_cparams("parallel"),
    )(ya, y, bon, g, gates, x, *consts)


def _topk_rows(s, ids, k):
    big = jnp.asarray(1e9, F32)
    vals, idxs = [], []
    for _ in range(k):
        m = jnp.max(s, axis=0, keepdims=True)
        ix = jnp.min(jnp.where(s == m, ids, big), axis=0, keepdims=True)
        vals.append(m)
        idxs.append(ix)
        s = jnp.where(ids == ix, -jnp.inf, s)
    return jnp.concatenate(vals, axis=0), jnp.concatenate(idxs, axis=0)


def _route_kernel(x_ref, wq_ref, keys_ref, e0_ref, e1_ref, gate_ref):
    tm = x_ref.shape[0]
    topk = PEER_TOPK
    q = _dg(x_ref[...], wq_ref[...])
    qb = q.astype(BF16)
    half = N_KEYS
    key_ids = lax.broadcasted_iota(I32, (N_KEYS, tm), 0).astype(F32)
    sub8 = lax.broadcasted_iota(I32, (8, tm), 0).astype(F32)
    sub16 = lax.broadcasted_iota(I32, (topk, tm), 0).astype(F32)
    cand_ids = jnp.concatenate([sub16] + [a * topk + sub8 for a in range(1, 8)] + [(sub8 + 8.0) * topk], axis=0)
    e0s, e1s, gates = [], [], []
    for h in range(PEER_HEADS):
        tops = []
        for c in range(2):
            qs = qb[:, (2 * h + c) * half:(2 * h + c + 1) * half]
            s = _dg(keys_ref[c], qs, NT)
            tops.append(_topk_rows(s, key_ids, topk))
        (s0, i0), (s1, i1) = tops
        cand = jnp.concatenate([s0[0:1, :] + s1] + [s0[a:a + 1, :] + s1[:8, :] for a in range(1, 8)]
                               + [s0[8:, :] + s1[0:1, :]], axis=0)
        best, pos = _topk_rows(cand, cand_ids, topk)
        a_sel = jnp.floor(pos * (1.0 / topk))
        b_sel = pos - a_sel * topk
        e0 = jnp.zeros_like(pos)
        e1 = jnp.zeros_like(pos)
        for j in range(topk):
            e0 = jnp.where(a_sel == j, i0[j:j + 1, :], e0)
            e1 = jnp.where(b_sel == j, i1[j:j + 1, :], e1)
        ex = jnp.exp(best - best[0:1, :])
        gates.append(ex / jnp.sum(ex, axis=0, keepdims=True))
        e0s.append(e0)
        e1s.append(e1)
    e0_ref[...] = jnp.concatenate(e0s, axis=0).T.astype(I32)
    e1_ref[...] = jnp.concatenate(e1s, axis=0).T.astype(I32)
    gate_ref[...] = jnp.concatenate(gates, axis=0).T


def _route(x1b, w_q, sub_keys):
    t = x1b.shape[0]
    tm = min(ROUTE_TM, t)
    out = lambda: pl.BlockSpec((tm, N_SEL), lambda i: (i, 0))
    return pl.pallas_call(
        _route_kernel,
        grid=(t // tm,),
        in_specs=[pl.BlockSpec((tm, D_MODEL), lambda i: (i, 0)), _const_spec(w_q), _const_spec(sub_keys)],
        out_specs=[out(), out(), out()],
        out_shape=[jax.ShapeDtypeStruct((t, N_SEL), I32), jax.ShapeDtypeStruct((t, N_SEL), I32),
                   jax.ShapeDtypeStruct((t, N_SEL), F32)],
        compiler_params=_cparams("parallel"),
    )(x1b, w_q, sub_keys)


def _gelu_exact(x):
    return 0.5 * x * (1.0 + lax.erf(x * (2.0 ** -0.5)))


def _peer_score_kernel(x_ref, u_ref, e0_ref, e1_ref, gate_ref, o_ref, acc_ref):
    j = pl.program_id(1)

    @pl.when(j == 0)
    def _():
        acc_ref[...] = jnp.zeros_like(acc_ref)

    x = x_ref[...]
    e0, e1 = e0_ref[...], e1_ref[...]
    blocks = u_ref.shape[0] // N_KEYS
    per_dot = PH_SUB // N_KEYS
    acc = acc_ref[...]
    for s in range(u_ref.shape[0] // PH_SUB):
        h = _dg(x, u_ref[s * PH_SUB:(s + 1) * PH_SUB, :], NT)
        for b in range(per_dot):
            picked = jnp.take_along_axis(h[:, b * N_KEYS:(b + 1) * N_KEYS], e1, axis=1)
            acc = acc + jnp.where(e0 == j * blocks + s * per_dot + b, picked, 0.0)
    acc_ref[...] = acc

    @pl.when(j == pl.num_programs(1) - 1)
    def _():
        o_ref[...] = _gelu_exact(acc_ref[...]) * gate_ref[...]


def _peer_score(x1b, u_bf16, e0, e1, gate):
    t = x1b.shape[0]
    tm = min(PH_TM, t)
    tok = lambda: pl.BlockSpec((tm, N_SEL), lambda i, j: (i, 0))
    return pl.pallas_call(
        _peer_score_kernel,
        grid=(t // tm, N_EXPERTS // PH_TN),
        in_specs=[pl.BlockSpec((tm, D_MODEL), lambda i, j: (i, 0)),
                  pl.BlockSpec((PH_TN, D_MODEL), lambda i, j: (j, 0)), tok(), tok(), tok()],
        out_specs=tok(),
        out_shape=jax.ShapeDtypeStruct((t, N_SEL), F32),
        scratch_shapes=[pltpu.VMEM((tm, N_SEL), F32)],
        compiler_params=_cparams("parallel", "arbitrary"),
    )(x1b, u_bf16, e0, e1, gate)


def _peer_gate_kernel(e0_ref, e1_ref, act_ref, o_ref):
    sub = lax.broadcasted_iota(I32, (N_KEYS, N_SEL), 0)

    def body(grp, carry):
        mats = []
        for u in range(PG_GRP):
            row = pl.ds(grp * PG_GRP + u, 1)
            left = jnp.where(e0_ref[row, :] == sub, act_ref[row, :], 0.0).astype(BF16)
            right = jnp.where(e1_ref[row, :] == sub, 1.0, 0.0).astype(BF16)
            mats.append(_dg(left, right, NT))
        o_ref[grp] = jnp.swapaxes(jnp.stack(mats, axis=0), 0, 1).astype(o_ref.dtype)
        return carry

    lax.fori_loop(0, o_ref.shape[0], body, 0)


def _peer_gate(e0, e1, act):
    t = e0.shape[0]
    tt = min(PG_TT, t)
    tok = lambda: pl.BlockSpec((tt, N_SEL), lambda i: (i, 0))
    return pl.pallas_call(
        _peer_gate_kernel,
        grid=(t // tt,),
        in_specs=[tok(), tok(), tok()],
        out_specs=pl.BlockSpec((tt // PG_GRP, N_KEYS, PG_GRP, N_KEYS), lambda i: (i, 0, 0, 0)),
        out_shape=jax.ShapeDtypeStruct((t // PG_GRP, N_KEYS, PG_GRP, N_KEYS), BF16),
        compiler_params=_cparams("parallel"),
    )(e0, e1, act)


def _peer_value_kernel(g_ref, v_ref, o_ref):
    @pl.when(pl.program_id(1) == 0)
    def _():
        o_ref[...] = jnp.zeros_like(o_ref)

    tm = o_ref.shape[0]
    acc = None
    for e in range(0, g_ref.shape[1], 2):
        lhs = jnp.concatenate([g_ref[:, e, :, :].reshape(tm, N_KEYS), g_ref[:, e + 1, :, :].reshape(tm, N_KEYS)], axis=1)
        part = _dg(lhs, v_ref[e * N_KEYS:(e + 2) * N_KEYS, :])
        acc = part if acc is None else acc + part
    o_ref[...] += acc


def _peer_value(g4, v_bf16):
    t = g4.shape[0] * PG_GRP
    tm = min(PV_TM, t)
    return pl.pallas_call(
        _peer_value_kernel,
        grid=(t // tm, N_KEYS // PV_NE0),
        in_specs=[pl.BlockSpec((tm // PG_GRP, PV_NE0, PG_GRP, N_KEYS), lambda i, k: (i, k, 0, 0)),
                  pl.BlockSpec((PV_NE0 * N_KEYS, D_MODEL), lambda i, k: (k, 0))],
        out_specs=pl.BlockSpec((tm, D_MODEL), lambda i, k: (i, 0)),
        out_shape=jax.ShapeDtypeStruct((t, D_MODEL), F32),
        compiler_params=_cparams("parallel", "arbitrary"),
    )(g4, v_bf16)


def _final_kernel(x1_ref, x1b_ref, ffn_ref, p_ref, wg_ref, wp_ref, lnw_ref, lnb_ref, o_ref):
    ple = _sigmoid(_dg(x1b_ref[...], wg_ref[...])) * _dg(p_ref[...].astype(BF16), wp_ref[...])
    o_ref[...] = _layer_norm(DEEPNORM_ALPHA * x1_ref[...] + ffn_ref[...] + ple, lnw_ref[...], lnb_ref[...])


def _final(x1, x1b, ffn, p, w_gate, w_proj, ln_w, ln_b):
    t = x1.shape[0]
    tm = min(FIN_TM, t)
    consts = (w_gate, w_proj, ln_w.reshape(1, -1), ln_b.reshape(1, -1))
    row = lambda w: pl.BlockSpec((tm, w), lambda i: (i, 0))
    return pl.pallas_call(
        _final_kernel,
        grid=(t // tm,),
        in_specs=[row(D_MODEL), row(D_MODEL), row(D_MODEL), row(p.shape[1])] + [_const_spec(c) for c in consts],
        out_specs=row(D_MODEL),
        out_shape=jax.ShapeDtypeStruct((t, D_MODEL), F32),
        compiler_params=_cparams("parallel"),
    )(x1, x1b, ffn, p, *consts)


def _layer(x, p, w_in, rwkv_mu, rwkv_w0, rwkv_w_lora1, rwkv_w_lora2, rwkv_a0, rwkv_a_lora1,
           rwkv_a_lora2, rwkv_g_lora1, rwkv_g_lora2, rwkv_k_k, rwkv_k_a, rwkv_r_k, rwkv_lnx_w,
           rwkv_lnx_b, w_up_a, w_up_b, w_o, ln1_w, ln1_b, peer_w_q, peer_sub_keys, peer_u, peer_v,
           ple_w_gate, ple_w_proj, ln2_w, ln2_b):
    bf = lambda w: w.astype(BF16)
    n_sb = 3 * MIX_WIDTH
    n_rw = 4 * MIX_WIDTH
    w_in = bf(w_in)
    qkv = _proj(x, w_in[:, :n_sb], BF16)
    rw = _proj(x, w_in[:, n_sb:n_sb + n_rw], F32)
    gates = _proj(x, w_in[:, n_sb + n_rw:], F32)
    y_a = _sb_attention(qkv)
    r, k, v, lw, a, g = _rwkv_pre(rw, rwkv_mu, rwkv_w0, rwkv_w_lora1, rwkv_w_lora2, rwkv_a0,
                                  rwkv_a_lora1, rwkv_a_lora2, rwkv_g_lora1, rwkv_g_lora2)
    rp, y0, bon, pm, qm = _rwkv_chunk(r, k, v, lw, a, rwkv_k_k, rwkv_k_a, rwkv_r_k.reshape(-1))
    y_b = _rwkv_state(rp, y0, pm, qm)
    x1, x1b = _mix(y_a, y_b, bon, g, gates, x, rwkv_lnx_w, rwkv_lnx_b, bf(w_up_a), bf(w_up_b), bf(w_o),
                   ln1_w, ln1_b)
    e0, e1, gate = _route(x1b, bf(peer_w_q), bf(peer_sub_keys))
    act = _peer_score(x1b, bf(peer_u), e0, e1, gate)
    ffn = _peer_value(_peer_gate(e0, e1, act), bf(peer_v))
    return _final(x1, x1b, ffn, p, bf(ple_w_gate), bf(ple_w_proj), ln2_w, ln2_b)


def kernel(x, p, w_in, rwkv_mu, rwkv_w0, rwkv_w_lora1, rwkv_w_lora2, rwkv_a0, rwkv_a_lora1, rwkv_a_lora2, rwkv_g_lora1, rwkv_g_lora2, rwkv_k_k, rwkv_k_a, rwkv_r_k, rwkv_lnx_w, rwkv_lnx_b, w_up_a, w_up_b, w_o, ln1_w, ln1_b, peer_w_q, peer_sub_keys, peer_u, peer_v, ple_w_gate, ple_w_proj, ln2_w, ln2_b):
    bsz, t, d = x.shape
    depth = w_in.shape[0]
    xs = x.reshape(bsz * t, d)
    assert bsz == 1, "token shift / attention / scan treat the flattened rows as one sequence"
    for i in range(depth):
        xs = _layer(xs, p[i].reshape(bsz * t, -1), w_in[i], rwkv_mu[i], rwkv_w0[i], rwkv_w_lora1[i],
                    rwkv_w_lora2[i], rwkv_a0[i], rwkv_a_lora1[i], rwkv_a_lora2[i], rwkv_g_lora1[i],
                    rwkv_g_lora2[i], rwkv_k_k[i], rwkv_k_a[i], rwkv_r_k[i], rwkv_lnx_w[i], rwkv_lnx_b[i],
                    w_up_a[i], w_up_b[i], w_o[i], ln1_w[i], ln1_b[i], peer_w_q[i], peer_sub_keys[i],
                    peer_u[i], peer_v[i], ple_w_gate[i], ple_w_proj[i], ln2_w[i], ln2_b[i])
    return xs.reshape(bsz, t, d)
```

```python
import functools

import jax
import jax.numpy as jnp
from jax import lax
from jax.experimental import pallas as pl
from jax.experimental.pallas import tpu as pltpu

F32 = jnp.float32
BF16 = jnp.bfloat16
I32 = jnp.int32

D_MODEL = 2048
HEAD_DIM = 64
N_HEADS = 16
MIX_WIDTH = N_HEADS * HEAD_DIM
PEER_HEADS = 8
PEER_TOPK = 16
N_KEYS = 128
N_EXPERTS = N_KEYS * N_KEYS
N_SEL = PEER_HEADS * PEER_TOPK
GN_EPS = 64e-5
LN_EPS = 1e-5
DEEPNORM_ALPHA = 2.0 ** 0.25

LANES = 128
VMEM_LIMIT = 56 * 1024 * 1024

MM_TM, MM_TN = 1024, 1024
SB_BLK = 128
SB_SUBS = 4
SB_WIN = 2 * LANES
SB_DONE = -88.0
RW_TM = 256
RW_C = 64
RW_NCH = 8
RW_NCS = 8
MIX_TM = 256
ROUTE_TM = 256
PH_TM, PH_TN = 512, 2048
PH_SUB = 256
PG_TT = 128
PG_GRP = 16
PV_TM, PV_NE0 = 512, 16
FIN_TM = 256

assert 2 * RW_C == LANES and RW_C == HEAD_DIM

NN = (((1,), (0,)), ((), ()))
NT = (((1,), (1,)), ((), ()))
TN = (((0,), (0,)), ((), ()))


def _cparams(*sem):
    return pltpu.CompilerParams(dimension_semantics=tuple(sem), vmem_limit_bytes=VMEM_LIMIT)


def _dg(a, b, dims=NN):
    return lax.dot_general(a, b, dims, preferred_element_type=F32)


def _split2(x):
    hi = x.astype(BF16)
    lo = (x - hi.astype(F32)).astype(BF16)
    return hi, lo


def _mm1(a, b, dims=NN):
    return _dg(a.astype(BF16), b.astype(BF16), dims)


def _mm_sum_rhs(x, m_bf16):
    hi, lo = _split2(x)
    return _dg(hi, m_bf16) + _dg(lo, m_bf16)


def _mm_sum_lhs(m_bf16, x):
    hi, lo = _split2(x)
    return _dg(m_bf16, hi) + _dg(m_bf16, lo)


def _softplus(z):
    return jnp.maximum(z, 0.0) + jnp.log(1.0 + jnp.exp(-jnp.abs(z)))


def _sigmoid(z):
    return 1.0 / (1.0 + jnp.exp(-z))


def _layer_norm(x, g, b):
    mu = jnp.mean(x, axis=-1, keepdims=True)
    d = x - mu
    var = jnp.mean(d * d, axis=-1, keepdims=True)
    return d * lax.rsqrt(var + LN_EPS) * g + b


def _proj_kernel(a_ref, b_ref, o_ref):
    o_ref[...] = _dg(a_ref[...].astype(BF16), b_ref[...]).astype(o_ref.dtype)


def _proj(a, b, out_dtype):
    m, k = a.shape
    n = b.shape[1]
    tm, tn = min(MM_TM, m), min(MM_TN, n)
    return pl.pallas_call(
        _proj_kernel,
        grid=(m // tm, n // tn),
        in_specs=[pl.BlockSpec((tm, k), lambda i, j: (i, 0)),
                  pl.BlockSpec((k, tn), lambda i, j: (0, j))],
        out_specs=pl.BlockSpec((tm, tn), lambda i, j: (i, j)),
        out_shape=jax.ShapeDtypeStruct((m, n), out_dtype),
        compiler_params=_cparams("parallel", "parallel"),
    )(a, b)


def _sb_kernel(q_ref, k_ref, v_ref, o_ref, acc_ref, carry_ref):
    i = pl.program_id(1)
    blk = SB_BLK
    n_sub = q_ref.shape[0] // blk
    win = SB_WIN
    lane = lax.broadcasted_iota(I32, (1, LANES), 1)
    row = lax.broadcasted_iota(I32, (blk, win), 0)
    col = lax.broadcasted_iota(I32, (blk, win), 1)
    r_io = lax.broadcasted_iota(I32, (win, win), 0)
    c_io = lax.broadcasted_iota(I32, (win, win), 1)
    later_mat = jnp.where(r_io > c_io, 1.0, 0.0).astype(BF16)
    scale = jnp.asarray(HEAD_DIM ** -0.5, BF16)
    chains = [(s, h) for s in range(n_sub) for h in range(2)]
    qh = []
    for s, h in chains:
        q = q_ref[s * blk:(s + 1) * blk, :]
        qh.append(jnp.where((lane < HEAD_DIM) == (h == 0), q, jnp.zeros_like(q)) * scale)
    first_end = [(i * n_sub + s + 1) * blk for s in range(n_sub)]
    acc_ref[...] = jnp.zeros_like(acc_ref)
    carry_ref[...] = jnp.zeros_like(carry_ref)

    def step(it):
        kwin, vwin, valid = [], [], []
        for s in range(n_sub):
            end = first_end[s] - it * win
            start = pl.multiple_of(jnp.maximum(end - win, 0), blk)
            kwin.append(k_ref[pl.ds(start, win), :])
            vwin.append(v_ref[pl.ds(start, win), :])
            valid.append((start + col) < jnp.minimum(first_end[s] - blk + row, end))
        z = [_dg(qh[c], kwin[s], NT) for c, (s, h) in enumerate(chains)]
        sp = [_softplus(x) for x in z]
        log_keep = [jnp.where(valid[s], -sp[c], 0.0) for c, (s, h) in enumerate(chains)]
        parts = [_split2(x) for x in log_keep]
        carry = [carry_ref[c] for c in range(len(chains))]
        log_later = [_dg(hi, later_mat) + _dg(lo, later_mat) + jnp.concatenate([carry[c]] * (win // LANES), axis=1)
                     for c, (hi, lo) in enumerate(parts)]
        w = [jnp.where(valid[s], jnp.exp(z[c] - sp[c] + log_later[c]), 0.0).astype(BF16)
             for c, (s, h) in enumerate(chains)]
        pv = [_dg(w[c], vwin[s]) for c, (s, h) in enumerate(chains)]
        for c in range(len(chains)):
            acc_ref[c] += pv[c]
            carry_ref[c] = jnp.broadcast_to(log_later[c][:, 0:1] + log_keep[c][:, 0:1], (blk, LANES))
        return it + 1

    def cond(it):
        alive = None
        for s in range(n_sub):
            more = jnp.logical_and(first_end[s] - it * win > 0, jnp.max(carry_ref[2 * s:2 * s + 2]) > SB_DONE)
            alive = more if alive is None else jnp.logical_or(alive, more)
        return alive

    lax.while_loop(cond, step, step(0))
    for s in range(n_sub):
        o_ref[s * blk:(s + 1) * blk, :] = jnp.where(lane < HEAD_DIM, acc_ref[2 * s], acc_ref[2 * s + 1]).astype(o_ref.dtype)


def _sb_attention(qkv):
    t = qkv.shape[0]
    blk = SB_BLK * SB_SUBS
    n_pairs = MIX_WIDTH // LANES
    return pl.pallas_call(
        _sb_kernel,
        grid=(n_pairs, t // blk),
        in_specs=[pl.BlockSpec((blk, LANES), lambda hp, i: (i, hp)),
                  pl.BlockSpec((t, LANES), lambda hp, i: (0, n_pairs + hp)),
                  pl.BlockSpec((t, LANES), lambda hp, i: (0, 2 * n_pairs + hp))],
        out_specs=pl.BlockSpec((blk, LANES), lambda hp, i: (i, hp)),
        out_shape=jax.ShapeDtypeStruct((t, MIX_WIDTH), BF16),
        scratch_shapes=[pltpu.VMEM((2 * SB_SUBS, SB_BLK, LANES), F32), pltpu.VMEM((2 * SB_SUBS, SB_BLK, LANES), F32)],
        compiler_params=_cparams("parallel", "parallel"),
    )(qkv, qkv, qkv)


def _rwkv_pre_kernel(cur_ref, prev_ref, mu_ref, w0_ref, wl1_ref, wl2_ref, a0_ref, al1_ref, al2_ref,
                     gl1_ref, gl2_ref, r_ref, k_ref, v_ref, lw_ref, a_ref, g_ref):
    i = pl.program_id(0)
    tm = cur_ref.shape[0]
    wd = MIX_WIDTH
    first_row = lax.broadcasted_iota(I32, (tm, wd), 0) == 0
    keep_prev = jnp.where(i == 0, 0.0, 1.0)

    def shifted(col):
        z = cur_ref[:, col * wd:(col + 1) * wd]
        last = prev_ref[7:8, col * wd:(col + 1) * wd] * keep_prev
        prev = jnp.where(first_row, last, pltpu.roll(z, 1, 0))
        return z, prev - z

    z, d = shifted(0)
    r_ref[...] = z + d * mu_ref[0:1, :]
    z, d = shifted(1)
    k_ref[...] = z + d * mu_ref[1:2, :]
    z, d = shifted(2)
    v_ref[...] = z + d * mu_ref[2:3, :]
    z, d = shifted(3)
    xw = z + d * mu_ref[3:4, :]
    xa = z + d * mu_ref[4:5, :]
    xg = z + d * mu_ref[5:6, :]
    w = w0_ref[...] + _mm1(jnp.tanh(_mm1(xw, wl1_ref[...])), wl2_ref[...])
    w = -_softplus(-w) - 0.5
    lw_ref[...] = -jnp.exp(w)
    a_ref[...] = _sigmoid(a0_ref[...] + _mm1(_mm1(xa, al1_ref[...]), al2_ref[...]))
    g_ref[...] = _mm1(_sigmoid(_mm1(xg, gl1_ref[...])), gl2_ref[...])


def _pad_to(x, axis, size):
    pad = [(0, 0)] * x.ndim
    pad[axis] = (0, size - x.shape[axis])
    return jnp.pad(x, pad)


def _rwkv_pre(rw, mu, w0, wl1, wl2, a0, al1, al2, gl1, gl2):
    t = rw.shape[0]
    tm = min(RW_TM, t)
    wd = MIX_WIDTH
    lo = LANES

    def lora_pair(l1, l2):
        n = -(-l1.shape[1] // lo) * lo
        return _pad_to(l1, 1, n).astype(BF16), _pad_to(l2, 0, n).astype(BF16)

    wl1, wl2 = lora_pair(wl1, wl2)
    al1, al2 = lora_pair(al1, al2)
    gl1, gl2 = lora_pair(gl1, gl2)
    full = lambda x: pl.BlockSpec(x.shape, lambda i: (0,) * x.ndim)
    row = lambda: pl.BlockSpec((tm, wd), lambda i: (i, 0))
    consts = (mu, w0.reshape(1, wd), wl1, wl2, a0.reshape(1, wd), al1, al2, gl1, gl2)
    return pl.pallas_call(
        _rwkv_pre_kernel,
        grid=(t // tm,),
        in_specs=[pl.BlockSpec((tm, 4 * wd), lambda i: (i, 0)),
                  pl.BlockSpec((8, 4 * wd), lambda i: (jnp.maximum(i * (tm // 8) - 1, 0), 0))]
                 + [full(c) for c in consts],
        out_specs=[row() for _ in range(6)],
        out_shape=[jax.ShapeDtypeStruct((t, wd), F32) for _ in range(6)],
        compiler_params=_cparams("parallel"),
    )(rw, rw, *consts)


def _rwkv_chunk_kernel(r_ref, k_ref, v_ref, lw_ref, a_ref, kkp_ref, kap_ref, rkp_ref,
                       rp_ref, y0_ref, bon_ref, p_ref, q_ref):
    c_len = RW_C
    n2 = 2 * c_len
    n_chunks = r_ref.shape[0] // c_len
    r2 = lax.broadcasted_iota(I32, (n2, LANES), 0)
    c2 = lax.broadcasted_iota(I32, (n2, LANES), 1)
    same = (r2 < c_len) == (c2 < HEAD_DIM)
    t_r = jnp.where(r2 < c_len, r2, r2 - c_len)
    t_c = jnp.where(c2 < HEAD_DIM, c2, c2 - HEAD_DIM)
    strict = jnp.logical_and(same, t_r > t_c)
    incl = jnp.logical_and(same, t_r >= t_c)
    eye = jnp.where(r2 == c2, 1.0, 0.0).astype(F32)
    group_ones = jnp.where(same, 1.0, 0.0).astype(BF16)
    lr = lax.broadcasted_iota(I32, (c_len, c_len), 0)
    lc = lax.broadcasted_iota(I32, (c_len, c_len), 1)
    cum_mat = jnp.where(lr >= lc, 1.0, 0.0).astype(BF16)
    kkp, kap, rkp = kkp_ref[...], kap_ref[...], rkp_ref[...]

    def stack(z):
        return jnp.where(same, jnp.concatenate([z, z], axis=0), 0.0)

    def unstack(zs):
        return zs[:c_len] + zs[c_len:]

    cs = range(n_chunks)
    rows = [pl.ds(c * c_len, c_len) for c in cs]
    r = [r_ref[rw, :] for rw in rows]
    kr = [k_ref[rw, :] for rw in rows]
    v = [v_ref[rw, :] for rw in rows]
    lw = [lw_ref[rw, :] for rw in rows]
    a = [a_ref[rw, :] for rw in rows]
    kk = [x * kkp for x in kr]
    cum = [_mm_sum_lhs(cum_mat, x) for x in lw]
    ssq = [_mm_sum_rhs(x * x, group_ones) for x in kk]
    km = [kr[c] * (1.0 + (a[c] - 1.0) * kap) for c in cs]
    bsum = [_mm_sum_rhs(r[c] * km[c] * rkp, group_ones) for c in cs]
    kk = [kk[c] / jnp.maximum(jnp.sqrt(ssq[c]), 1e-12) for c in cs]
    bv = [kk[c] * a[c] for c in cs]
    last = [x[c_len - 1:c_len, :] for x in cum]
    rt = [stack(r[c] * jnp.exp(cum[c])) for c in cs]
    at_b = [stack(-kk[c] * jnp.exp(cum[c] - lw[c])).astype(BF16) for c in cs]
    rt_b = [x.astype(BF16) for x in rt]
    g_inv = [jnp.exp(-x) for x in cum]
    btkt = [jnp.concatenate([stack(bv[c] * g_inv[c]), stack(km[c] * g_inv[c])], axis=0).astype(BF16) for c in cs]
    m_a = [_dg(at_b[c], btkt[c], NT) for c in cs]
    m_r = [_dg(rt_b[c], btkt[c], NT) for c in cs]
    vs_b = [stack(x).astype(BF16) for x in v]
    g_end = [jnp.exp(last[c] - cum[c]) for c in cs]
    bd_b = [stack(bv[c] * g_end[c]).astype(BF16) for c in cs]
    kd_b = [stack(km[c] * g_end[c]).astype(BF16) for c in cs]
    m_ab = [jnp.where(strict, x[:, :n2], 0.0) for x in m_a]
    m_ak = [jnp.where(strict, x[:, n2:], 0.0).astype(BF16) for x in m_a]
    m_rb = [jnp.where(incl, x[:, :n2], 0.0).astype(BF16) for x in m_r]
    m_rk = [jnp.where(incl, x[:, n2:], 0.0).astype(BF16) for x in m_r]
    akv = [_dg(m_ak[c], vs_b[c]) for c in cs]
    rkv = [_dg(m_rk[c], vs_b[c]) for c in cs]
    kdv = [_dg(kd_b[c], vs_b[c], TN) for c in cs]
    tinv = [eye + x for x in m_ab]
    npow = [x.astype(BF16) for x in m_ab]
    for _ in range(5):
        npow = [_dg(x, x).astype(BF16) for x in npow]
        tinv = [tinv[c] + _dg(tinv[c].astype(BF16), npow[c]) for c in cs]
    au_b = [_dg(tinv[c].astype(BF16), jnp.concatenate([at_b[c], akv[c].astype(BF16)], axis=1)).astype(BF16)
            for c in cs]
    ry = [_dg(m_rb[c], au_b[c]) for c in cs]
    pq = [_dg(bd_b[c], au_b[c], TN) for c in cs]
    for c in cs:
        prow = pl.ds(c * n2, n2)
        rp_ref[rows[c], :] = unstack(rt[c] + ry[c][:, :LANES]).astype(rp_ref.dtype)
        y0_ref[rows[c], :] = unstack(ry[c][:, LANES:] + rkv[c])
        p_ref[prow, :] = (eye * jnp.exp(last[c]) + pq[c][:, :LANES]).astype(p_ref.dtype)
        q_ref[prow, :] = pq[c][:, LANES:] + kdv[c]
        bon_ref[rows[c], :] = bsum[c] * v[c]


def _rwkv_chunk(r, k, v, lw, a, k_k, k_a, r_k):
    t = r.shape[0]
    rb = min(RW_C * RW_NCH, t)
    n_pairs = MIX_WIDTH // LANES
    row = lambda: pl.BlockSpec((rb, LANES), lambda i, hp: (i, hp))
    par = lambda: pl.BlockSpec((1, LANES), lambda i, hp: (0, hp))
    mat = lambda: pl.BlockSpec((2 * rb, LANES), lambda i, hp: (i, hp))
    return pl.pallas_call(
        _rwkv_chunk_kernel,
        grid=(t // rb, n_pairs),
        in_specs=[row() for _ in range(5)] + [par() for _ in range(3)],
        out_specs=[row(), row(), row(), mat(), mat()],
        out_shape=[jax.ShapeDtypeStruct((t, MIX_WIDTH), BF16), jax.ShapeDtypeStruct((t, MIX_WIDTH), F32),
                   jax.ShapeDtypeStruct((t, MIX_WIDTH), F32), jax.ShapeDtypeStruct((2 * t, MIX_WIDTH), BF16),
                   jax.ShapeDtypeStruct((2 * t, MIX_WIDTH), F32)],
        compiler_params=_cparams("parallel", "parallel"),
    )(r, k, v, lw, a, k_k.reshape(1, -1), k_a.reshape(1, -1), r_k.reshape(1, -1))


def _rwkv_state_kernel(rp_ref, y0_ref, p_ref, q_ref, o_ref, st_ref):
    @pl.when(pl.program_id(0) == 0)
    def _():
        st_ref[...] = jnp.zeros_like(st_ref)

    c_len = RW_C
    pairs = range(MIX_WIDTH // LANES)
    lanes = [slice(hp * LANES, (hp + 1) * LANES) for hp in pairs]
    st = [st_ref[hp] for hp in pairs]
    for c in range(rp_ref.shape[0] // c_len):
        rows = slice(c * c_len, (c + 1) * c_len)
        mrows = slice(2 * c * c_len, 2 * (c + 1) * c_len)
        st_b = [x.astype(BF16) for x in st]
        st = [_dg(p_ref[mrows, lanes[hp]], st_b[hp]) + q_ref[mrows, lanes[hp]] for hp in pairs]
        for hp in pairs:
            o_ref[rows, lanes[hp]] = _dg(rp_ref[rows, lanes[hp]], st_b[hp]) + y0_ref[rows, lanes[hp]]
    for hp in pairs:
        st_ref[hp] = st[hp]


def _rwkv_state(rp, y0, p, q):
    t = rp.shape[0]
    rb = min(RW_C * RW_NCS, t)
    wd = MIX_WIDTH
    row = lambda: pl.BlockSpec((rb, wd), lambda i: (i, 0))
    mat = lambda: pl.BlockSpec((2 * rb, wd), lambda i: (i, 0))
    return pl.pallas_call(
        _rwkv_state_kernel,
        grid=(t // rb,),
        in_specs=[row(), row(), mat(), mat()],
        out_specs=row(),
        out_shape=jax.ShapeDtypeStruct((t, wd), F32),
        scratch_shapes=[pltpu.VMEM((wd // LANES, LANES, LANES), F32)],
        compiler_params=_cparams("arbitrary"),
    )(rp, y0, p, q)


def _mix_kernel(ya_ref, y_ref, bon_ref, g_ref, gate_ref, x_ref, gnw_ref, gnb_ref, wa_ref, wb_ref, wo_ref,
                lnw_ref, lnb_ref, x1_ref, x1b_ref):
    r2 = lax.broadcasted_iota(I32, (LANES, LANES), 0)
    c2 = lax.broadcasted_iota(I32, (LANES, LANES), 1)
    group_ones = jnp.where((r2 < HEAD_DIM) == (c2 < HEAD_DIM), 1.0, 0.0).astype(BF16)
    inv_n = 1.0 / HEAD_DIM
    pieces = []
    for hp in range(MIX_WIDTH // LANES):
        ls = slice(hp * LANES, (hp + 1) * LANES)
        y = y_ref[:, ls]
        d = y - _mm_sum_rhs(y, group_ones) * inv_n
        var = _mm_sum_rhs(d * d, group_ones) * inv_n
        yn = d * lax.rsqrt(var + GN_EPS) * gnw_ref[:, ls] + gnb_ref[:, ls]
        pieces.append(((yn + bon_ref[:, ls]) * g_ref[:, ls]).astype(BF16))
    ya = _dg(ya_ref[...], wa_ref[...])
    yb = _dg(jnp.concatenate(pieces, axis=1), wb_ref[...])
    ga = _sigmoid(gate_ref[:, :D_MODEL])
    gb = _sigmoid(gate_ref[:, D_MODEL:])
    mixed = _dg((ga * ya + gb * yb).astype(BF16), wo_ref[...])
    x1 = _layer_norm(DEEPNORM_ALPHA * x_ref[...] + mixed, lnw_ref[...], lnb_ref[...])
    x1_ref[...] = x1
    x1b_ref[...] = x1.astype(BF16)


def _const_spec(x):
    return pl.BlockSpec(x.shape, lambda *_: (0,) * x.ndim, pipeline_mode=pl.Buffered(1))


def _mix(ya, y, bon, g, gates, x, gn_w, gn_b, w_up_a, w_up_b, w_o, ln_w, ln_b):
    t = x.shape[0]
    tm = min(MIX_TM, t)
    consts = (gn_w.reshape(1, -1), gn_b.reshape(1, -1), w_up_a, w_up_b, w_o, ln_w.reshape(1, -1), ln_b.reshape(1, -1))
    row = lambda w: pl.BlockSpec((tm, w), lambda i: (i, 0))
    return pl.pallas_call(
        _mix_kernel,
        grid=(t // tm,),
        in_specs=[row(MIX_WIDTH), row(MIX_WIDTH), row(MIX_WIDTH), row(MIX_WIDTH), row(2 * D_MODEL), row(D_MODEL)]
                 + [_const_spec(c) for c in consts],
        out_specs=[pl.BlockSpec((tm, D_MODEL), lambda i: (i, 0))] * 2,
        out_shape=[jax.ShapeDtypeStruct((t, D_MODEL), F32), jax.ShapeDtypeStruct((t, D_MODEL), BF16)],
        compiler_params=_cparams("parallel"),
    )(ya, y, bon, g, gates, x, *consts)


def _topk_rows(s, ids, k):
    big = jnp.asarray(1e9, F32)
    vals, idxs = [], []
    for _ in range(k):
        m = jnp.max(s, axis=0, keepdims=True)
        ix = jnp.min(jnp.where(s == m, ids, big), axis=0, keepdims=True)
        vals.append(m)
        idxs.append(ix)
        s = jnp.where(ids == ix, -jnp.inf, s)
    return jnp.concatenate(vals, axis=0), jnp.concatenate(idxs, axis=0)


def _route_kernel(x_ref, wq_ref, keys_ref, e0_ref, e1_ref, gate_ref):
    tm = x_ref.shape[0]
    topk = PEER_TOPK
    q = _dg(x_ref[...], wq_ref[...])
    qb = q.astype(BF16)
    half = N_KEYS
    key_ids = lax.broadcasted_iota(I32, (N_KEYS, tm), 0).astype(F32)
    sub8 = lax.broadcasted_iota(I32, (8, tm), 0).astype(F32)
    sub16 = lax.broadcasted_iota(I32, (topk, tm), 0).astype(F32)
    cand_ids = jnp.concatenate([sub16] + [a * topk + sub8 for a in range(1, 8)] + [(sub8 + 8.0) * topk], axis=0)
    e0s, e1s, gates = [], [], []
    for h in range(PEER_HEADS):
        tops = []
        for c in range(2):
            qs = qb[:, (2 * h + c) * half:(2 * h + c + 1) * half]
            s = _dg(keys_ref[c], qs, NT)
            tops.append(_topk_rows(s, key_ids, topk))
        (s0, i0), (s1, i1) = tops
        cand = jnp.concatenate([s0[0:1, :] + s1] + [s0[a:a + 1, :] + s1[:8, :] for a in range(1, 8)]
                               + [s0[8:, :] + s1[0:1, :]], axis=0)
        best, pos = _topk_rows(cand, cand_ids, topk)
        a_sel = jnp.floor(pos * (1.0 / topk))
        b_sel = pos - a_sel * topk
        e0 = jnp.zeros_like(pos)
        e1 = jnp.zeros_like(pos)
        for j in range(topk):
            e0 = jnp.where(a_sel == j, i0[j:j + 1, :], e0)
            e1 = jnp.where(b_sel == j, i1[j:j + 1, :], e1)
        ex = jnp.exp(best - best[0:1, :])
        gates.append(ex / jnp.sum(ex, axis=0, keepdims=True))
        e0s.append(e0)
        e1s.append(e1)
    e0_ref[...] = jnp.concatenate(e0s, axis=0).T.astype(I32)
    e1_ref[...] = jnp.concatenate(e1s, axis=0).T.astype(I32)
    gate_ref[...] = jnp.concatenate(gates, axis=0).T


def _route(x1b, w_q, sub_keys):
    t = x1b.shape[0]
    tm = min(ROUTE_TM, t)
    out = lambda: pl.BlockSpec((tm, N_SEL), lambda i: (i, 0))
    return pl.pallas_call(
        _route_kernel,
        grid=(t // tm,),
        in_specs=[pl.BlockSpec((tm, D_MODEL), lambda i: (i, 0)), _const_spec(w_q), _const_spec(sub_keys)],
        out_specs=[out(), out(), out()],
        out_shape=[jax.ShapeDtypeStruct((t, N_SEL), I32), jax.ShapeDtypeStruct((t, N_SEL), I32),
                   jax.ShapeDtypeStruct((t, N_SEL), F32)],
        compiler_params=_cparams("parallel"),
    )(x1b, w_q, sub_keys)


def _gelu_exact(x):
    return 0.5 * x * (1.0 + lax.erf(x * (2.0 ** -0.5)))


def _peer_score_kernel(x_ref, u_ref, e0_ref, e1_ref, gate_ref, o_ref, acc_ref):
    j = pl.program_id(1)

    @pl.when(j == 0)
    def _():
        acc_ref[...] = jnp.zeros_like(acc_ref)

    x = x_ref[...]
    e0, e1 = e0_ref[...], e1_ref[...]
    blocks = u_ref.shape[0] // N_KEYS
    per_dot = PH_SUB // N_KEYS
    acc = acc_ref[...]
    for s in range(u_ref.shape[0] // PH_SUB):
        h = _dg(x, u_ref[s * PH_SUB:(s + 1) * PH_SUB, :], NT)
        for b in range(per_dot):
            picked = jnp.take_along_axis(h[:, b * N_KEYS:(b + 1) * N_KEYS], e1, axis=1)
            acc = acc + jnp.where(e0 == j * blocks + s * per_dot + b, picked, 0.0)
    acc_ref[...] = acc

    @pl.when(j == pl.num_programs(1) - 1)
    def _():
        o_ref[...] = _gelu_exact(acc_ref[...]) * gate_ref[...]


def _peer_score(x1b, u_bf16, e0, e1, gate):
    t = x1b.shape[0]
    tm = min(PH_TM, t)
    tok = lambda: pl.BlockSpec((tm, N_SEL), lambda i, j: (i, 0))
    return pl.pallas_call(
        _peer_score_kernel,
        grid=(t // tm, N_EXPERTS // PH_TN),
        in_specs=[pl.BlockSpec((tm, D_MODEL), lambda i, j: (i, 0)),
                  pl.BlockSpec((PH_TN, D_MODEL), lambda i, j: (j, 0)), tok(), tok(), tok()],
        out_specs=tok(),
        out_shape=jax.ShapeDtypeStruct((t, N_SEL), F32),
        scratch_shapes=[pltpu.VMEM((tm, N_SEL), F32)],
        compiler_params=_cparams("parallel", "arbitrary"),
    )(x1b, u_bf16, e0, e1, gate)


def _peer_gate_kernel(e0_ref, e1_ref, act_ref, o_ref):
    sub = lax.broadcasted_iota(I32, (N_KEYS, N_SEL), 0)

    def body(grp, carry):
        mats = []
        for u in range(PG_GRP):
            row = pl.ds(grp * PG_GRP + u, 1)
            left = jnp.where(e0_ref[row, :] == sub, act_ref[row, :], 0.0).astype(BF16)
            right = jnp.where(e1_ref[row, :] == sub, 1.0, 0.0).astype(BF16)
            mats.append(_dg(left, right, NT))
        o_ref[grp] = jnp.swapaxes(jnp.stack(mats, axis=0), 0, 1).astype(o_ref.dtype)
        return carry

    lax.fori_loop(0, o_ref.shape[0], body, 0)


def _peer_gate(e0, e1, act):
    t = e0.shape[0]
    tt = min(PG_TT, t)
    tok = lambda: pl.BlockSpec((tt, N_SEL), lambda i: (i, 0))
    return pl.pallas_call(
        _peer_gate_kernel,
        grid=(t // tt,),
        in_specs=[tok(), tok(), tok()],
        out_specs=pl.BlockSpec((tt // PG_GRP, N_KEYS, PG_GRP, N_KEYS), lambda i: (i, 0, 0, 0)),
        out_shape=jax.ShapeDtypeStruct((t // PG_GRP, N_KEYS, PG_GRP, N_KEYS), BF16),
        compiler_params=_cparams("parallel"),
    )(e0, e1, act)


def _peer_value_kernel(g_ref, v_ref, o_ref):
    @pl.when(pl.program_id(1) == 0)
    def _():
        o_ref[...] = jnp.zeros_like(o_ref)

    tm = o_ref.shape[0]
    acc = None
    for e in range(0, g_ref.shape[1], 2):
        lhs = jnp.concatenate([g_ref[:, e, :, :].reshape(tm, N_KEYS), g_ref[:, e + 1, :, :].reshape(tm, N_KEYS)], axis=1)
        part = _dg(lhs, v_ref[e * N_KEYS:(e + 2) * N_KEYS, :])
        acc = part if acc is None else acc + part
    o_ref[...] += acc


def _peer_value(g4, v_bf16):
    t = g4.shape[0] * PG_GRP
    tm = min(PV_TM, t)
    return pl.pallas_call(
        _peer_value_kernel,
        grid=(t // tm, N_KEYS // PV_NE0),
        in_specs=[pl.BlockSpec((tm // PG_GRP, PV_NE0, PG_GRP, N_KEYS), lambda i, k: (i, k, 0, 0)),
                  pl.BlockSpec((PV_NE0 * N_KEYS, D_MODEL), lambda i, k: (k, 0))],
        out_specs=pl.BlockSpec((tm, D_MODEL), lambda i, k: (i, 0)),
        out_shape=jax.ShapeDtypeStruct((t, D_MODEL), F32),
        compiler_params=_cparams("parallel", "arbitrary"),
    )(g4, v_bf16)


def _final_kernel(x1_ref, x1b_ref, ffn_ref, p_ref, wg_ref, wp_ref, lnw_ref, lnb_ref, o_ref):
    ple = _sigmoid(_dg(x1b_ref[...], wg_ref[...])) * _dg(p_ref[...].astype(BF16), wp_ref[...])
    o_ref[...] = _layer_norm(DEEPNORM_ALPHA * x1_ref[...] + ffn_ref[...] + ple, lnw_ref[...], lnb_ref[...])


def _final(x1, x1b, ffn, p, w_gate, w_proj, ln_w, ln_b):
    t = x1.shape[0]
    tm = min(FIN_TM, t)
    consts = (w_gate, w_proj, ln_w.reshape(1, -1), ln_b.reshape(1, -1))
    row = lambda w: pl.BlockSpec((tm, w), lambda i: (i, 0))
    return pl.pallas_call(
        _final_kernel,
        grid=(t // tm,),
        in_specs=[row(D_MODEL), row(D_MODEL), row(D_MODEL), row(p.shape[1])] + [_const_spec(c) for c in consts],
        out_specs=row(D_MODEL),
        out_shape=jax.ShapeDtypeStruct((t, D_MODEL), F32),
        compiler_params=_cparams("parallel"),
    )(x1, x1b, ffn, p, *consts)


def _layer(x, p, w_in, rwkv_mu, rwkv_w0, rwkv_w_lora1, rwkv_w_lora2, rwkv_a0, rwkv_a_lora1,
           rwkv_a_lora2, rwkv_g_lora1, rwkv_g_lora2, rwkv_k_k, rwkv_k_a, rwkv_r_k, rwkv_lnx_w,
           rwkv_lnx_b, w_up_a, w_up_b, w_o, ln1_w, ln1_b, peer_w_q, peer_sub_keys, peer_u, peer_v,
           ple_w_gate, ple_w_proj, ln2_w, ln2_b):
    bf = lambda w: w.astype(BF16)
    n_sb = 3 * MIX_WIDTH
    n_rw = 4 * MIX_WIDTH
    w_in = bf(w_in)
    qkv = _proj(x, w_in[:, :n_sb], BF16)
    rw = _proj(x, w_in[:, n_sb:n_sb + n_rw], F32)
    gates = _proj(x, w_in[:, n_sb + n_rw:], F32)
    y_a = _sb_attention(qkv)
    r, k, v, lw, a, g = _rwkv_pre(rw, rwkv_mu, rwkv_w0, rwkv_w_lora1, rwkv_w_lora2, rwkv_a0,
                                  rwkv_a_lora1, rwkv_a_lora2, rwkv_g_lora1, rwkv_g_lora2)
    rp, y0, bon, pm, qm = _rwkv_chunk(r, k, v, lw, a, rwkv_k_k, rwkv_k_a, rwkv_r_k.reshape(-1))
    y_b = _rwkv_state(rp, y0, pm, qm)
    x1, x1b = _mix(y_a, y_b, bon, g, gates, x, rwkv_lnx_w, rwkv_lnx_b, bf(w_up_a), bf(w_up_b), bf(w_o),
                   ln1_w, ln1_b)
    e0, e1, gate = _route(x1b, bf(peer_w_q), bf(peer_sub_keys))
    act = _peer_score(x1b, bf(peer_u), e0, e1, gate)
    ffn = _peer_value(_peer_gate(e0, e1, act), bf(peer_v))
    return _final(x1, x1b, ffn, p, bf(ple_w_gate), bf(ple_w_proj), ln2_w, ln2_b)


def kernel(x, p, w_in, rwkv_mu, rwkv_w0, rwkv_w_lora1, rwkv_w_lora2, rwkv_a0, rwkv_a_lora1, rwkv_a_lora2, rwkv_g_lora1, rwkv_g_lora2, rwkv_k_k, rwkv_k_a, rwkv_r_k, rwkv_lnx_w, rwkv_lnx_b, w_up_a, w_up_b, w_o, ln1_w, ln1_b, peer_w_q, peer_sub_keys, peer_u, peer_v, ple_w_gate, ple_w_proj, ln2_w, ln2_b):
    bsz, t, d = x.shape
    depth = w_in.shape[0]
    xs = x.reshape(bsz * t, d)
    assert bsz == 1, "token shift / attention / scan treat the flattened rows as one sequence"
    for i in range(depth):
        xs = _layer(xs, p[i].reshape(bsz * t, -1), w_in[i], rwkv_mu[i], rwkv_w0[i], rwkv_w_lora1[i],
                    rwkv_w_lora2[i], rwkv_a0[i], rwkv_a_lora1[i], rwkv_a_lora2[i], rwkv_g_lora1[i],
                    rwkv_g_lora2[i], rwkv_k_k[i], rwkv_k_a[i], rwkv_r_k[i], rwkv_lnx_w[i], rwkv_lnx_b[i],
                    w_up_a[i], w_up_b[i], w_o[i], ln1_w[i], ln1_b[i], peer_w_q[i], peer_sub_keys[i],
                    peer_u[i], peer_v[i], ple_w_gate[i], ple_w_proj[i], ln2_w[i], ln2_b[i])
    return xs.reshape(bsz, t, d)
```

```python
import functools

import jax
import jax.numpy as jnp
from jax import lax
from jax.experimental import pallas as pl
from jax.experimental.pallas import tpu as pltpu

F32 = jnp.float32
BF16 = jnp.bfloat16
I32 = jnp.int32

D_MODEL = 2048
HEAD_DIM = 64
N_HEADS = 16
MIX_WIDTH = N_HEADS * HEAD_DIM
PEER_HEADS = 8
PEER_TOPK = 16
N_KEYS = 128
N_EXPERTS = N_KEYS * N_KEYS
N_SEL = PEER_HEADS * PEER_TOPK
GN_EPS = 64e-5
LN_EPS = 1e-5
DEEPNORM_ALPHA = 2.0 ** 0.25

LANES = 128
VMEM_LIMIT = 56 * 1024 * 1024

MM_TM, MM_TN = 1024, 1024
SB_BLK = 128
SB_SUBS = 4
SB_WIN = 2 * LANES
SB_DONE = -88.0
RW_TM = 256
RW_C = 64
RW_NCH = 8
RW_NCS = 8
MIX_TM = 256
ROUTE_TM = 256
PH_TM, PH_TN = 1024, 2048
PH_SUB = 256
PG_TT = 128
PG_GRP = 16
PV_TM, PV_NE0 = 512, 16
FIN_TM = 256

assert 2 * RW_C == LANES and RW_C == HEAD_DIM

NN = (((1,), (0,)), ((), ()))
NT = (((1,), (1,)), ((), ()))
TN = (((0,), (0,)), ((), ()))


def _cparams(*sem):
    return pltpu.CompilerParams(dimension_semantics=tuple(sem), vmem_limit_bytes=VMEM_LIMIT)


def _dg(a, b, dims=NN):
    return lax.dot_general(a, b, dims, preferred_element_type=F32)


def _split2(x):
    hi = x.astype(BF16)
    lo = (x - hi.astype(F32)).astype(BF16)
    return hi, lo


def _mm1(a, b, dims=NN):
    return _dg(a.astype(BF16), b.astype(BF16), dims)


def _mm_sum_lhs(m_bf16, x):
    hi, lo = _split2(x)
    return _dg(m_bf16, hi) + _dg(m_bf16, lo)


def _softplus(z):
    return jnp.maximum(z, 0.0) + jnp.log(1.0 + jnp.exp(-jnp.abs(z)))


def _sigmoid(z):
    return 1.0 / (1.0 + jnp.exp(-z))


def _layer_norm(x, g, b):
    mu = jnp.mean(x, axis=-1, keepdims=True)
    d = x - mu
    var = jnp.mean(d * d, axis=-1, keepdims=True)
    return d * lax.rsqrt(var + LN_EPS) * g + b


def _proj_kernel(a_ref, b_ref, o_ref):
    o_ref[...] = _dg(a_ref[...].astype(BF16), b_ref[...]).astype(o_ref.dtype)


def _proj(a, b, out_dtype):
    m, k = a.shape
    n = b.shape[1]
    tm, tn = min(MM_TM, m), min(MM_TN, n)
    return pl.pallas_call(
        _proj_kernel,
        grid=(m // tm, n // tn),
        in_specs=[pl.BlockSpec((tm, k), lambda i, j: (i, 0)),
                  pl.BlockSpec((k, tn), lambda i, j: (0, j))],
        out_specs=pl.BlockSpec((tm, tn), lambda i, j: (i, j)),
        out_shape=jax.ShapeDtypeStruct((m, n), out_dtype),
        compiler_params=_cparams("parallel", "parallel"),
    )(a, b)


def _sb_kernel(q_ref, k_ref, v_ref, o_ref, acc_ref, carry_ref):
    i = pl.program_id(1)
    blk = SB_BLK
    n_sub = q_ref.shape[0] // blk
    win = SB_WIN
    lane = lax.broadcasted_iota(I32, (1, LANES), 1)
    row = lax.broadcasted_iota(I32, (blk, win), 0)
    col = lax.broadcasted_iota(I32, (blk, win), 1)
    r_io = lax.broadcasted_iota(I32, (win, win), 0)
    c_io = lax.broadcasted_iota(I32, (win, win), 1)
    later_mat = jnp.where(r_io > c_io, 1.0, 0.0).astype(BF16)
    scale = jnp.asarray(HEAD_DIM ** -0.5, BF16)
    chains = [(s, h) for s in range(n_sub) for h in range(2)]
    qh = []
    for s, h in chains:
        q = q_ref[s * blk:(s + 1) * blk, :]
        qh.append(jnp.where((lane < HEAD_DIM) == (h == 0), q, jnp.zeros_like(q)) * scale)
    first_end = [(i * n_sub + s + 1) * blk for s in range(n_sub)]
    acc_ref[...] = jnp.zeros_like(acc_ref)
    carry_ref[...] = jnp.zeros_like(carry_ref)

    def step(it):
        kwin, vwin, valid = [], [], []
        for s in range(n_sub):
            end = first_end[s] - it * win
            start = pl.multiple_of(jnp.maximum(end - win, 0), blk)
            kwin.append(k_ref[pl.ds(start, win), :])
            vwin.append(v_ref[pl.ds(start, win), :])
            valid.append((start + col) < jnp.minimum(first_end[s] - blk + row, end))
        z = [_dg(qh[c], kwin[s], NT) for c, (s, h) in enumerate(chains)]
        sp = [_softplus(x) for x in z]
        log_keep = [jnp.where(valid[s], -sp[c], 0.0).astype(BF16) for c, (s, h) in enumerate(chains)]
        carry = [carry_ref[c] for c in range(len(chains))]
        log_later = [_dg(log_keep[c], later_mat) + jnp.concatenate([carry[c]] * (win // LANES), axis=1)
                     for c in range(len(chains))]
        w = [jnp.where(valid[s], jnp.exp(z[c] - sp[c] + log_later[c]), 0.0).astype(BF16)
             for c, (s, h) in enumerate(chains)]
        pv = [_dg(w[c], vwin[s]) for c, (s, h) in enumerate(chains)]
        for c in range(len(chains)):
            acc_ref[c] += pv[c]
            carry_ref[c] = jnp.broadcast_to(log_later[c][:, 0:1] + log_keep[c][:, 0:1], (blk, LANES))
        return it + 1

    def cond(it):
        alive = None
        for s in range(n_sub):
            more = jnp.logical_and(first_end[s] - it * win > 0, jnp.max(carry_ref[2 * s:2 * s + 2]) > SB_DONE)
            alive = more if alive is None else jnp.logical_or(alive, more)
        return alive

    lax.while_loop(cond, step, step(0))
    for s in range(n_sub):
        o_ref[s * blk:(s + 1) * blk, :] = jnp.where(lane < HEAD_DIM, acc_ref[2 * s], acc_ref[2 * s + 1]).astype(o_ref.dtype)


def _sb_attention(qkv):
    t = qkv.shape[0]
    blk = SB_BLK * SB_SUBS
    n_pairs = MIX_WIDTH // LANES
    return pl.pallas_call(
        _sb_kernel,
        grid=(n_pairs, t // blk),
        in_specs=[pl.BlockSpec((blk, LANES), lambda hp, i: (i, hp)),
                  pl.BlockSpec((t, LANES), lambda hp, i: (0, n_pairs + hp)),
                  pl.BlockSpec((t, LANES), lambda hp, i: (0, 2 * n_pairs + hp))],
        out_specs=pl.BlockSpec((blk, LANES), lambda hp, i: (i, hp)),
        out_shape=jax.ShapeDtypeStruct((t, MIX_WIDTH), BF16),
        scratch_shapes=[pltpu.VMEM((2 * SB_SUBS, SB_BLK, LANES), F32), pltpu.VMEM((2 * SB_SUBS, SB_BLK, LANES), F32)],
        compiler_params=_cparams("parallel", "parallel"),
    )(qkv, qkv, qkv)


def _rwkv_pre_kernel(cur_ref, prev_ref, mu_ref, w0_ref, wl1_ref, wl2_ref, a0_ref, al1_ref, al2_ref,
                     gl1_ref, gl2_ref, r_ref, k_ref, v_ref, lw_ref, a_ref, g_ref):
    i = pl.program_id(0)
    tm = cur_ref.shape[0]
    wd = MIX_WIDTH
    first_row = lax.broadcasted_iota(I32, (tm, wd), 0) == 0
    keep_prev = jnp.where(i == 0, 0.0, 1.0)

    n_prev = prev_ref.shape[0]

    def shifted(col):
        z = cur_ref[:, col * wd:(col + 1) * wd].astype(F32)
        last = prev_ref[n_prev - 1:n_prev, col * wd:(col + 1) * wd].astype(F32) * keep_prev
        prev = jnp.where(first_row, last, pltpu.roll(z, 1, 0))
        return z, prev - z

    z, d = shifted(0)
    r_ref[...] = (z + d * mu_ref[0:1, :]).astype(r_ref.dtype)
    z, d = shifted(1)
    k_ref[...] = (z + d * mu_ref[1:2, :]).astype(k_ref.dtype)
    z, d = shifted(2)
    v_ref[...] = (z + d * mu_ref[2:3, :]).astype(v_ref.dtype)
    z, d = shifted(3)
    xw = z + d * mu_ref[3:4, :]
    xa = z + d * mu_ref[4:5, :]
    xg = z + d * mu_ref[5:6, :]
    w = w0_ref[...] + _mm1(jnp.tanh(_mm1(xw, wl1_ref[...])), wl2_ref[...])
    w = -_softplus(-w) - 0.5
    lw_ref[...] = -jnp.exp(w)
    a_ref[...] = _sigmoid(a0_ref[...] + _mm1(_mm1(xa, al1_ref[...]), al2_ref[...])).astype(a_ref.dtype)
    g_ref[...] = _mm1(_sigmoid(_mm1(xg, gl1_ref[...])), gl2_ref[...]).astype(g_ref.dtype)


def _pad_to(x, axis, size):
    pad = [(0, 0)] * x.ndim
    pad[axis] = (0, size - x.shape[axis])
    return jnp.pad(x, pad)


def _rwkv_pre(rw, mu, w0, wl1, wl2, a0, al1, al2, gl1, gl2):
    t = rw.shape[0]
    tm = min(RW_TM, t)
    wd = MIX_WIDTH
    lo = LANES

    def lora_pair(l1, l2):
        n = -(-l1.shape[1] // lo) * lo
        return _pad_to(l1, 1, n).astype(BF16), _pad_to(l2, 0, n).astype(BF16)

    wl1, wl2 = lora_pair(wl1, wl2)
    al1, al2 = lora_pair(al1, al2)
    gl1, gl2 = lora_pair(gl1, gl2)
    full = lambda x: pl.BlockSpec(x.shape, lambda i: (0,) * x.ndim)
    row = lambda: pl.BlockSpec((tm, wd), lambda i: (i, 0))
    consts = (mu, w0.reshape(1, wd), wl1, wl2, a0.reshape(1, wd), al1, al2, gl1, gl2)
    n_prev = 32 // rw.dtype.itemsize
    out_dtypes = (BF16, BF16, BF16, F32, BF16, BF16)
    return pl.pallas_call(
        _rwkv_pre_kernel,
        grid=(t // tm,),
        in_specs=[pl.BlockSpec((tm, 4 * wd), lambda i: (i, 0)),
                  pl.BlockSpec((n_prev, 4 * wd), lambda i: (jnp.maximum(i * (tm // n_prev) - 1, 0), 0))]
                 + [full(c) for c in consts],
        out_specs=[row() for _ in range(6)],
        out_shape=[jax.ShapeDtypeStruct((t, wd), dt) for dt in out_dtypes],
        compiler_params=_cparams("parallel"),
    )(rw, rw, *consts)


def _rwkv_chunk_kernel(r_ref, k_ref, v_ref, lw_ref, a_ref, kkp_ref, kap_ref, rkp_ref,
                       rp_ref, y0_ref, bon_ref, p_ref, q_ref):
    c_len = RW_C
    n2 = 2 * c_len
    n_chunks = r_ref.shape[0] // c_len
    r2 = lax.broadcasted_iota(I32, (n2, LANES), 0)
    c2 = lax.broadcasted_iota(I32, (n2, LANES), 1)
    same = (r2 < c_len) == (c2 < HEAD_DIM)
    t_r = jnp.where(r2 < c_len, r2, r2 - c_len)
    t_c = jnp.where(c2 < HEAD_DIM, c2, c2 - HEAD_DIM)
    strict = jnp.logical_and(same, t_r > t_c)
    incl = jnp.logical_and(same, t_r >= t_c)
    eye = jnp.where(r2 == c2, 1.0, 0.0).astype(F32)
    group_ones = jnp.where(same, 1.0, 0.0).astype(BF16)
    lr = lax.broadcasted_iota(I32, (c_len, c_len), 0)
    lc = lax.broadcasted_iota(I32, (c_len, c_len), 1)
    cum_mat = jnp.where(lr >= lc, 1.0, 0.0).astype(BF16)
    kkp, kap, rkp = kkp_ref[...], kap_ref[...], rkp_ref[...]

    def stack(z):
        return jnp.where(same, jnp.concatenate([z, z], axis=0), 0.0)

    def unstack(zs):
        return zs[:c_len] + zs[c_len:]

    cs = range(n_chunks)
    rows = [pl.ds(c * c_len, c_len) for c in cs]
    r = [r_ref[rw, :].astype(F32) for rw in rows]
    kr = [k_ref[rw, :].astype(F32) for rw in rows]
    v = [v_ref[rw, :].astype(F32) for rw in rows]
    lw = [lw_ref[rw, :] for rw in rows]
    a = [a_ref[rw, :].astype(F32) for rw in rows]
    kk = [x * kkp for x in kr]
    cum = [_mm_sum_lhs(cum_mat, x) for x in lw]
    ssq = [_dg((x * x).astype(BF16), group_ones) for x in kk]
    km = [kr[c] * (1.0 + (a[c] - 1.0) * kap) for c in cs]
    bsum = [_dg((r[c] * km[c] * rkp).astype(BF16), group_ones) for c in cs]
    kk = [kk[c] / jnp.maximum(jnp.sqrt(ssq[c]), 1e-12) for c in cs]
    bv = [kk[c] * a[c] for c in cs]
    last = [x[c_len - 1:c_len, :] for x in cum]
    rt = [stack(r[c] * jnp.exp(cum[c])) for c in cs]
    at_b = [stack(-kk[c] * jnp.exp(cum[c] - lw[c])).astype(BF16) for c in cs]
    rt_b = [x.astype(BF16) for x in rt]
    g_inv = [jnp.exp(-x) for x in cum]
    btkt = [jnp.concatenate([stack(bv[c] * g_inv[c]), stack(km[c] * g_inv[c])], axis=0).astype(BF16) for c in cs]
    m_a = [_dg(at_b[c], btkt[c], NT) for c in cs]
    m_r = [_dg(rt_b[c], btkt[c], NT) for c in cs]
    vs_b = [stack(x).astype(BF16) for x in v]
    g_end = [jnp.exp(last[c] - cum[c]) for c in cs]
    bd_b = [stack(bv[c] * g_end[c]).astype(BF16) for c in cs]
    kd_b = [stack(km[c] * g_end[c]).astype(BF16) for c in cs]
    m_ab = [jnp.where(strict, x[:, :n2], 0.0) for x in m_a]
    m_ak = [jnp.where(strict, x[:, n2:], 0.0).astype(BF16) for x in m_a]
    m_rb = [jnp.where(incl, x[:, :n2], 0.0).astype(BF16) for x in m_r]
    m_rk = [jnp.where(incl, x[:, n2:], 0.0).astype(BF16) for x in m_r]
    akv = [_dg(m_ak[c], vs_b[c]) for c in cs]
    rkv = [_dg(m_rk[c], vs_b[c]) for c in cs]
    kdv = [_dg(kd_b[c], vs_b[c], TN) for c in cs]
    tinv = [eye + x for x in m_ab]
    npow = [x.astype(BF16) for x in m_ab]
    for _ in range(5):
        npow = [_dg(x, x).astype(BF16) for x in npow]
        tinv = [tinv[c] + _dg(tinv[c].astype(BF16), npow[c]) for c in cs]
    au_b = [_dg(tinv[c].astype(BF16), jnp.concatenate([at_b[c], akv[c].astype(BF16)], axis=1)).astype(BF16)
            for c in cs]
    ry = [_dg(m_rb[c], au_b[c]) for c in cs]
    pq = [_dg(bd_b[c], au_b[c], TN) for c in cs]
    for c in cs:
        prow = pl.ds(c * n2, n2)
        rp_ref[rows[c], :] = unstack(rt[c] + ry[c][:, :LANES]).astype(rp_ref.dtype)
        y0_ref[rows[c], :] = unstack(ry[c][:, LANES:] + rkv[c])
        p_ref[prow, :] = (eye * jnp.exp(last[c]) + pq[c][:, :LANES]).astype(p_ref.dtype)
        q_ref[prow, :] = pq[c][:, LANES:] + kdv[c]
        bon_ref[rows[c], :] = (bsum[c] * v[c]).astype(bon_ref.dtype)


def _rwkv_chunk(r, k, v, lw, a, k_k, k_a, r_k):
    t = r.shape[0]
    rb = min(RW_C * RW_NCH, t)
    n_pairs = MIX_WIDTH // LANES
    row = lambda: pl.BlockSpec((rb, LANES), lambda i, hp: (i, hp))
    par = lambda: pl.BlockSpec((1, LANES), lambda i, hp: (0, hp))
    mat = lambda: pl.BlockSpec((2 * rb, LANES), lambda i, hp: (i, hp))
    return pl.pallas_call(
        _rwkv_chunk_kernel,
        grid=(t // rb, n_pairs),
        in_specs=[row() for _ in range(5)] + [par() for _ in range(3)],
        out_specs=[row(), row(), row(), mat(), mat()],
        out_shape=[jax.ShapeDtypeStruct((t, MIX_WIDTH), BF16), jax.ShapeDtypeStruct((t, MIX_WIDTH), F32),
                   jax.ShapeDtypeStruct((t, MIX_WIDTH), BF16), jax.ShapeDtypeStruct((2 * t, MIX_WIDTH), BF16),
                   jax.ShapeDtypeStruct((2 * t, MIX_WIDTH), F32)],
        compiler_params=_cparams("parallel", "parallel"),
    )(r, k, v, lw, a, k_k.reshape(1, -1), k_a.reshape(1, -1), r_k.reshape(1, -1))


def _rwkv_state_kernel(rp_ref, y0_ref, p_ref, q_ref, o_ref, st_ref):
    @pl.when(pl.program_id(0) == 0)
    def _():
        st_ref[...] = jnp.zeros_like(st_ref)

    c_len = RW_C
    pairs = range(MIX_WIDTH // LANES)
    lanes = [slice(hp * LANES, (hp + 1) * LANES) for hp in pairs]
    st = [st_ref[hp] for hp in pairs]
    for c in range(rp_ref.shape[0] // c_len):
        rows = slice(c * c_len, (c + 1) * c_len)
        mrows = slice(2 * c * c_len, 2 * (c + 1) * c_len)
        st_b = [x.astype(BF16) for x in st]
        st = [_dg(p_ref[mrows, lanes[hp]], st_b[hp]) + q_ref[mrows, lanes[hp]] for hp in pairs]
        for hp in pairs:
            o_ref[rows, lanes[hp]] = _dg(rp_ref[rows, lanes[hp]], st_b[hp]) + y0_ref[rows, lanes[hp]]
    for hp in pairs:
        st_ref[hp] = st[hp]


def _rwkv_state(rp, y0, p, q):
    t = rp.shape[0]
    rb = min(RW_C * RW_NCS, t)
    wd = MIX_WIDTH
    row = lambda: pl.BlockSpec((rb, wd), lambda i: (i, 0))
    mat = lambda: pl.BlockSpec((2 * rb, wd), lambda i: (i, 0))
    return pl.pallas_call(
        _rwkv_state_kernel,
        grid=(t // rb,),
        in_specs=[row(), row(), mat(), mat()],
        out_specs=row(),
        out_shape=jax.ShapeDtypeStruct((t, wd), F32),
        scratch_shapes=[pltpu.VMEM((wd // LANES, LANES, LANES), F32)],
        compiler_params=_cparams("arbitrary"),
    )(rp, y0, p, q)


def _mix_kernel(ya_ref, y_ref, bon_ref, g_ref, gate_ref, x_ref, gnw_ref, gnb_ref, wa_ref, wb_ref, wo_ref,
                lnw_ref, lnb_ref, x1_ref, x1b_ref):
    r2 = lax.broadcasted_iota(I32, (LANES, LANES), 0)
    c2 = lax.broadcasted_iota(I32, (LANES, LANES), 1)
    group_ones = jnp.where((r2 < HEAD_DIM) == (c2 < HEAD_DIM), 1.0, 0.0).astype(BF16)
    inv_n = 1.0 / HEAD_DIM
    pieces = []
    for hp in range(MIX_WIDTH // LANES):
        ls = slice(hp * LANES, (hp + 1) * LANES)
        y = y_ref[:, ls]
        d = y - _dg(y.astype(BF16), group_ones) * inv_n
        var = _dg((d * d).astype(BF16), group_ones) * inv_n
        yn = d * lax.rsqrt(var + GN_EPS) * gnw_ref[:, ls] + gnb_ref[:, ls]
        pieces.append(((yn + bon_ref[:, ls]) * g_ref[:, ls]).astype(BF16))
    ya = _dg(ya_ref[...], wa_ref[...])
    yb = _dg(jnp.concatenate(pieces, axis=1), wb_ref[...])
    ga = _sigmoid(gate_ref[:, :D_MODEL].astype(F32))
    gb = _sigmoid(gate_ref[:, D_MODEL:].astype(F32))
    mixed = _dg((ga * ya + gb * yb).astype(BF16), wo_ref[...])
    x1 = _layer_norm(DEEPNORM_ALPHA * x_ref[...] + mixed, lnw_ref[...], lnb_ref[...])
    x1_ref[...] = x1
    x1b_ref[...] = x1.astype(BF16)


def _const_spec(x):
    return pl.BlockSpec(x.shape, lambda *_: (0,) * x.ndim, pipeline_mode=pl.Buffered(1))


def _mix(ya, y, bon, g, gates, x, gn_w, gn_b, w_up_a, w_up_b, w_o, ln_w, ln_b):
    t = x.shape[0]
    tm = min(MIX_TM, t)
    consts = (gn_w.reshape(1, -1), gn_b.reshape(1, -1), w_up_a, w_up_b, w_o, ln_w.reshape(1, -1), ln_b.reshape(1, -1))
    row = lambda w: pl.BlockSpec((tm, w), lambda i: (i, 0))
    return pl.pallas_call(
        _mix_kernel,
        grid=(t // tm,),
        in_specs=[row(MIX_WIDTH), row(MIX_WIDTH), row(MIX_WIDTH), row(MIX_WIDTH), row(2 * D_MODEL), row(D_MODEL)]
                 + [_const_spec(c) for c in consts],
        out_specs=[pl.BlockSpec((tm, D_MODEL), lambda i: (i, 0))] * 2,
        out_shape=[jax.ShapeDtypeStruct((t, D_MODEL), F32), jax.ShapeDtypeStruct((t, D_MODEL), BF16)],
        compiler_params=_cparams("parallel"),
    )(ya, y, bon, g, gates, x, *consts)


def _topk_rows(s, ids, k):
    big = jnp.asarray(1e9, F32)
    vals, idxs = [], []
    for _ in range(k):
        m = jnp.max(s, axis=0, keepdims=True)
        ix = jnp.min(jnp.where(s == m, ids, big), axis=0, keepdims=True)
        vals.append(m)
        idxs.append(ix)
        s = jnp.where(ids == ix, -jnp.inf, s)
    return jnp.concatenate(vals, axis=0), jnp.concatenate(idxs, axis=0)


def _route_kernel(x_ref, wq_ref, keys_ref, e0_ref, e1_ref, gate_ref):
    tm = x_ref.shape[0]
    topk = PEER_TOPK
    q = _dg(x_ref[...], wq_ref[...])
    qb = q.astype(BF16)
    half = N_KEYS
    key_ids = lax.broadcasted_iota(I32, (N_KEYS, tm), 0).astype(F32)
    sub8 = lax.broadcasted_iota(I32, (8, tm), 0).astype(F32)
    sub16 = lax.broadcasted_iota(I32, (topk, tm), 0).astype(F32)
    cand_ids = jnp.concatenate([sub16] + [a * topk + sub8 for a in range(1, 8)] + [(sub8 + 8.0) * topk], axis=0)
    e0s, e1s, gates = [], [], []
    for h in range(PEER_HEADS):
        tops = []
        for c in range(2):
            qs = qb[:, (2 * h + c) * half:(2 * h + c + 1) * half]
            s = _dg(keys_ref[c], qs, NT)
            tops.append(_topk_rows(s, key_ids, topk))
        (s0, i0), (s1, i1) = tops
        cand = jnp.concatenate([s0[0:1, :] + s1] + [s0[a:a + 1, :] + s1[:8, :] for a in range(1, 8)]
                               + [s0[8:, :] + s1[0:1, :]], axis=0)
        best, pos = _topk_rows(cand, cand_ids, topk)
        a_sel = jnp.floor(pos * (1.0 / topk))
        b_sel = pos - a_sel * topk
        e0 = jnp.zeros_like(pos)
        e1 = jnp.zeros_like(pos)
        for j in range(topk):
            e0 = jnp.where(a_sel == j, i0[j:j + 1, :], e0)
            e1 = jnp.where(b_sel == j, i1[j:j + 1, :], e1)
        ex = jnp.exp(best - best[0:1, :])
        gates.append(ex / jnp.sum(ex, axis=0, keepdims=True))
        e0s.append(e0)
        e1s.append(e1)
    e0_ref[...] = jnp.concatenate(e0s, axis=0).T.astype(I32)
    e1_ref[...] = jnp.concatenate(e1s, axis=0).T.astype(I32)
    gate_ref[...] = jnp.concatenate(gates, axis=0).T


def _route(x1b, w_q, sub_keys):
    t = x1b.shape[0]
    tm = min(ROUTE_TM, t)
    out = lambda: pl.BlockSpec((tm, N_SEL), lambda i: (i, 0))
    return pl.pallas_call(
        _route_kernel,
        grid=(t // tm,),
        in_specs=[pl.BlockSpec((tm, D_MODEL), lambda i: (i, 0)), _const_spec(w_q), _const_spec(sub_keys)],
        out_specs=[out(), out(), out()],
        out_shape=[jax.ShapeDtypeStruct((t, N_SEL), I32), jax.ShapeDtypeStruct((t, N_SEL), I32),
                   jax.ShapeDtypeStruct((t, N_SEL), F32)],
        compiler_params=_cparams("parallel"),
    )(x1b, w_q, sub_keys)


def _gelu_exact(x):
    return 0.5 * x * (1.0 + lax.erf(x * (2.0 ** -0.5)))


def _peer_score_kernel(x_ref, u_ref, e0_ref, e1_ref, gate_ref, o_ref, acc_ref):
    j = pl.program_id(1)

    @pl.when(j == 0)
    def _():
        acc_ref[...] = jnp.zeros_like(acc_ref)

    x = x_ref[...]
    e0, e1 = e0_ref[...], e1_ref[...]
    blocks = u_ref.shape[0] // N_KEYS
    per_dot = PH_SUB // N_KEYS
    acc = acc_ref[...]
    for s in range(u_ref.shape[0] // PH_SUB):
        h = _dg(x, u_ref[s * PH_SUB:(s + 1) * PH_SUB, :], NT)
        for b in range(per_dot):
            picked = jnp.take_along_axis(h[:, b * N_KEYS:(b + 1) * N_KEYS], e1, axis=1)
            acc = acc + jnp.where(e0 == j * blocks + s * per_dot + b, picked, 0.0)
    acc_ref[...] = acc

    @pl.when(j == pl.num_programs(1) - 1)
    def _():
        o_ref[...] = _gelu_exact(acc_ref[...]) * gate_ref[...]


def _peer_score(x1b, u_bf16, e0, e1, gate):
    t = x1b.shape[0]
    tm = min(PH_TM, t)
    tok = lambda: pl.BlockSpec((tm, N_SEL), lambda i, j: (i, 0))
    return pl.pallas_call(
        _peer_score_kernel,
        grid=(t // tm, N_EXPERTS // PH_TN),
        in_specs=[pl.BlockSpec((tm, D_MODEL), lambda i, j: (i, 0)),
                  pl.BlockSpec((PH_TN, D_MODEL), lambda i, j: (j, 0)), tok(), tok(), tok()],
        out_specs=tok(),
        out_shape=jax.ShapeDtypeStruct((t, N_SEL), F32),
        scratch_shapes=[pltpu.VMEM((tm, N_SEL), F32)],
        compiler_params=_cparams("parallel", "arbitrary"),
    )(x1b, u_bf16, e0, e1, gate)


def _peer_gate_kernel(e0_ref, e1_ref, act_ref, o_ref):
    sub = lax.broadcasted_iota(I32, (N_KEYS, N_SEL), 0)

    def body(grp, carry):
        mats = []
        for u in range(PG_GRP):
            row = pl.ds(grp * PG_GRP + u, 1)
            left = jnp.where(e0_ref[row, :] == sub, act_ref[row, :], 0.0).astype(BF16)
            right = jnp.where(e1_ref[row, :] == sub, 1.0, 0.0).astype(BF16)
            mats.append(_dg(left, right, NT))
        o_ref[grp] = jnp.swapaxes(jnp.stack(mats, axis=0), 0, 1).astype(o_ref.dtype)
        return carry

    lax.fori_loop(0, o_ref.shape[0], body, 0)


def _peer_gate(e0, e1, act):
    t = e0.shape[0]
    tt = min(PG_TT, t)
    tok = lambda: pl.BlockSpec((tt, N_SEL), lambda i: (i, 0))
    return pl.pallas_call(
        _peer_gate_kernel,
        grid=(t // tt,),
        in_specs=[tok(), tok(), tok()],
        out_specs=pl.BlockSpec((tt // PG_GRP, N_KEYS, PG_GRP, N_KEYS), lambda i: (i, 0, 0, 0)),
        out_shape=jax.ShapeDtypeStruct((t // PG_GRP, N_KEYS, PG_GRP, N_KEYS), BF16),
        compiler_params=_cparams("parallel"),
    )(e0, e1, act)


def _peer_value_kernel(g_ref, v_ref, o_ref):
    @pl.when(pl.program_id(1) == 0)
    def _():
        o_ref[...] = jnp.zeros_like(o_ref)

    tm = o_ref.shape[0]
    acc = None
    for e in range(0, g_ref.shape[1], 2):
        lhs = jnp.concatenate([g_ref[:, e, :, :].reshape(tm, N_KEYS), g_ref[:, e + 1, :, :].reshape(tm, N_KEYS)], axis=1)
        part = _dg(lhs, v_ref[e * N_KEYS:(e + 2) * N_KEYS, :])
        acc = part if acc is None else acc + part
    o_ref[...] += acc


def _peer_value(g4, v_bf16):
    t = g4.shape[0] * PG_GRP
    tm = min(PV_TM, t)
    return pl.pallas_call(
        _peer_value_kernel,
        grid=(t // tm, N_KEYS // PV_NE0),
        in_specs=[pl.BlockSpec((tm // PG_GRP, PV_NE0, PG_GRP, N_KEYS), lambda i, k: (i, k, 0, 0)),
                  pl.BlockSpec((PV_NE0 * N_KEYS, D_MODEL), lambda i, k: (k, 0))],
        out_specs=pl.BlockSpec((tm, D_MODEL), lambda i, k: (i, 0)),
        out_shape=jax.ShapeDtypeStruct((t, D_MODEL), F32),
        compiler_params=_cparams("parallel", "arbitrary"),
    )(g4, v_bf16)


def _final_kernel(x1_ref, x1b_ref, ffn_ref, p_ref, wg_ref, wp_ref, lnw_ref, lnb_ref, o_ref):
    ple = _sigmoid(_dg(x1b_ref[...], wg_ref[...])) * _dg(p_ref[...].astype(BF16), wp_ref[...])
    o_ref[...] = _layer_norm(DEEPNORM_ALPHA * x1_ref[...] + ffn_ref[...] + ple, lnw_ref[...], lnb_ref[...])


def _final(x1, x1b, ffn, p, w_gate, w_proj, ln_w, ln_b):
    t = x1.shape[0]
    tm = min(FIN_TM, t)
    consts = (w_gate, w_proj, ln_w.reshape(1, -1), ln_b.reshape(1, -1))
    row = lambda w: pl.BlockSpec((tm, w), lambda i: (i, 0))
    return pl.pallas_call(
        _final_kernel,
        grid=(t // tm,),
        in_specs=[row(D_MODEL), row(D_MODEL), row(D_MODEL), row(p.shape[1])] + [_const_spec(c) for c in consts],
        out_specs=row(D_MODEL),
        out_shape=jax.ShapeDtypeStruct((t, D_MODEL), F32),
        compiler_params=_cparams("parallel"),
    )(x1, x1b, ffn, p, *consts)


def _layer(x, p, w_in, rwkv_mu, rwkv_w0, rwkv_w_lora1, rwkv_w_lora2, rwkv_a0, rwkv_a_lora1,
           rwkv_a_lora2, rwkv_g_lora1, rwkv_g_lora2, rwkv_k_k, rwkv_k_a, rwkv_r_k, rwkv_lnx_w,
           rwkv_lnx_b, w_up_a, w_up_b, w_o, ln1_w, ln1_b, peer_w_q, peer_sub_keys, peer_u, peer_v,
           ple_w_gate, ple_w_proj, ln2_w, ln2_b):
    bf = lambda w: w.astype(BF16)
    n_sb = 3 * MIX_WIDTH
    n_rw = 4 * MIX_WIDTH
    w_in = bf(w_in)
    qkv = _proj(x, w_in[:, :n_sb], BF16)
    rw = _proj(x, w_in[:, n_sb:n_sb + n_rw], BF16)
    gates = _proj(x, w_in[:, n_sb + n_rw:], BF16)
    y_a = _sb_attention(qkv)
    r, k, v, lw, a, g = _rwkv_pre(rw, rwkv_mu, rwkv_w0, rwkv_w_lora1, rwkv_w_lora2, rwkv_a0,
                                  rwkv_a_lora1, rwkv_a_lora2, rwkv_g_lora1, rwkv_g_lora2)
    rp, y0, bon, pm, qm = _rwkv_chunk(r, k, v, lw, a, rwkv_k_k, rwkv_k_a, rwkv_r_k.reshape(-1))
    y_b = _rwkv_state(rp, y0, pm, qm)
    x1, x1b = _mix(y_a, y_b, bon, g, gates, x, rwkv_lnx_w, rwkv_lnx_b, bf(w_up_a), bf(w_up_b), bf(w_o),
                   ln1_w, ln1_b)
    e0, e1, gate = _route(x1b, bf(peer_w_q), bf(peer_sub_keys))
    act = _peer_score(x1b, bf(peer_u), e0, e1, gate)
    ffn = _peer_value(_peer_gate(e0, e1, act), bf(peer_v))
    return _final(x1, x1b, ffn, p, bf(ple_w_gate), bf(ple_w_proj), ln2_w, ln2_b)


def kernel(x, p, w_in, rwkv_mu, rwkv_w0, rwkv_w_lora1, rwkv_w_lora2, rwkv_a0, rwkv_a_lora1, rwkv_a_lora2, rwkv_g_lora1, rwkv_g_lora2, rwkv_k_k, rwkv_k_a, rwkv_r_k, rwkv_lnx_w, rwkv_lnx_b, w_up_a, w_up_b, w_o, ln1_w, ln1_b, peer_w_q, peer_sub_keys, peer_u, peer_v, ple_w_gate, ple_w_proj, ln2_w, ln2_b):
    bsz, t, d = x.shape
    depth = w_in.shape[0]
    xs = x.reshape(bsz * t, d)
    assert bsz == 1, "token shift / attention / scan treat the flattened rows as one sequence"
    for i in range(depth):
        xs = _layer(xs, p[i].reshape(bsz * t, -1), w_in[i], rwkv_mu[i], rwkv_w0[i], rwkv_w_lora1[i],
                    rwkv_w_lora2[i], rwkv_a0[i], rwkv_a_lora1[i], rwkv_a_lora2[i], rwkv_g_lora1[i],
                    rwkv_g_lora2[i], rwkv_k_k[i], rwkv_k_a[i], rwkv_r_k[i], rwkv_lnx_w[i], rwkv_lnx_b[i],
                    w_up_a[i], w_up_b[i], w_o[i], ln1_w[i], ln1_b[i], peer_w_q[i], peer_sub_keys[i],
                    peer_u[i], peer_v[i], ple_w_gate[i], ple_w_proj[i], ln2_w[i], ln2_b[i])
    return xs.reshape(bsz, t, d)
```

```python
import functools

import jax
import jax.numpy as jnp
from jax import lax
from jax.experimental import pallas as pl
from jax.experimental.pallas import tpu as pltpu

F32 = jnp.float32
BF16 = jnp.bfloat16
I32 = jnp.int32

D_MODEL = 2048
HEAD_DIM = 64
N_HEADS = 16
MIX_WIDTH = N_HEADS * HEAD_DIM
PEER_HEADS = 8
PEER_TOPK = 16
N_KEYS = 128
N_EXPERTS = N_KEYS * N_KEYS
N_SEL = PEER_HEADS * PEER_TOPK
GN_EPS = 64e-5
LN_EPS = 1e-5
DEEPNORM_ALPHA = 2.0 ** 0.25

LANES = 128
VMEM_LIMIT = 56 * 1024 * 1024

MM_TM, MM_TN = 1024, 1024
SB_BLK = 128
SB_SUBS = 4
SB_WIN = 2 * LANES
SB_DONE = -88.0
RW_TM = 256
RW_C = 64
RW_NCH = 8
RW_NCS = 8
MIX_TM = 256
ROUTE_TM = 256
PH_TM, PH_TN = 1024, 2048
PH_SUB = 256
PG_TT = 128
PG_GRP = 16
PV_TM, PV_NE0 = 512, 16
FIN_TM = 256

assert 2 * RW_C == LANES and RW_C == HEAD_DIM

NN = (((1,), (0,)), ((), ()))
NT = (((1,), (1,)), ((), ()))
TN = (((0,), (0,)), ((), ()))


def _cparams(*sem):
    return pltpu.CompilerParams(dimension_semantics=tuple(sem), vmem_limit_bytes=VMEM_LIMIT)


def _dg(a, b, dims=NN):
    return lax.dot_general(a, b, dims, preferred_element_type=F32)


def _split2(x):
    hi = x.astype(BF16)
    lo = (x - hi.astype(F32)).astype(BF16)
    return hi, lo


def _mm1(a, b, dims=NN):
    return _dg(a.astype(BF16), b.astype(BF16), dims)


def _mm_sum_lhs(m_bf16, x):
    hi, lo = _split2(x)
    return _dg(m_bf16, hi) + _dg(m_bf16, lo)


def _softplus(z):
    return jnp.maximum(z, 0.0) + jnp.log(1.0 + jnp.exp(-jnp.abs(z)))


def _sigmoid(z):
    return 1.0 / (1.0 + jnp.exp(-z))


def _layer_norm(x, g, b):
    mu = jnp.mean(x, axis=-1, keepdims=True)
    d = x - mu
    var = jnp.mean(d * d, axis=-1, keepdims=True)
    return d * lax.rsqrt(var + LN_EPS) * g + b


def _proj_kernel(a_ref, b_ref, o_ref):
    o_ref[...] = _dg(a_ref[...].astype(BF16), b_ref[...].astype(BF16)).astype(o_ref.dtype)


def _proj(a, b, col0, n, out_dtype):
    m, k = a.shape
    tm, tn = min(MM_TM, m), MM_TN
    assert col0 % tn == 0 and n % tn == 0
    off = col0 // tn
    return pl.pallas_call(
        _proj_kernel,
        grid=(m // tm, n // tn),
        in_specs=[pl.BlockSpec((tm, k), lambda i, j: (i, 0)),
                  pl.BlockSpec((k, tn), lambda i, j: (0, j + off))],
        out_specs=pl.BlockSpec((tm, tn), lambda i, j: (i, j)),
        out_shape=jax.ShapeDtypeStruct((m, n), out_dtype),
        compiler_params=_cparams("parallel", "parallel"),
    )(a, b)


def _sb_kernel(q_ref, k_ref, v_ref, o_ref, acc_ref, carry_ref):
    i = pl.program_id(1)
    blk = SB_BLK
    n_sub = q_ref.shape[0] // blk
    win = SB_WIN
    lane = lax.broadcasted_iota(I32, (1, LANES), 1)
    row = lax.broadcasted_iota(I32, (blk, win), 0)
    col = lax.broadcasted_iota(I32, (blk, win), 1)
    r_io = lax.broadcasted_iota(I32, (win, win), 0)
    c_io = lax.broadcasted_iota(I32, (win, win), 1)
    later_mat = jnp.where(r_io > c_io, 1.0, 0.0).astype(BF16)
    scale = jnp.asarray(HEAD_DIM ** -0.5, BF16)
    chains = [(s, h) for s in range(n_sub) for h in range(2)]
    qh = []
    for s, h in chains:
        q = q_ref[s * blk:(s + 1) * blk, :]
        qh.append(jnp.where((lane < HEAD_DIM) == (h == 0), q, jnp.zeros_like(q)) * scale)
    first_end = [(i * n_sub + s + 1) * blk for s in range(n_sub)]
    acc_ref[...] = jnp.zeros_like(acc_ref)
    carry_ref[...] = jnp.zeros_like(carry_ref)

    def step(it):
        kwin, vwin, valid = [], [], []
        for s in range(n_sub):
            end = first_end[s] - it * win
            start = pl.multiple_of(jnp.maximum(end - win, 0), blk)
            kwin.append(k_ref[pl.ds(start, win), :])
            vwin.append(v_ref[pl.ds(start, win), :])
            valid.append((start + col) < jnp.minimum(first_end[s] - blk + row, end))
        z = [_dg(qh[c], kwin[s], NT) for c, (s, h) in enumerate(chains)]
        sp = [_softplus(x) for x in z]
        log_keep = [jnp.where(valid[s], -sp[c], 0.0).astype(BF16) for c, (s, h) in enumerate(chains)]
        carry = [carry_ref[c] for c in range(len(chains))]
        log_later = [_dg(log_keep[c], later_mat) + jnp.concatenate([carry[c]] * (win // LANES), axis=1)
                     for c in range(len(chains))]
        w = [jnp.where(valid[s], jnp.exp(z[c] - sp[c] + log_later[c]), 0.0).astype(BF16)
             for c, (s, h) in enumerate(chains)]
        pv = [_dg(w[c], vwin[s]) for c, (s, h) in enumerate(chains)]
        for c in range(len(chains)):
            acc_ref[c] += pv[c]
            carry_ref[c] = jnp.broadcast_to(log_later[c][:, 0:1] + log_keep[c][:, 0:1], (blk, LANES))
        return it + 1

    def cond(it):
        alive = None
        for s in range(n_sub):
            more = jnp.logical_and(first_end[s] - it * win > 0, jnp.max(carry_ref[2 * s:2 * s + 2]) > SB_DONE)
            alive = more if alive is None else jnp.logical_or(alive, more)
        return alive

    lax.while_loop(cond, step, step(0))
    for s in range(n_sub):
        o_ref[s * blk:(s + 1) * blk, :] = jnp.where(lane < HEAD_DIM, acc_ref[2 * s], acc_ref[2 * s + 1]).astype(o_ref.dtype)


def _sb_attention(qkv):
    t = qkv.shape[0]
    blk = SB_BLK * SB_SUBS
    n_pairs = MIX_WIDTH // LANES
    return pl.pallas_call(
        _sb_kernel,
        grid=(n_pairs, t // blk),
        in_specs=[pl.BlockSpec((blk, LANES), lambda hp, i: (i, hp)),
                  pl.BlockSpec((t, LANES), lambda hp, i: (0, n_pairs + hp)),
                  pl.BlockSpec((t, LANES), lambda hp, i: (0, 2 * n_pairs + hp))],
        out_specs=pl.BlockSpec((blk, LANES), lambda hp, i: (i, hp)),
        out_shape=jax.ShapeDtypeStruct((t, MIX_WIDTH), BF16),
        scratch_shapes=[pltpu.VMEM((2 * SB_SUBS, SB_BLK, LANES), F32), pltpu.VMEM((2 * SB_SUBS, SB_BLK, LANES), F32)],
        compiler_params=_cparams("parallel", "parallel"),
    )(qkv, qkv, qkv)


def _rwkv_pre_kernel(cur_ref, prev_ref, mu_ref, w0_ref, wl1_ref, wl2_ref, a0_ref, al1_ref, al2_ref,
                     gl1_ref, gl2_ref, r_ref, k_ref, v_ref, lw_ref, a_ref, g_ref):
    i = pl.program_id(0)
    tm = cur_ref.shape[0]
    wd = MIX_WIDTH
    first_row = lax.broadcasted_iota(I32, (tm, wd), 0) == 0
    keep_prev = jnp.where(i == 0, 0.0, 1.0)

    n_prev = prev_ref.shape[0]

    def shifted(col):
        z = cur_ref[:, col * wd:(col + 1) * wd].astype(F32)
        last = prev_ref[n_prev - 1:n_prev, col * wd:(col + 1) * wd].astype(F32) * keep_prev
        prev = jnp.where(first_row, last, pltpu.roll(z, 1, 0))
        return z, prev - z

    z, d = shifted(0)
    r_ref[...] = (z + d * mu_ref[0:1, :]).astype(r_ref.dtype)
    z, d = shifted(1)
    k_ref[...] = (z + d * mu_ref[1:2, :]).astype(k_ref.dtype)
    z, d = shifted(2)
    v_ref[...] = (z + d * mu_ref[2:3, :]).astype(v_ref.dtype)
    z, d = shifted(3)
    xw = z + d * mu_ref[3:4, :]
    xa = z + d * mu_ref[4:5, :]
    xg = z + d * mu_ref[5:6, :]
    w = w0_ref[...] + _mm1(jnp.tanh(_mm1(xw, wl1_ref[...])), wl2_ref[...])
    w = -_softplus(-w) - 0.5
    lw_ref[...] = -jnp.exp(w)
    a_ref[...] = _sigmoid(a0_ref[...] + _mm1(_mm1(xa, al1_ref[...]), al2_ref[...])).astype(a_ref.dtype)
    g_ref[...] = _mm1(_sigmoid(_mm1(xg, gl1_ref[...])), gl2_ref[...]).astype(g_ref.dtype)


def _pad_to(x, axis, size):
    pad = [(0, 0)] * x.ndim
    pad[axis] = (0, size - x.shape[axis])
    return jnp.pad(x, pad)


def _rwkv_pre(rw, mu, w0, wl1, wl2, a0, al1, al2, gl1, gl2):
    t = rw.shape[0]
    tm = min(RW_TM, t)
    wd = MIX_WIDTH
    lo = LANES

    def lora_pair(l1, l2):
        n = -(-l1.shape[1] // lo) * lo
        return _pad_to(l1, 1, n).astype(BF16), _pad_to(l2, 0, n).astype(BF16)

    wl1, wl2 = lora_pair(wl1, wl2)
    al1, al2 = lora_pair(al1, al2)
    gl1, gl2 = lora_pair(gl1, gl2)
    full = lambda x: pl.BlockSpec(x.shape, lambda i: (0,) * x.ndim)
    row = lambda: pl.BlockSpec((tm, wd), lambda i: (i, 0))
    consts = (mu, w0.reshape(1, wd), wl1, wl2, a0.reshape(1, wd), al1, al2, gl1, gl2)
    n_prev = 32 // rw.dtype.itemsize
    out_dtypes = (BF16, BF16, BF16, F32, BF16, BF16)
    return pl.pallas_call(
        _rwkv_pre_kernel,
        grid=(t // tm,),
        in_specs=[pl.BlockSpec((tm, 4 * wd), lambda i: (i, 0)),
                  pl.BlockSpec((n_prev, 4 * wd), lambda i: (jnp.maximum(i * (tm // n_prev) - 1, 0), 0))]
                 + [full(c) for c in consts],
        out_specs=[row() for _ in range(6)],
        out_shape=[jax.ShapeDtypeStruct((t, wd), dt) for dt in out_dtypes],
        compiler_params=_cparams("parallel"),
    )(rw, rw, *consts)


def _rwkv_chunk_kernel(r_ref, k_ref, v_ref, lw_ref, a_ref, kkp_ref, kap_ref, rkp_ref,
                       rp_ref, y0_ref, bon_ref, p_ref, q_ref):
    c_len = RW_C
    n2 = 2 * c_len
    n_chunks = r_ref.shape[0] // c_len
    r2 = lax.broadcasted_iota(I32, (n2, LANES), 0)
    c2 = lax.broadcasted_iota(I32, (n2, LANES), 1)
    same = (r2 < c_len) == (c2 < HEAD_DIM)
    t_r = jnp.where(r2 < c_len, r2, r2 - c_len)
    t_c = jnp.where(c2 < HEAD_DIM, c2, c2 - HEAD_DIM)
    strict = jnp.logical_and(same, t_r > t_c)
    incl = jnp.logical_and(same, t_r >= t_c)
    eye = jnp.where(r2 == c2, 1.0, 0.0).astype(F32)
    group_ones = jnp.where(same, 1.0, 0.0).astype(BF16)
    lr = lax.broadcasted_iota(I32, (c_len, c_len), 0)
    lc = lax.broadcasted_iota(I32, (c_len, c_len), 1)
    cum_mat = jnp.where(lr >= lc, 1.0, 0.0).astype(BF16)
    kkp, kap, rkp = kkp_ref[...], kap_ref[...], rkp_ref[...]

    def stack(z):
        return jnp.where(same, jnp.concatenate([z, z], axis=0), 0.0)

    def unstack(zs):
        return zs[:c_len] + zs[c_len:]

    cs = range(n_chunks)
    rows = [pl.ds(c * c_len, c_len) for c in cs]
    r = [r_ref[rw, :].astype(F32) for rw in rows]
    kr = [k_ref[rw, :].astype(F32) for rw in rows]
    v = [v_ref[rw, :].astype(F32) for rw in rows]
    lw = [lw_ref[rw, :] for rw in rows]
    a = [a_ref[rw, :].astype(F32) for rw in rows]
    kk = [x * kkp for x in kr]
    cum = [_mm_sum_lhs(cum_mat, x) for x in lw]
    ssq = [_dg((x * x).astype(BF16), group_ones) for x in kk]
    km = [kr[c] * (1.0 + (a[c] - 1.0) * kap) for c in cs]
    bsum = [_dg((r[c] * km[c] * rkp).astype(BF16), group_ones) for c in cs]
    kk = [kk[c] / jnp.maximum(jnp.sqrt(ssq[c]), 1e-12) for c in cs]
    bv = [kk[c] * a[c] for c in cs]
    last = [x[c_len - 1:c_len, :] for x in cum]
    rt = [stack(r[c] * jnp.exp(cum[c])) for c in cs]
    at_b = [stack(-kk[c] * jnp.exp(cum[c] - lw[c])).astype(BF16) for c in cs]
    rt_b = [x.astype(BF16) for x in rt]
    g_inv = [jnp.exp(-x) for x in cum]
    btkt = [jnp.concatenate([stack(bv[c] * g_inv[c]), stack(km[c] * g_inv[c])], axis=0).astype(BF16) for c in cs]
    m_a = [_dg(at_b[c], btkt[c], NT) for c in cs]
    m_r = [_dg(rt_b[c], btkt[c], NT) for c in cs]
    vs_b = [stack(x).astype(BF16) for x in v]
    g_end = [jnp.exp(last[c] - cum[c]) for c in cs]
    bd_b = [stack(bv[c] * g_end[c]).astype(BF16) for c in cs]
    kd_b = [stack(km[c] * g_end[c]).astype(BF16) for c in cs]
    m_ab = [jnp.where(strict, x[:, :n2], 0.0) for x in m_a]
    m_ak = [jnp.where(strict, x[:, n2:], 0.0).astype(BF16) for x in m_a]
    m_rb = [jnp.where(incl, x[:, :n2], 0.0).astype(BF16) for x in m_r]
    m_rk = [jnp.where(incl, x[:, n2:], 0.0).astype(BF16) for x in m_r]
    akv = [_dg(m_ak[c], vs_b[c]) for c in cs]
    rkv = [_dg(m_rk[c], vs_b[c]) for c in cs]
    kdv = [_dg(kd_b[c], vs_b[c], TN) for c in cs]
    tinv = [eye + x for x in m_ab]
    npow = [x.astype(BF16) for x in m_ab]
    for _ in range(5):
        npow = [_dg(x, x).astype(BF16) for x in npow]
        tinv = [tinv[c] + _dg(tinv[c].astype(BF16), npow[c]) for c in cs]
    au_b = [_dg(tinv[c].astype(BF16), jnp.concatenate([at_b[c], akv[c].astype(BF16)], axis=1)).astype(BF16)
            for c in cs]
    ry = [_dg(m_rb[c], au_b[c]) for c in cs]
    pq = [_dg(bd_b[c], au_b[c], TN) for c in cs]
    for c in cs:
        prow = pl.ds(c * n2, n2)
        rp_ref[rows[c], :] = unstack(rt[c] + ry[c][:, :LANES]).astype(rp_ref.dtype)
        y0_ref[rows[c], :] = unstack(ry[c][:, LANES:] + rkv[c])
        p_ref[prow, :] = (eye * jnp.exp(last[c]) + pq[c][:, :LANES]).astype(p_ref.dtype)
        q_ref[prow, :] = pq[c][:, LANES:] + kdv[c]
        bon_ref[rows[c], :] = (bsum[c] * v[c]).astype(bon_ref.dtype)


def _rwkv_chunk(r, k, v, lw, a, k_k, k_a, r_k):
    t = r.shape[0]
    rb = min(RW_C * RW_NCH, t)
    n_pairs = MIX_WIDTH // LANES
    row = lambda: pl.BlockSpec((rb, LANES), lambda i, hp: (i, hp))
    par = lambda: pl.BlockSpec((1, LANES), lambda i, hp: (0, hp))
    mat = lambda: pl.BlockSpec((2 * rb, LANES), lambda i, hp: (i, hp))
    return pl.pallas_call(
        _rwkv_chunk_kernel,
        grid=(t // rb, n_pairs),
        in_specs=[row() for _ in range(5)] + [par() for _ in range(3)],
        out_specs=[row(), row(), row(), mat(), mat()],
        out_shape=[jax.ShapeDtypeStruct((t, MIX_WIDTH), BF16), jax.ShapeDtypeStruct((t, MIX_WIDTH), F32),
                   jax.ShapeDtypeStruct((t, MIX_WIDTH), BF16), jax.ShapeDtypeStruct((2 * t, MIX_WIDTH), BF16),
                   jax.ShapeDtypeStruct((2 * t, MIX_WIDTH), F32)],
        compiler_params=_cparams("parallel", "parallel"),
    )(r, k, v, lw, a, k_k.reshape(1, -1), k_a.reshape(1, -1), r_k.reshape(1, -1))


def _rwkv_state_kernel(rp_ref, y0_ref, p_ref, q_ref, o_ref, st_ref):
    @pl.when(pl.program_id(0) == 0)
    def _():
        st_ref[...] = jnp.zeros_like(st_ref)

    c_len = RW_C
    pairs = range(MIX_WIDTH // LANES)
    lanes = [slice(hp * LANES, (hp + 1) * LANES) for hp in pairs]
    st = [st_ref[hp] for hp in pairs]
    for c in range(rp_ref.shape[0] // c_len):
        rows = slice(c * c_len, (c + 1) * c_len)
        mrows = slice(2 * c * c_len, 2 * (c + 1) * c_len)
        st_b = [x.astype(BF16) for x in st]
        st = [_dg(p_ref[mrows, lanes[hp]], st_b[hp]) + q_ref[mrows, lanes[hp]] for hp in pairs]
        for hp in pairs:
            o_ref[rows, lanes[hp]] = _dg(rp_ref[rows, lanes[hp]], st_b[hp]) + y0_ref[rows, lanes[hp]]
    for hp in pairs:
        st_ref[hp] = st[hp]


def _rwkv_state(rp, y0, p, q):
    t = rp.shape[0]
    rb = min(RW_C * RW_NCS, t)
    wd = MIX_WIDTH
    row = lambda: pl.BlockSpec((rb, wd), lambda i: (i, 0))
    mat = lambda: pl.BlockSpec((2 * rb, wd), lambda i: (i, 0))
    return pl.pallas_call(
        _rwkv_state_kernel,
        grid=(t // rb,),
        in_specs=[row(), row(), mat(), mat()],
        out_specs=row(),
        out_shape=jax.ShapeDtypeStruct((t, wd), F32),
        scratch_shapes=[pltpu.VMEM((wd // LANES, LANES, LANES), F32)],
        compiler_params=_cparams("arbitrary"),
    )(rp, y0, p, q)


def _mix_kernel(ya_ref, y_ref, bon_ref, g_ref, gate_ref, x_ref, gnw_ref, gnb_ref, wa_ref, wb_ref, wo_ref,
                lnw_ref, lnb_ref, x1_ref, x1b_ref):
    r2 = lax.broadcasted_iota(I32, (LANES, LANES), 0)
    c2 = lax.broadcasted_iota(I32, (LANES, LANES), 1)
    group_ones = jnp.where((r2 < HEAD_DIM) == (c2 < HEAD_DIM), 1.0, 0.0).astype(BF16)
    inv_n = 1.0 / HEAD_DIM
    pieces = []
    for hp in range(MIX_WIDTH // LANES):
        ls = slice(hp * LANES, (hp + 1) * LANES)
        y = y_ref[:, ls]
        d = y - _dg(y.astype(BF16), group_ones) * inv_n
        var = _dg((d * d).astype(BF16), group_ones) * inv_n
        yn = d * lax.rsqrt(var + GN_EPS) * gnw_ref[:, ls] + gnb_ref[:, ls]
        pieces.append(((yn + bon_ref[:, ls]) * g_ref[:, ls]).astype(BF16))
    ya = _dg(ya_ref[...], wa_ref[...])
    yb = _dg(jnp.concatenate(pieces, axis=1), wb_ref[...])
    ga = _sigmoid(gate_ref[:, :D_MODEL].astype(F32))
    gb = _sigmoid(gate_ref[:, D_MODEL:].astype(F32))
    mixed = _dg((ga * ya + gb * yb).astype(BF16), wo_ref[...])
    x1 = _layer_norm(DEEPNORM_ALPHA * x_ref[...] + mixed, lnw_ref[...], lnb_ref[...])
    x1_ref[...] = x1
    x1b_ref[...] = x1.astype(BF16)


def _const_spec(x):
    return pl.BlockSpec(x.shape, lambda *_: (0,) * x.ndim, pipeline_mode=pl.Buffered(1))


def _mix(ya, y, bon, g, gates, x, gn_w, gn_b, w_up_a, w_up_b, w_o, ln_w, ln_b):
    t = x.shape[0]
    tm = min(MIX_TM, t)
    consts = (gn_w.reshape(1, -1), gn_b.reshape(1, -1), w_up_a, w_up_b, w_o, ln_w.reshape(1, -1), ln_b.reshape(1, -1))
    row = lambda w: pl.BlockSpec((tm, w), lambda i: (i, 0))
    return pl.pallas_call(
        _mix_kernel,
        grid=(t // tm,),
        in_specs=[row(MIX_WIDTH), row(MIX_WIDTH), row(MIX_WIDTH), row(MIX_WIDTH), row(2 * D_MODEL), row(D_MODEL)]
                 + [_const_spec(c) for c in consts],
        out_specs=[pl.BlockSpec((tm, D_MODEL), lambda i: (i, 0))] * 2,
        out_shape=[jax.ShapeDtypeStruct((t, D_MODEL), F32), jax.ShapeDtypeStruct((t, D_MODEL), BF16)],
        compiler_params=_cparams("parallel"),
    )(ya, y, bon, g, gates, x, *consts)


def _topk_rows(s, ids, k):
    big = jnp.asarray(1e9, F32)
    vals, idxs = [], []
    for _ in range(k):
        m = jnp.max(s, axis=0, keepdims=True)
        ix = jnp.min(jnp.where(s == m, ids, big), axis=0, keepdims=True)
        vals.append(m)
        idxs.append(ix)
        s = jnp.where(ids == ix, -jnp.inf, s)
    return jnp.concatenate(vals, axis=0), jnp.concatenate(idxs, axis=0)


def _route_kernel(x_ref, wq_ref, keys_ref, e0_ref, e1_ref, gate_ref):
    tm = x_ref.shape[0]
    topk = PEER_TOPK
    q = _dg(x_ref[...], wq_ref[...])
    qb = q.astype(BF16)
    half = N_KEYS
    key_ids = lax.broadcasted_iota(I32, (N_KEYS, tm), 0).astype(F32)
    sub8 = lax.broadcasted_iota(I32, (8, tm), 0).astype(F32)
    sub16 = lax.broadcasted_iota(I32, (topk, tm), 0).astype(F32)
    cand_ids = jnp.concatenate([sub16] + [a * topk + sub8 for a in range(1, 8)] + [(sub8 + 8.0) * topk], axis=0)
    e0s, e1s, gates = [], [], []
    for h in range(PEER_HEADS):
        tops = []
        for c in range(2):
            qs = qb[:, (2 * h + c) * half:(2 * h + c + 1) * half]
            s = _dg(keys_ref[c], qs, NT)
            tops.append(_topk_rows(s, key_ids, topk))
        (s0, i0), (s1, i1) = tops
        cand = jnp.concatenate([s0[0:1, :] + s1] + [s0[a:a + 1, :] + s1[:8, :] for a in range(1, 8)]
                               + [s0[8:, :] + s1[0:1, :]], axis=0)
        best, pos = _topk_rows(cand, cand_ids, topk)
        a_sel = jnp.floor(pos * (1.0 / topk))
        b_sel = pos - a_sel * topk
        e0 = jnp.zeros_like(pos)
        e1 = jnp.zeros_like(pos)
        for j in range(topk):
            e0 = jnp.where(a_sel == j, i0[j:j + 1, :], e0)
            e1 = jnp.where(b_sel == j, i1[j:j + 1, :], e1)
        ex = jnp.exp(best - best[0:1, :])
        gates.append(ex / jnp.sum(ex, axis=0, keepdims=True))
        e0s.append(e0)
        e1s.append(e1)
    e0_ref[...] = jnp.concatenate(e0s, axis=0).T.astype(I32)
    e1_ref[...] = jnp.concatenate(e1s, axis=0).T.astype(I32)
    gate_ref[...] = jnp.concatenate(gates, axis=0).T


def _route(x1b, w_q, sub_keys):
    t = x1b.shape[0]
    tm = min(ROUTE_TM, t)
    out = lambda: pl.BlockSpec((tm, N_SEL), lambda i: (i, 0))
    return pl.pallas_call(
        _route_kernel,
        grid=(t // tm,),
        in_specs=[pl.BlockSpec((tm, D_MODEL), lambda i: (i, 0)), _const_spec(w_q), _const_spec(sub_keys)],
        out_specs=[out(), out(), out()],
        out_shape=[jax.ShapeDtypeStruct((t, N_SEL), I32), jax.ShapeDtypeStruct((t, N_SEL), I32),
                   jax.ShapeDtypeStruct((t, N_SEL), F32)],
        compiler_params=_cparams("parallel"),
    )(x1b, w_q, sub_keys)


def _gelu_exact(x):
    return 0.5 * x * (1.0 + lax.erf(x * (2.0 ** -0.5)))


def _peer_score_kernel(x_ref, u_ref, e0_ref, e1_ref, gate_ref, o_ref, acc_ref):
    j = pl.program_id(1)

    @pl.when(j == 0)
    def _():
        acc_ref[...] = jnp.zeros_like(acc_ref)

    x = x_ref[...]
    e0, e1 = e0_ref[...], e1_ref[...]
    blocks = u_ref.shape[0] // N_KEYS
    per_dot = PH_SUB // N_KEYS
    acc = acc_ref[...]
    for s in range(u_ref.shape[0] // PH_SUB):
        h = _dg(x, u_ref[s * PH_SUB:(s + 1) * PH_SUB, :].astype(BF16), NT)
        for b in range(per_dot):
            picked = jnp.take_along_axis(h[:, b * N_KEYS:(b + 1) * N_KEYS], e1, axis=1)
            acc = acc + jnp.where(e0 == j * blocks + s * per_dot + b, picked, 0.0)
    acc_ref[...] = acc

    @pl.when(j == pl.num_programs(1) - 1)
    def _():
        o_ref[...] = _gelu_exact(acc_ref[...]) * gate_ref[...]


def _peer_score(x1b, u_bf16, e0, e1, gate):
    t = x1b.shape[0]
    tm = min(PH_TM, t)
    tok = lambda: pl.BlockSpec((tm, N_SEL), lambda i, j: (i, 0))
    return pl.pallas_call(
        _peer_score_kernel,
        grid=(t // tm, N_EXPERTS // PH_TN),
        in_specs=[pl.BlockSpec((tm, D_MODEL), lambda i, j: (i, 0)),
                  pl.BlockSpec((PH_TN, D_MODEL), lambda i, j: (j, 0)), tok(), tok(), tok()],
        out_specs=tok(),
        out_shape=jax.ShapeDtypeStruct((t, N_SEL), F32),
        scratch_shapes=[pltpu.VMEM((tm, N_SEL), F32)],
        compiler_params=_cparams("parallel", "arbitrary"),
    )(x1b, u_bf16, e0, e1, gate)


def _peer_gate_kernel(e0_ref, e1_ref, act_ref, o_ref):
    sub = lax.broadcasted_iota(I32, (N_KEYS, N_SEL), 0)

    def body(grp, carry):
        mats = []
        for u in range(PG_GRP):
            row = pl.ds(grp * PG_GRP + u, 1)
            left = jnp.where(e0_ref[row, :] == sub, act_ref[row, :], 0.0).astype(BF16)
            right = jnp.where(e1_ref[row, :] == sub, 1.0, 0.0).astype(BF16)
            mats.append(_dg(left, right, NT))
        o_ref[grp] = jnp.swapaxes(jnp.stack(mats, axis=0), 0, 1).astype(o_ref.dtype)
        return carry

    lax.fori_loop(0, o_ref.shape[0], body, 0)


def _peer_gate(e0, e1, act):
    t = e0.shape[0]
    tt = min(PG_TT, t)
    tok = lambda: pl.BlockSpec((tt, N_SEL), lambda i: (i, 0))
    return pl.pallas_call(
        _peer_gate_kernel,
        grid=(t // tt,),
        in_specs=[tok(), tok(), tok()],
        out_specs=pl.BlockSpec((tt // PG_GRP, N_KEYS, PG_GRP, N_KEYS), lambda i: (i, 0, 0, 0)),
        out_shape=jax.ShapeDtypeStruct((t // PG_GRP, N_KEYS, PG_GRP, N_KEYS), BF16),
        compiler_params=_cparams("parallel"),
    )(e0, e1, act)


def _peer_value_kernel(g_ref, v_ref, o_ref):
    @pl.when(pl.program_id(1) == 0)
    def _():
        o_ref[...] = jnp.zeros_like(o_ref)

    tm = o_ref.shape[0]
    acc = None
    for e in range(0, g_ref.shape[1], 2):
        lhs = jnp.concatenate([g_ref[:, e, :, :].reshape(tm, N_KEYS), g_ref[:, e + 1, :, :].reshape(tm, N_KEYS)], axis=1)
        part = _dg(lhs, v_ref[e * N_KEYS:(e + 2) * N_KEYS, :])
        acc = part if acc is None else acc + part
    o_ref[...] += acc


def _peer_value(g4, v_bf16):
    t = g4.shape[0] * PG_GRP
    tm = min(PV_TM, t)
    return pl.pallas_call(
        _peer_value_kernel,
        grid=(t // tm, N_KEYS // PV_NE0),
        in_specs=[pl.BlockSpec((tm // PG_GRP, PV_NE0, PG_GRP, N_KEYS), lambda i, k: (i, k, 0, 0)),
                  pl.BlockSpec((PV_NE0 * N_KEYS, D_MODEL), lambda i, k: (k, 0))],
        out_specs=pl.BlockSpec((tm, D_MODEL), lambda i, k: (i, 0)),
        out_shape=jax.ShapeDtypeStruct((t, D_MODEL), F32),
        compiler_params=_cparams("parallel", "arbitrary"),
    )(g4, v_bf16)


def _final_kernel(x1_ref, x1b_ref, ffn_ref, p_ref, wg_ref, wp_ref, lnw_ref, lnb_ref, o_ref):
    ple = _sigmoid(_dg(x1b_ref[...], wg_ref[...])) * _dg(p_ref[...].astype(BF16), wp_ref[...])
    o_ref[...] = _layer_norm(DEEPNORM_ALPHA * x1_ref[...] + ffn_ref[...] + ple, lnw_ref[...], lnb_ref[...])


def _final(x1, x1b, ffn, p, w_gate, w_proj, ln_w, ln_b):
    t = x1.shape[0]
    tm = min(FIN_TM, t)
    consts = (w_gate, w_proj, ln_w.reshape(1, -1), ln_b.reshape(1, -1))
    row = lambda w: pl.BlockSpec((tm, w), lambda i: (i, 0))
    return pl.pallas_call(
        _final_kernel,
        grid=(t // tm,),
        in_specs=[row(D_MODEL), row(D_MODEL), row(D_MODEL), row(p.shape[1])] + [_const_spec(c) for c in consts],
        out_specs=row(D_MODEL),
        out_shape=jax.ShapeDtypeStruct((t, D_MODEL), F32),
        compiler_params=_cparams("parallel"),
    )(x1, x1b, ffn, p, *consts)


def _layer(x, p, w_in, rwkv_mu, rwkv_w0, rwkv_w_lora1, rwkv_w_lora2, rwkv_a0, rwkv_a_lora1,
           rwkv_a_lora2, rwkv_g_lora1, rwkv_g_lora2, rwkv_k_k, rwkv_k_a, rwkv_r_k, rwkv_lnx_w,
           rwkv_lnx_b, w_up_a, w_up_b, w_o, ln1_w, ln1_b, peer_w_q, peer_sub_keys, peer_u, peer_v,
           ple_w_gate, ple_w_proj, ln2_w, ln2_b):
    bf = lambda w: w.astype(BF16)
    n_sb = 3 * MIX_WIDTH
    n_rw = 4 * MIX_WIDTH
    qkv = _proj(x, w_in, 0, n_sb, BF16)
    rw = _proj(x, w_in, n_sb, n_rw, BF16)
    gates = _proj(x, w_in, n_sb + n_rw, 2 * D_MODEL, BF16)
    y_a = _sb_attention(qkv)
    r, k, v, lw, a, g = _rwkv_pre(rw, rwkv_mu, rwkv_w0, rwkv_w_lora1, rwkv_w_lora2, rwkv_a0,
                                  rwkv_a_lora1, rwkv_a_lora2, rwkv_g_lora1, rwkv_g_lora2)
    rp, y0, bon, pm, qm = _rwkv_chunk(r, k, v, lw, a, rwkv_k_k, rwkv_k_a, rwkv_r_k.reshape(-1))
    y_b = _rwkv_state(rp, y0, pm, qm)
    x1, x1b = _mix(y_a, y_b, bon, g, gates, x, rwkv_lnx_w, rwkv_lnx_b, bf(w_up_a), bf(w_up_b), bf(w_o),
                   ln1_w, ln1_b)
    e0, e1, gate = _route(x1b, bf(peer_w_q), bf(peer_sub_keys))
    act = _peer_score(x1b, peer_u, e0, e1, gate)
    ffn = _peer_value(_peer_gate(e0, e1, act), bf(peer_v))
    return _final(x1, x1b, ffn, p, bf(ple_w_gate), bf(ple_w_proj), ln2_w, ln2_b)


def kernel(x, p, w_in, rwkv_mu, rwkv_w0, rwkv_w_lora1, rwkv_w_lora2, rwkv_a0, rwkv_a_lora1, rwkv_a_lora2, rwkv_g_lora1, rwkv_g_lora2, rwkv_k_k, rwkv_k_a, rwkv_r_k, rwkv_lnx_w, rwkv_lnx_b, w_up_a, w_up_b, w_o, ln1_w, ln1_b, peer_w_q, peer_sub_keys, peer_u, peer_v, ple_w_gate, ple_w_proj, ln2_w, ln2_b):
    bsz, t, d = x.shape
    depth = w_in.shape[0]
    xs = x.reshape(bsz * t, d)
    assert bsz == 1, "token shift / attention / scan treat the flattened rows as one sequence"
    for i in range(depth):
        xs = _layer(xs, p[i].reshape(bsz * t, -1), w_in[i], rwkv_mu[i], rwkv_w0[i], rwkv_w_lora1[i],
                    rwkv_w_lora2[i], rwkv_a0[i], rwkv_a_lora1[i], rwkv_a_lora2[i], rwkv_g_lora1[i],
                    rwkv_g_lora2[i], rwkv_k_k[i], rwkv_k_a[i], rwkv_r_k[i], rwkv_lnx_w[i], rwkv_lnx_b[i],
                    w_up_a[i], w_up_b[i], w_o[i], ln1_w[i], ln1_b[i], peer_w_q[i], peer_sub_keys[i],
                    peer_u[i], peer_v[i], ple_w_gate[i], ple_w_proj[i], ln2_w[i], ln2_b[i])
    return xs.reshape(bsz, t, d)
```

```python
import functools

import jax
import jax.numpy as jnp
from jax import lax
from jax.experimental import pallas as pl
from jax.experimental.pallas import tpu as pltpu

F32 = jnp.float32
BF16 = jnp.bfloat16
I32 = jnp.int32

D_MODEL = 2048
HEAD_DIM = 64
N_HEADS = 16
MIX_WIDTH = N_HEADS * HEAD_DIM
PEER_HEADS = 8
PEER_TOPK = 16
N_KEYS = 128
N_EXPERTS = N_KEYS * N_KEYS
N_SEL = PEER_HEADS * PEER_TOPK
GN_EPS = 64e-5
LN_EPS = 1e-5
DEEPNORM_ALPHA = 2.0 ** 0.25

LANES = 128
VMEM_LIMIT = 56 * 1024 * 1024

MM_TM, MM_TN = 1024, 1024
SB_BLK = 128
SB_SUBS = 4
SB_WIN = 2 * LANES
SB_DONE = -88.0
RW_TM = 256
RW_C = 64
RW_NCH = 8
RW_NCS = 8
MIX_TM = 256
ROUTE_TM = 256
PH_TM, PH_TN = 1024, 2048
PH_SUB = 256
PG_TT = 128
PG_GRP = 16
PV_TM, PV_NE0 = 512, 16
FIN_TM = 256

assert 2 * RW_C == LANES and RW_C == HEAD_DIM

NN = (((1,), (0,)), ((), ()))
NT = (((1,), (1,)), ((), ()))
TN = (((0,), (0,)), ((), ()))


def _cparams(*sem):
    return pltpu.CompilerParams(dimension_semantics=tuple(sem), vmem_limit_bytes=VMEM_LIMIT)


def _dg(a, b, dims=NN):
    return lax.dot_general(a, b, dims, preferred_element_type=F32)


def _split2(x):
    hi = x.astype(BF16)
    lo = (x - hi.astype(F32)).astype(BF16)
    return hi, lo


def _mm1(a, b, dims=NN):
    return _dg(a.astype(BF16), b.astype(BF16), dims)


def _mm_sum_lhs(m_bf16, x):
    hi, lo = _split2(x)
    return _dg(m_bf16, hi) + _dg(m_bf16, lo)


def _softplus(z):
    return jnp.maximum(z, 0.0) + jnp.log(1.0 + jnp.exp(-jnp.abs(z)))


def _sigmoid(z):
    return 1.0 / (1.0 + jnp.exp(-z))


def _layer_norm(x, g, b):
    mu = jnp.mean(x, axis=-1, keepdims=True)
    d = x - mu
    var = jnp.mean(d * d, axis=-1, keepdims=True)
    return d * lax.rsqrt(var + LN_EPS) * g + b


def _proj_kernel(a_ref, b_ref, o_ref, bw_ref):
    @pl.when(pl.program_id(1) == 0)
    def _():
        bw_ref[...] = b_ref[...].astype(BF16)

    o_ref[...] = _dg(a_ref[...].astype(BF16), bw_ref[...]).astype(o_ref.dtype)


def _proj(a, b, col0, n, out_dtype):
    m, k = a.shape
    tm, tn = min(MM_TM, m), MM_TN
    assert col0 % tn == 0 and n % tn == 0
    off = col0 // tn
    return pl.pallas_call(
        _proj_kernel,
        grid=(n // tn, m // tm),
        in_specs=[pl.BlockSpec((tm, k), lambda j, i: (i, 0)),
                  pl.BlockSpec((k, tn), lambda j, i: (0, j + off))],
        out_specs=pl.BlockSpec((tm, tn), lambda j, i: (i, j)),
        out_shape=jax.ShapeDtypeStruct((m, n), out_dtype),
        scratch_shapes=[pltpu.VMEM((k, tn), BF16)],
        compiler_params=_cparams("parallel", "arbitrary"),
    )(a, b)


def _sb_kernel(q_ref, k_ref, v_ref, o_ref, acc_ref, carry_ref):
    i = pl.program_id(1)
    blk = SB_BLK
    n_sub = q_ref.shape[0] // blk
    win = SB_WIN
    lane = lax.broadcasted_iota(I32, (1, LANES), 1)
    row = lax.broadcasted_iota(I32, (blk, win), 0)
    col = lax.broadcasted_iota(I32, (blk, win), 1)
    r_io = lax.broadcasted_iota(I32, (win, win), 0)
    c_io = lax.broadcasted_iota(I32, (win, win), 1)
    later_mat = jnp.where(r_io > c_io, 1.0, 0.0).astype(BF16)
    scale = jnp.asarray(HEAD_DIM ** -0.5, BF16)
    chains = [(s, h) for s in range(n_sub) for h in range(2)]
    qh = []
    for s, h in chains:
        q = q_ref[s * blk:(s + 1) * blk, :]
        qh.append(jnp.where((lane < HEAD_DIM) == (h == 0), q, jnp.zeros_like(q)) * scale)
    first_end = [(i * n_sub + s + 1) * blk for s in range(n_sub)]
    acc_ref[...] = jnp.zeros_like(acc_ref)
    carry_ref[...] = jnp.zeros_like(carry_ref)

    def step(it):
        kwin, vwin, valid = [], [], []
        for s in range(n_sub):
            end = first_end[s] - it * win
            start = pl.multiple_of(jnp.maximum(end - win, 0), blk)
            kwin.append(k_ref[pl.ds(start, win), :])
            vwin.append(v_ref[pl.ds(start, win), :])
            valid.append((start + col) < jnp.minimum(first_end[s] - blk + row, end))
        z = [_dg(qh[c], kwin[s], NT) for c, (s, h) in enumerate(chains)]
        sp = [_softplus(x) for x in z]
        log_keep = [jnp.where(valid[s], -sp[c], 0.0).astype(BF16) for c, (s, h) in enumerate(chains)]
        carry = [carry_ref[c] for c in range(len(chains))]
        log_later = [_dg(log_keep[c], later_mat) + jnp.concatenate([carry[c]] * (win // LANES), axis=1)
                     for c in range(len(chains))]
        w = [jnp.where(valid[s], jnp.exp(z[c] - sp[c] + log_later[c]), 0.0).astype(BF16)
             for c, (s, h) in enumerate(chains)]
        pv = [_dg(w[c], vwin[s]) for c, (s, h) in enumerate(chains)]
        for c in range(len(chains)):
            acc_ref[c] += pv[c]
            carry_ref[c] = jnp.broadcast_to(log_later[c][:, 0:1] + log_keep[c][:, 0:1], (blk, LANES))
        return it + 1

    def cond(it):
        alive = None
        for s in range(n_sub):
            more = jnp.logical_and(first_end[s] - it * win > 0, jnp.max(carry_ref[2 * s:2 * s + 2]) > SB_DONE)
            alive = more if alive is None else jnp.logical_or(alive, more)
        return alive

    lax.while_loop(cond, step, step(0))
    for s in range(n_sub):
        o_ref[s * blk:(s + 1) * blk, :] = jnp.where(lane < HEAD_DIM, acc_ref[2 * s], acc_ref[2 * s + 1]).astype(o_ref.dtype)


def _sb_attention(qkv):
    t = qkv.shape[0]
    blk = SB_BLK * SB_SUBS
    n_pairs = MIX_WIDTH // LANES
    return pl.pallas_call(
        _sb_kernel,
        grid=(n_pairs, t // blk),
        in_specs=[pl.BlockSpec((blk, LANES), lambda hp, i: (i, hp)),
                  pl.BlockSpec((t, LANES), lambda hp, i: (0, n_pairs + hp)),
                  pl.BlockSpec((t, LANES), lambda hp, i: (0, 2 * n_pairs + hp))],
        out_specs=pl.BlockSpec((blk, LANES), lambda hp, i: (i, hp)),
        out_shape=jax.ShapeDtypeStruct((t, MIX_WIDTH), BF16),
        scratch_shapes=[pltpu.VMEM((2 * SB_SUBS, SB_BLK, LANES), F32), pltpu.VMEM((2 * SB_SUBS, SB_BLK, LANES), F32)],
        compiler_params=_cparams("parallel", "parallel"),
    )(qkv, qkv, qkv)


def _rwkv_pre_kernel(cur_ref, prev_ref, mu_ref, w0_ref, wl1_ref, wl2_ref, a0_ref, al1_ref, al2_ref,
                     gl1_ref, gl2_ref, r_ref, k_ref, v_ref, lw_ref, a_ref, g_ref):
    i = pl.program_id(0)
    tm = cur_ref.shape[0]
    wd = MIX_WIDTH
    first_row = lax.broadcasted_iota(I32, (tm, wd), 0) == 0
    keep_prev = jnp.where(i == 0, 0.0, 1.0)

    n_prev = prev_ref.shape[0]

    def shifted(col):
        z = cur_ref[:, col * wd:(col + 1) * wd].astype(F32)
        last = prev_ref[n_prev - 1:n_prev, col * wd:(col + 1) * wd].astype(F32) * keep_prev
        prev = jnp.where(first_row, last, pltpu.roll(z, 1, 0))
        return z, prev - z

    z, d = shifted(0)
    r_ref[...] = (z + d * mu_ref[0:1, :]).astype(r_ref.dtype)
    z, d = shifted(1)
    k_ref[...] = (z + d * mu_ref[1:2, :]).astype(k_ref.dtype)
    z, d = shifted(2)
    v_ref[...] = (z + d * mu_ref[2:3, :]).astype(v_ref.dtype)
    z, d = shifted(3)
    xw = z + d * mu_ref[3:4, :]
    xa = z + d * mu_ref[4:5, :]
    xg = z + d * mu_ref[5:6, :]
    w = w0_ref[...] + _mm1(jnp.tanh(_mm1(xw, wl1_ref[...])), wl2_ref[...])
    w = -_softplus(-w) - 0.5
    lw_ref[...] = -jnp.exp(w)
    a_ref[...] = _sigmoid(a0_ref[...] + _mm1(_mm1(xa, al1_ref[...]), al2_ref[...])).astype(a_ref.dtype)
    g_ref[...] = _mm1(_sigmoid(_mm1(xg, gl1_ref[...])), gl2_ref[...]).astype(g_ref.dtype)


def _pad_to(x, axis, size):
    pad = [(0, 0)] * x.ndim
    pad[axis] = (0, size - x.shape[axis])
    return jnp.pad(x, pad)


def _rwkv_pre(rw, mu, w0, wl1, wl2, a0, al1, al2, gl1, gl2):
    t = rw.shape[0]
    tm = min(RW_TM, t)
    wd = MIX_WIDTH
    lo = LANES

    def lora_pair(l1, l2):
        n = -(-l1.shape[1] // lo) * lo
        return _pad_to(l1, 1, n).astype(BF16), _pad_to(l2, 0, n).astype(BF16)

    wl1, wl2 = lora_pair(wl1, wl2)
    al1, al2 = lora_pair(al1, al2)
    gl1, gl2 = lora_pair(gl1, gl2)
    full = lambda x: pl.BlockSpec(x.shape, lambda i: (0,) * x.ndim)
    row = lambda: pl.BlockSpec((tm, wd), lambda i: (i, 0))
    consts = (mu, w0.reshape(1, wd), wl1, wl2, a0.reshape(1, wd), al1, al2, gl1, gl2)
    n_prev = 32 // rw.dtype.itemsize
    out_dtypes = (BF16, BF16, BF16, F32, BF16, BF16)
    return pl.pallas_call(
        _rwkv_pre_kernel,
        grid=(t // tm,),
        in_specs=[pl.BlockSpec((tm, 4 * wd), lambda i: (i, 0)),
                  pl.BlockSpec((n_prev, 4 * wd), lambda i: (jnp.maximum(i * (tm // n_prev) - 1, 0), 0))]
                 + [full(c) for c in consts],
        out_specs=[row() for _ in range(6)],
        out_shape=[jax.ShapeDtypeStruct((t, wd), dt) for dt in out_dtypes],
        compiler_params=_cparams("parallel"),
    )(rw, rw, *consts)


def _rwkv_chunk_kernel(r_ref, k_ref, v_ref, lw_ref, a_ref, kkp_ref, kap_ref, rkp_ref,
                       rp_ref, y0_ref, bon_ref, p_ref, q_ref):
    c_len = RW_C
    n2 = 2 * c_len
    n_chunks = r_ref.shape[0] // c_len
    r2 = lax.broadcasted_iota(I32, (n2, LANES), 0)
    c2 = lax.broadcasted_iota(I32, (n2, LANES), 1)
    same = (r2 < c_len) == (c2 < HEAD_DIM)
    t_r = jnp.where(r2 < c_len, r2, r2 - c_len)
    t_c = jnp.where(c2 < HEAD_DIM, c2, c2 - HEAD_DIM)
    strict = jnp.logical_and(same, t_r > t_c)
    incl = jnp.logical_and(same, t_r >= t_c)
    eye = jnp.where(r2 == c2, 1.0, 0.0).astype(F32)
    group_ones = jnp.where(same, 1.0, 0.0).astype(BF16)
    lr = lax.broadcasted_iota(I32, (c_len, c_len), 0)
    lc = lax.broadcasted_iota(I32, (c_len, c_len), 1)
    cum_mat = jnp.where(lr >= lc, 1.0, 0.0).astype(BF16)
    kkp, kap, rkp = kkp_ref[...], kap_ref[...], rkp_ref[...]

    def stack(z):
        return jnp.where(same, jnp.concatenate([z, z], axis=0), 0.0)

    def unstack(zs):
        return zs[:c_len] + zs[c_len:]

    cs = range(n_chunks)
    rows = [pl.ds(c * c_len, c_len) for c in cs]
    r = [r_ref[rw, :].astype(F32) for rw in rows]
    kr = [k_ref[rw, :].astype(F32) for rw in rows]
    v = [v_ref[rw, :].astype(F32) for rw in rows]
    lw = [lw_ref[rw, :] for rw in rows]
    a = [a_ref[rw, :].astype(F32) for rw in rows]
    kk = [x * kkp for x in kr]
    cum = [_mm_sum_lhs(cum_mat, x) for x in lw]
    ssq = [_dg((x * x).astype(BF16), group_ones) for x in kk]
    km = [kr[c] * (1.0 + (a[c] - 1.0) * kap) for c in cs]
    bsum = [_dg((r[c] * km[c] * rkp).astype(BF16), group_ones) for c in cs]
    kk = [kk[c] / jnp.maximum(jnp.sqrt(ssq[c]), 1e-12) for c in cs]
    bv = [kk[c] * a[c] for c in cs]
    last = [x[c_len - 1:c_len, :] for x in cum]
    rt = [stack(r[c] * jnp.exp(cum[c])) for c in cs]
    at_b = [stack(-kk[c] * jnp.exp(cum[c] - lw[c])).astype(BF16) for c in cs]
    rt_b = [x.astype(BF16) for x in rt]
    g_inv = [jnp.exp(-x) for x in cum]
    btkt = [jnp.concatenate([stack(bv[c] * g_inv[c]), stack(km[c] * g_inv[c])], axis=0).astype(BF16) for c in cs]
    m_a = [_dg(at_b[c], btkt[c], NT) for c in cs]
    m_r = [_dg(rt_b[c], btkt[c], NT) for c in cs]
    vs_b = [stack(x).astype(BF16) for x in v]
    g_end = [jnp.exp(last[c] - cum[c]) for c in cs]
    bd_b = [stack(bv[c] * g_end[c]).astype(BF16) for c in cs]
    kd_b = [stack(km[c] * g_end[c]).astype(BF16) for c in cs]
    m_ab = [jnp.where(strict, x[:, :n2], 0.0) for x in m_a]
    m_ak = [jnp.where(strict, x[:, n2:], 0.0).astype(BF16) for x in m_a]
    m_rb = [jnp.where(incl, x[:, :n2], 0.0).astype(BF16) for x in m_r]
    m_rk = [jnp.where(incl, x[:, n2:], 0.0).astype(BF16) for x in m_r]
    akv = [_dg(m_ak[c], vs_b[c]) for c in cs]
    rkv = [_dg(m_rk[c], vs_b[c]) for c in cs]
    kdv = [_dg(kd_b[c], vs_b[c], TN) for c in cs]
    tinv = [eye + x for x in m_ab]
    npow = [x.astype(BF16) for x in m_ab]
    for _ in range(5):
        npow = [_dg(x, x).astype(BF16) for x in npow]
        tinv = [tinv[c] + _dg(tinv[c].astype(BF16), npow[c]) for c in cs]
    au_b = [_dg(tinv[c].astype(BF16), jnp.concatenate([at_b[c], akv[c].astype(BF16)], axis=1)).astype(BF16)
            for c in cs]
    ry = [_dg(m_rb[c], au_b[c]) for c in cs]
    pq = [_dg(bd_b[c], au_b[c], TN) for c in cs]
    for c in cs:
        prow = pl.ds(c * n2, n2)
        rp_ref[rows[c], :] = unstack(rt[c] + ry[c][:, :LANES]).astype(rp_ref.dtype)
        y0_ref[rows[c], :] = unstack(ry[c][:, LANES:] + rkv[c])
        p_ref[prow, :] = (eye * jnp.exp(last[c]) + pq[c][:, :LANES]).astype(p_ref.dtype)
        q_ref[prow, :] = pq[c][:, LANES:] + kdv[c]
        bon_ref[rows[c], :] = (bsum[c] * v[c]).astype(bon_ref.dtype)


def _rwkv_chunk(r, k, v, lw, a, k_k, k_a, r_k):
    t = r.shape[0]
    rb = min(RW_C * RW_NCH, t)
    n_pairs = MIX_WIDTH // LANES
    row = lambda: pl.BlockSpec((rb, LANES), lambda i, hp: (i, hp))
    par = lambda: pl.BlockSpec((1, LANES), lambda i, hp: (0, hp))
    mat = lambda: pl.BlockSpec((2 * rb, LANES), lambda i, hp: (i, hp))
    return pl.pallas_call(
        _rwkv_chunk_kernel,
        grid=(t // rb, n_pairs),
        in_specs=[row() for _ in range(5)] + [par() for _ in range(3)],
        out_specs=[row(), row(), row(), mat(), mat()],
        out_shape=[jax.ShapeDtypeStruct((t, MIX_WIDTH), BF16), jax.ShapeDtypeStruct((t, MIX_WIDTH), F32),
                   jax.ShapeDtypeStruct((t, MIX_WIDTH), BF16), jax.ShapeDtypeStruct((2 * t, MIX_WIDTH), BF16),
                   jax.ShapeDtypeStruct((2 * t, MIX_WIDTH), F32)],
        compiler_params=_cparams("parallel", "parallel"),
    )(r, k, v, lw, a, k_k.reshape(1, -1), k_a.reshape(1, -1), r_k.reshape(1, -1))


def _rwkv_state_kernel(rp_ref, y0_ref, p_ref, q_ref, o_ref, st_ref):
    @pl.when(pl.program_id(0) == 0)
    def _():
        st_ref[...] = jnp.zeros_like(st_ref)

    c_len = RW_C
    pairs = range(MIX_WIDTH // LANES)
    lanes = [slice(hp * LANES, (hp + 1) * LANES) for hp in pairs]
    st = [st_ref[hp] for hp in pairs]
    for c in range(rp_ref.shape[0] // c_len):
        rows = slice(c * c_len, (c + 1) * c_len)
        mrows = slice(2 * c * c_len, 2 * (c + 1) * c_len)
        st_b = [x.astype(BF16) for x in st]
        st = [_dg(p_ref[mrows, lanes[hp]], st_b[hp]) + q_ref[mrows, lanes[hp]] for hp in pairs]
        for hp in pairs:
            o_ref[rows, lanes[hp]] = _dg(rp_ref[rows, lanes[hp]], st_b[hp]) + y0_ref[rows, lanes[hp]]
    for hp in pairs:
        st_ref[hp] = st[hp]


def _rwkv_state(rp, y0, p, q):
    t = rp.shape[0]
    rb = min(RW_C * RW_NCS, t)
    wd = MIX_WIDTH
    row = lambda: pl.BlockSpec((rb, wd), lambda i: (i, 0))
    mat = lambda: pl.BlockSpec((2 * rb, wd), lambda i: (i, 0))
    return pl.pallas_call(
        _rwkv_state_kernel,
        grid=(t // rb,),
        in_specs=[row(), row(), mat(), mat()],
        out_specs=row(),
        out_shape=jax.ShapeDtypeStruct((t, wd), F32),
        scratch_shapes=[pltpu.VMEM((wd // LANES, LANES, LANES), F32)],
        compiler_params=_cparams("arbitrary"),
    )(rp, y0, p, q)


def _mix_kernel(ya_ref, y_ref, bon_ref, g_ref, gate_ref, x_ref, gnw_ref, gnb_ref, wa_ref, wb_ref, wo_ref,
                lnw_ref, lnb_ref, x1_ref, x1b_ref):
    r2 = lax.broadcasted_iota(I32, (LANES, LANES), 0)
    c2 = lax.broadcasted_iota(I32, (LANES, LANES), 1)
    group_ones = jnp.where((r2 < HEAD_DIM) == (c2 < HEAD_DIM), 1.0, 0.0).astype(BF16)
    inv_n = 1.0 / HEAD_DIM
    pieces = []
    for hp in range(MIX_WIDTH // LANES):
        ls = slice(hp * LANES, (hp + 1) * LANES)
        y = y_ref[:, ls]
        d = y - _dg(y.astype(BF16), group_ones) * inv_n
        var = _dg((d * d).astype(BF16), group_ones) * inv_n
        yn = d * lax.rsqrt(var + GN_EPS) * gnw_ref[:, ls] + gnb_ref[:, ls]
        pieces.append(((yn + bon_ref[:, ls]) * g_ref[:, ls]).astype(BF16))
    ya = _dg(ya_ref[...], wa_ref[...])
    yb = _dg(jnp.concatenate(pieces, axis=1), wb_ref[...])
    ga = _sigmoid(gate_ref[:, :D_MODEL].astype(F32))
    gb = _sigmoid(gate_ref[:, D_MODEL:].astype(F32))
    mixed = _dg((ga * ya + gb * yb).astype(BF16), wo_ref[...])
    x1 = _layer_norm(DEEPNORM_ALPHA * x_ref[...] + mixed, lnw_ref[...], lnb_ref[...])
    x1_ref[...] = x1
    x1b_ref[...] = x1.astype(BF16)


def _const_spec(x):
    return pl.BlockSpec(x.shape, lambda *_: (0,) * x.ndim, pipeline_mode=pl.Buffered(1))


def _mix(ya, y, bon, g, gates, x, gn_w, gn_b, w_up_a, w_up_b, w_o, ln_w, ln_b):
    t = x.shape[0]
    tm = min(MIX_TM, t)
    consts = (gn_w.reshape(1, -1), gn_b.reshape(1, -1), w_up_a, w_up_b, w_o, ln_w.reshape(1, -1), ln_b.reshape(1, -1))
    row = lambda w: pl.BlockSpec((tm, w), lambda i: (i, 0))
    return pl.pallas_call(
        _mix_kernel,
        grid=(t // tm,),
        in_specs=[row(MIX_WIDTH), row(MIX_WIDTH), row(MIX_WIDTH), row(MIX_WIDTH), row(2 * D_MODEL), row(D_MODEL)]
                 + [_const_spec(c) for c in consts],
        out_specs=[pl.BlockSpec((tm, D_MODEL), lambda i: (i, 0))] * 2,
        out_shape=[jax.ShapeDtypeStruct((t, D_MODEL), F32), jax.ShapeDtypeStruct((t, D_MODEL), BF16)],
        compiler_params=_cparams("parallel"),
    )(ya, y, bon, g, gates, x, *consts)


def _topk_rows(s, ids, k):
    big = jnp.asarray(1e9, F32)
    vals, idxs = [], []
    for _ in range(k):
        m = jnp.max(s, axis=0, keepdims=True)
        ix = jnp.min(jnp.where(s == m, ids, big), axis=0, keepdims=True)
        vals.append(m)
        idxs.append(ix)
        s = jnp.where(ids == ix, -jnp.inf, s)
    return jnp.concatenate(vals, axis=0), jnp.concatenate(idxs, axis=0)


def _route_kernel(x_ref, wq_ref, keys_ref, e0_ref, e1_ref, gate_ref):
    tm = x_ref.shape[0]
    topk = PEER_TOPK
    q = _dg(x_ref[...], wq_ref[...])
    qb = q.astype(BF16)
    half = N_KEYS
    key_ids = lax.broadcasted_iota(I32, (N_KEYS, tm), 0).astype(F32)
    sub8 = lax.broadcasted_iota(I32, (8, tm), 0).astype(F32)
    sub16 = lax.broadcasted_iota(I32, (topk, tm), 0).astype(F32)
    cand_ids = jnp.concatenate([sub16] + [a * topk + sub8 for a in range(1, 8)] + [(sub8 + 8.0) * topk], axis=0)
    e0s, e1s, gates = [], [], []
    for h in range(PEER_HEADS):
        tops = []
        for c in range(2):
            qs = qb[:, (2 * h + c) * half:(2 * h + c + 1) * half]
            s = _dg(keys_ref[c], qs, NT)
            tops.append(_topk_rows(s, key_ids, topk))
        (s0, i0), (s1, i1) = tops
        cand = jnp.concatenate([s0[0:1, :] + s1] + [s0[a:a + 1, :] + s1[:8, :] for a in range(1, 8)]
                               + [s0[8:, :] + s1[0:1, :]], axis=0)
        best, pos = _topk_rows(cand, cand_ids, topk)
        a_sel = jnp.floor(pos * (1.0 / topk))
        b_sel = pos - a_sel * topk
        e0 = jnp.zeros_like(pos)
        e1 = jnp.zeros_like(pos)
        for j in range(topk):
            e0 = jnp.where(a_sel == j, i0[j:j + 1, :], e0)
            e1 = jnp.where(b_sel == j, i1[j:j + 1, :], e1)
        ex = jnp.exp(best - best[0:1, :])
        gates.append(ex / jnp.sum(ex, axis=0, keepdims=True))
        e0s.append(e0)
        e1s.append(e1)
    e0_ref[...] = jnp.concatenate(e0s, axis=0).T.astype(I32)
    e1_ref[...] = jnp.concatenate(e1s, axis=0).T.astype(I32)
    gate_ref[...] = jnp.concatenate(gates, axis=0).T


def _route(x1b, w_q, sub_keys):
    t = x1b.shape[0]
    tm = min(ROUTE_TM, t)
    out = lambda: pl.BlockSpec((tm, N_SEL), lambda i: (i, 0))
    return pl.pallas_call(
        _route_kernel,
        grid=(t // tm,),
        in_specs=[pl.BlockSpec((tm, D_MODEL), lambda i: (i, 0)), _const_spec(w_q), _const_spec(sub_keys)],
        out_specs=[out(), out(), out()],
        out_shape=[jax.ShapeDtypeStruct((t, N_SEL), I32), jax.ShapeDtypeStruct((t, N_SEL), I32),
                   jax.ShapeDtypeStruct((t, N_SEL), F32)],
        compiler_params=_cparams("parallel"),
    )(x1b, w_q, sub_keys)


def _gelu_exact(x):
    return 0.5 * x * (1.0 + lax.erf(x * (2.0 ** -0.5)))


def _peer_score_kernel(x_ref, u_ref, e0_ref, e1_ref, gate_ref, o_ref, acc_ref):
    j = pl.program_id(1)

    @pl.when(j == 0)
    def _():
        acc_ref[...] = jnp.zeros_like(acc_ref)

    x = x_ref[...]
    e0, e1 = e0_ref[...], e1_ref[...]
    blocks = u_ref.shape[0] // N_KEYS
    per_dot = PH_SUB // N_KEYS
    acc = acc_ref[...]
    for s in range(u_ref.shape[0] // PH_SUB):
        h = _dg(x, u_ref[s * PH_SUB:(s + 1) * PH_SUB, :].astype(BF16), NT)
        for b in range(per_dot):
            picked = jnp.take_along_axis(h[:, b * N_KEYS:(b + 1) * N_KEYS], e1, axis=1)
            acc = acc + jnp.where(e0 == j * blocks + s * per_dot + b, picked, 0.0)
    acc_ref[...] = acc

    @pl.when(j == pl.num_programs(1) - 1)
    def _():
        o_ref[...] = _gelu_exact(acc_ref[...]) * gate_ref[...]


def _peer_score(x1b, u_bf16, e0, e1, gate):
    t = x1b.shape[0]
    tm = min(PH_TM, t)
    tok = lambda: pl.BlockSpec((tm, N_SEL), lambda i, j: (i, 0))
    return pl.pallas_call(
        _peer_score_kernel,
        grid=(t // tm, N_EXPERTS // PH_TN),
        in_specs=[pl.BlockSpec((tm, D_MODEL), lambda i, j: (i, 0)),
                  pl.BlockSpec((PH_TN, D_MODEL), lambda i, j: (j, 0)), tok(), tok(), tok()],
        out_specs=tok(),
        out_shape=jax.ShapeDtypeStruct((t, N_SEL), F32),
        scratch_shapes=[pltpu.VMEM((tm, N_SEL), F32)],
        compiler_params=_cparams("parallel", "arbitrary"),
    )(x1b, u_bf16, e0, e1, gate)


def _peer_gate_kernel(e0_ref, e1_ref, act_ref, o_ref):
    sub = lax.broadcasted_iota(I32, (N_KEYS, N_SEL), 0)

    def body(grp, carry):
        mats = []
        for u in range(PG_GRP):
            row = pl.ds(grp * PG_GRP + u, 1)
            left = jnp.where(e0_ref[row, :] == sub, act_ref[row, :], 0.0).astype(BF16)
            right = jnp.where(e1_ref[row, :] == sub, 1.0, 0.0).astype(BF16)
            mats.append(_dg(left, right, NT))
        o_ref[grp] = jnp.swapaxes(jnp.stack(mats, axis=0), 0, 1).astype(o_ref.dtype)
        return carry

    lax.fori_loop(0, o_ref.shape[0], body, 0)


def _peer_gate(e0, e1, act):
    t = e0.shape[0]
    tt = min(PG_TT, t)
    tok = lambda: pl.BlockSpec((tt, N_SEL), lambda i: (i, 0))
    return pl.pallas_call(
        _peer_gate_kernel,
        grid=(t // tt,),
        in_specs=[tok(), tok(), tok()],
        out_specs=pl.BlockSpec((tt // PG_GRP, N_KEYS, PG_GRP, N_KEYS), lambda i: (i, 0, 0, 0)),
        out_shape=jax.ShapeDtypeStruct((t // PG_GRP, N_KEYS, PG_GRP, N_KEYS), BF16),
        compiler_params=_cparams("parallel"),
    )(e0, e1, act)


def _peer_value_kernel(g_ref, v_ref, o_ref):
    @pl.when(pl.program_id(1) == 0)
    def _():
        o_ref[...] = jnp.zeros_like(o_ref)

    tm = o_ref.shape[0]
    acc = None
    for e in range(0, g_ref.shape[1], 2):
        lhs = jnp.concatenate([g_ref[:, e, :, :].reshape(tm, N_KEYS), g_ref[:, e + 1, :, :].reshape(tm, N_KEYS)], axis=1)
        part = _dg(lhs, v_ref[e * N_KEYS:(e + 2) * N_KEYS, :])
        acc = part if acc is None else acc + part
    o_ref[...] += acc


def _peer_value(g4, v_bf16):
    t = g4.shape[0] * PG_GRP
    tm = min(PV_TM, t)
    return pl.pallas_call(
        _peer_value_kernel,
        grid=(t // tm, N_KEYS // PV_NE0),
        in_specs=[pl.BlockSpec((tm // PG_GRP, PV_NE0, PG_GRP, N_KEYS), lambda i, k: (i, k, 0, 0)),
                  pl.BlockSpec((PV_NE0 * N_KEYS, D_MODEL), lambda i, k: (k, 0))],
        out_specs=pl.BlockSpec((tm, D_MODEL), lambda i, k: (i, 0)),
        out_shape=jax.ShapeDtypeStruct((t, D_MODEL), F32),
        compiler_params=_cparams("parallel", "arbitrary"),
    )(g4, v_bf16)


def _final_kernel(x1_ref, x1b_ref, ffn_ref, p_ref, wg_ref, wp_ref, lnw_ref, lnb_ref, o_ref):
    ple = _sigmoid(_dg(x1b_ref[...], wg_ref[...])) * _dg(p_ref[...].astype(BF16), wp_ref[...])
    o_ref[...] = _layer_norm(DEEPNORM_ALPHA * x1_ref[...] + ffn_ref[...] + ple, lnw_ref[...], lnb_ref[...])


def _final(x1, x1b, ffn, p, w_gate, w_proj, ln_w, ln_b):
    t = x1.shape[0]
    tm = min(FIN_TM, t)
    consts = (w_gate, w_proj, ln_w.reshape(1, -1), ln_b.reshape(1, -1))
    row = lambda w: pl.BlockSpec((tm, w), lambda i: (i, 0))
    return pl.pallas_call(
        _final_kernel,
        grid=(t // tm,),
        in_specs=[row(D_MODEL), row(D_MODEL), row(D_MODEL), row(p.shape[1])] + [_const_spec(c) for c in consts],
        out_specs=row(D_MODEL),
        out_shape=jax.ShapeDtypeStruct((t, D_MODEL), F32),
        compiler_params=_cparams("parallel"),
    )(x1, x1b, ffn, p, *consts)


def _layer(x, p, w_in, rwkv_mu, rwkv_w0, rwkv_w_lora1, rwkv_w_lora2, rwkv_a0, rwkv_a_lora1,
           rwkv_a_lora2, rwkv_g_lora1, rwkv_g_lora2, rwkv_k_k, rwkv_k_a, rwkv_r_k, rwkv_lnx_w,
           rwkv_lnx_b, w_up_a, w_up_b, w_o, ln1_w, ln1_b, peer_w_q, peer_sub_keys, peer_u, peer_v,
           ple_w_gate, ple_w_proj, ln2_w, ln2_b):
    bf = lambda w: w.astype(BF16)
    n_sb = 3 * MIX_WIDTH
    n_rw = 4 * MIX_WIDTH
    qkv = _proj(x, w_in, 0, n_sb, BF16)
    rw = _proj(x, w_in, n_sb, n_rw, BF16)
    gates = _proj(x, w_in, n_sb + n_rw, 2 * D_MODEL, BF16)
    y_a = _sb_attention(qkv)
    r, k, v, lw, a, g = _rwkv_pre(rw, rwkv_mu, rwkv_w0, rwkv_w_lora1, rwkv_w_lora2, rwkv_a0,
                                  rwkv_a_lora1, rwkv_a_lora2, rwkv_g_lora1, rwkv_g_lora2)
    rp, y0, bon, pm, qm = _rwkv_chunk(r, k, v, lw, a, rwkv_k_k, rwkv_k_a, rwkv_r_k.reshape(-1))
    y_b = _rwkv_state(rp, y0, pm, qm)
    x1, x1b = _mix(y_a, y_b, bon, g, gates, x, rwkv_lnx_w, rwkv_lnx_b, bf(w_up_a), bf(w_up_b), bf(w_o),
                   ln1_w, ln1_b)
    e0, e1, gate = _route(x1b, bf(peer_w_q), bf(peer_sub_keys))
    act = _peer_score(x1b, peer_u, e0, e1, gate)
    ffn = _peer_value(_peer_gate(e0, e1, act), bf(peer_v))
    return _final(x1, x1b, ffn, p, bf(ple_w_gate), bf(ple_w_proj), ln2_w, ln2_b)


def kernel(x, p, w_in, rwkv_mu, rwkv_w0, rwkv_w_lora1, rwkv_w_lora2, rwkv_a0, rwkv_a_lora1, rwkv_a_lora2, rwkv_g_lora1, rwkv_g_lora2, rwkv_k_k, rwkv_k_a, rwkv_r_k, rwkv_lnx_w, rwkv_lnx_b, w_up_a, w_up_b, w_o, ln1_w, ln1_b, peer_w_q, peer_sub_keys, peer_u, peer_v, ple_w_gate, ple_w_proj, ln2_w, ln2_b):
    bsz, t, d = x.shape
    depth = w_in.shape[0]
    xs = x.reshape(bsz * t, d)
    assert bsz == 1, "token shift / attention / scan treat the flattened rows as one sequence"
    for i in range(depth):
        xs = _layer(xs, p[i].reshape(bsz * t, -1), w_in[i], rwkv_mu[i], rwkv_w0[i], rwkv_w_lora1[i],
                    rwkv_w_lora2[i], rwkv_a0[i], rwkv_a_lora1[i], rwkv_a_lora2[i], rwkv_g_lora1[i],
                    rwkv_g_lora2[i], rwkv_k_k[i], rwkv_k_a[i], rwkv_r_k[i], rwkv_lnx_w[i], rwkv_lnx_b[i],
                    w_up_a[i], w_up_b[i], w_o[i], ln1_w[i], ln1_b[i], peer_w_q[i], peer_sub_keys[i],
                    peer_u[i], peer_v[i], ple_w_gate[i], ple_w_proj[i], ln2_w[i], ln2_b[i])
    return xs.reshape(bsz, t, d)
```

```python
import functools

import jax
import jax.numpy as jnp
from jax import lax
from jax.experimental import pallas as pl
from jax.experimental.pallas import tpu as pltpu

F32 = jnp.float32
BF16 = jnp.bfloat16
I32 = jnp.int32

D_MODEL = 2048
HEAD_DIM = 64
N_HEADS = 16
MIX_WIDTH = N_HEADS * HEAD_DIM
PEER_HEADS = 8
PEER_TOPK = 16
N_KEYS = 128
N_EXPERTS = N_KEYS * N_KEYS
N_SEL = PEER_HEADS * PEER_TOPK
GN_EPS = 64e-5
LN_EPS = 1e-5
DEEPNORM_ALPHA = 2.0 ** 0.25

LANES = 128
VMEM_LIMIT = 56 * 1024 * 1024

MM_TM, MM_TN = 1024, 1024
SB_BLK = 128
SB_SUBS = 4
SB_WIN = 2 * LANES
SB_DONE = -88.0
RW_TM = 256
RW_C = 64
RW_NCH = 8
RW_NCS = 8
MIX_TM = 256
ROUTE_TM = 256
PH_TM, PH_TN = 1024, 2048
PH_SUB = 256
PG_TT = 128
PG_GRP = 16
PV_TM, PV_NE0 = 512, 16
FIN_TM = 256

assert 2 * RW_C == LANES and RW_C == HEAD_DIM

NN = (((1,), (0,)), ((), ()))
NT = (((1,), (1,)), ((), ()))
TN = (((0,), (0,)), ((), ()))


def _cparams(*sem):
    return pltpu.CompilerParams(dimension_semantics=tuple(sem), vmem_limit_bytes=VMEM_LIMIT)


def _dg(a, b, dims=NN):
    return lax.dot_general(a, b, dims, preferred_element_type=F32)


def _split2(x):
    hi = x.astype(BF16)
    lo = (x - hi.astype(F32)).astype(BF16)
    return hi, lo


def _mm1(a, b, dims=NN):
    return _dg(a.astype(BF16), b.astype(BF16), dims)


def _mm_sum_lhs(m_bf16, x):
    hi, lo = _split2(x)
    return _dg(m_bf16, hi) + _dg(m_bf16, lo)


def _softplus(z):
    return jnp.maximum(z, 0.0) + jnp.log(1.0 + jnp.exp(-jnp.abs(z)))


def _sigmoid(z):
    return 0.5 * jnp.tanh(0.5 * z) + 0.5


def _layer_norm(x, g, b):
    mu = jnp.mean(x, axis=-1, keepdims=True)
    d = x - mu
    var = jnp.mean(d * d, axis=-1, keepdims=True)
    return d * lax.rsqrt(var + LN_EPS) * g + b


def _proj_kernel(a_ref, b_ref, o_ref, bw_ref):
    @pl.when(pl.program_id(1) == 0)
    def _():
        bw_ref[...] = b_ref[...].astype(BF16)

    o_ref[...] = _dg(a_ref[...].astype(BF16), bw_ref[...]).astype(o_ref.dtype)


def _proj(a, b, col0, n, out_dtype):
    m, k = a.shape
    tm, tn = min(MM_TM, m), MM_TN
    assert col0 % tn == 0 and n % tn == 0
    off = col0 // tn
    return pl.pallas_call(
        _proj_kernel,
        grid=(n // tn, m // tm),
        in_specs=[pl.BlockSpec((tm, k), lambda j, i: (i, 0)),
                  pl.BlockSpec((k, tn), lambda j, i: (0, j + off))],
        out_specs=pl.BlockSpec((tm, tn), lambda j, i: (i, j)),
        out_shape=jax.ShapeDtypeStruct((m, n), out_dtype),
        scratch_shapes=[pltpu.VMEM((k, tn), BF16)],
        compiler_params=_cparams("parallel", "arbitrary"),
    )(a, b)


def _sb_kernel(q_ref, k_ref, v_ref, o_ref, acc_ref, carry_ref):
    i = pl.program_id(1)
    blk = SB_BLK
    n_sub = q_ref.shape[0] // blk
    win = SB_WIN
    lane = lax.broadcasted_iota(I32, (1, LANES), 1)
    row = lax.broadcasted_iota(I32, (blk, win), 0)
    col = lax.broadcasted_iota(I32, (blk, win), 1)
    r_io = lax.broadcasted_iota(I32, (win, win), 0)
    c_io = lax.broadcasted_iota(I32, (win, win), 1)
    later_mat = jnp.where(r_io > c_io, 1.0, 0.0).astype(BF16)
    scale = jnp.asarray(HEAD_DIM ** -0.5, BF16)
    chains = [(s, h) for s in range(n_sub) for h in range(2)]
    qh = []
    for s, h in chains:
        q = q_ref[s * blk:(s + 1) * blk, :]
        qh.append(jnp.where((lane < HEAD_DIM) == (h == 0), q, jnp.zeros_like(q)) * scale)
    first_end = [(i * n_sub + s + 1) * blk for s in range(n_sub)]
    acc_ref[...] = jnp.zeros_like(acc_ref)
    carry_ref[...] = jnp.zeros_like(carry_ref)

    def step(it):
        kwin, vwin, valid = [], [], []
        for s in range(n_sub):
            end = first_end[s] - it * win
            start = pl.multiple_of(jnp.maximum(end - win, 0), blk)
            kwin.append(k_ref[pl.ds(start, win), :])
            vwin.append(v_ref[pl.ds(start, win), :])
            valid.append((start + col) < jnp.minimum(first_end[s] - blk + row, end))
        z = [_dg(qh[c], kwin[s], NT) for c, (s, h) in enumerate(chains)]
        sp = [_softplus(x) for x in z]
        log_keep = [jnp.where(valid[s], -sp[c], 0.0).astype(BF16) for c, (s, h) in enumerate(chains)]
        carry = [carry_ref[c] for c in range(len(chains))]
        log_later = [_dg(log_keep[c], later_mat) + jnp.concatenate([carry[c]] * (win // LANES), axis=1)
                     for c in range(len(chains))]
        w = [jnp.where(valid[s], jnp.exp(z[c] - sp[c] + log_later[c]), 0.0).astype(BF16)
             for c, (s, h) in enumerate(chains)]
        pv = [_dg(w[c], vwin[s]) for c, (s, h) in enumerate(chains)]
        for c in range(len(chains)):
            acc_ref[c] += pv[c]
            carry_ref[c] = jnp.broadcast_to(log_later[c][:, 0:1] + log_keep[c][:, 0:1], (blk, LANES))
        return it + 1

    def cond(it):
        alive = None
        for s in range(n_sub):
            more = jnp.logical_and(first_end[s] - it * win > 0, jnp.max(carry_ref[2 * s:2 * s + 2]) > SB_DONE)
            alive = more if alive is None else jnp.logical_or(alive, more)
        return alive

    lax.while_loop(cond, step, step(0))
    for s in range(n_sub):
        o_ref[s * blk:(s + 1) * blk, :] = jnp.where(lane < HEAD_DIM, acc_ref[2 * s], acc_ref[2 * s + 1]).astype(o_ref.dtype)


def _sb_attention(qkv):
    t = qkv.shape[0]
    blk = SB_BLK * SB_SUBS
    n_pairs = MIX_WIDTH // LANES
    return pl.pallas_call(
        _sb_kernel,
        grid=(n_pairs, t // blk),
        in_specs=[pl.BlockSpec((blk, LANES), lambda hp, i: (i, hp)),
                  pl.BlockSpec((t, LANES), lambda hp, i: (0, n_pairs + hp)),
                  pl.BlockSpec((t, LANES), lambda hp, i: (0, 2 * n_pairs + hp))],
        out_specs=pl.BlockSpec((blk, LANES), lambda hp, i: (i, hp)),
        out_shape=jax.ShapeDtypeStruct((t, MIX_WIDTH), BF16),
        scratch_shapes=[pltpu.VMEM((2 * SB_SUBS, SB_BLK, LANES), F32), pltpu.VMEM((2 * SB_SUBS, SB_BLK, LANES), F32)],
        compiler_params=_cparams("parallel", "parallel"),
    )(qkv, qkv, qkv)


def _rwkv_pre_kernel(cur_ref, prev_ref, mu_ref, w0_ref, wl1_ref, wl2_ref, a0_ref, al1_ref, al2_ref,
                     gl1_ref, gl2_ref, r_ref, k_ref, v_ref, lw_ref, a_ref, g_ref):
    i = pl.program_id(0)
    tm = cur_ref.shape[0]
    wd = MIX_WIDTH
    first_row = lax.broadcasted_iota(I32, (tm, wd), 0) == 0
    keep_prev = jnp.where(i == 0, 0.0, 1.0)

    n_prev = prev_ref.shape[0]

    def shifted(col):
        z = cur_ref[:, col * wd:(col + 1) * wd].astype(F32)
        last = prev_ref[n_prev - 1:n_prev, col * wd:(col + 1) * wd].astype(F32) * keep_prev
        prev = jnp.where(first_row, last, pltpu.roll(z, 1, 0))
        return z, prev - z

    z, d = shifted(0)
    r_ref[...] = (z + d * mu_ref[0:1, :]).astype(r_ref.dtype)
    z, d = shifted(1)
    k_ref[...] = (z + d * mu_ref[1:2, :]).astype(k_ref.dtype)
    z, d = shifted(2)
    v_ref[...] = (z + d * mu_ref[2:3, :]).astype(v_ref.dtype)
    z, d = shifted(3)
    xw = z + d * mu_ref[3:4, :]
    xa = z + d * mu_ref[4:5, :]
    xg = z + d * mu_ref[5:6, :]
    w = w0_ref[...] + _mm1(jnp.tanh(_mm1(xw, wl1_ref[...])), wl2_ref[...])
    w = -_softplus(-w) - 0.5
    lw_ref[...] = -jnp.exp(w)
    a_ref[...] = _sigmoid(a0_ref[...] + _mm1(_mm1(xa, al1_ref[...]), al2_ref[...])).astype(a_ref.dtype)
    g_ref[...] = _mm1(_sigmoid(_mm1(xg, gl1_ref[...])), gl2_ref[...]).astype(g_ref.dtype)


def _pad_to(x, axis, size):
    pad = [(0, 0)] * x.ndim
    pad[axis] = (0, size - x.shape[axis])
    return jnp.pad(x, pad)


def _rwkv_pre(rw, mu, w0, wl1, wl2, a0, al1, al2, gl1, gl2):
    t = rw.shape[0]
    tm = min(RW_TM, t)
    wd = MIX_WIDTH
    lo = LANES

    def lora_pair(l1, l2):
        n = -(-l1.shape[1] // lo) * lo
        return _pad_to(l1, 1, n).astype(BF16), _pad_to(l2, 0, n).astype(BF16)

    wl1, wl2 = lora_pair(wl1, wl2)
    al1, al2 = lora_pair(al1, al2)
    gl1, gl2 = lora_pair(gl1, gl2)
    full = lambda x: pl.BlockSpec(x.shape, lambda i: (0,) * x.ndim)
    row = lambda: pl.BlockSpec((tm, wd), lambda i: (i, 0))
    consts = (mu, w0.reshape(1, wd), wl1, wl2, a0.reshape(1, wd), al1, al2, gl1, gl2)
    n_prev = 32 // rw.dtype.itemsize
    out_dtypes = (BF16, BF16, BF16, F32, BF16, BF16)
    return pl.pallas_call(
        _rwkv_pre_kernel,
        grid=(t // tm,),
        in_specs=[pl.BlockSpec((tm, 4 * wd), lambda i: (i, 0)),
                  pl.BlockSpec((n_prev, 4 * wd), lambda i: (jnp.maximum(i * (tm // n_prev) - 1, 0), 0))]
                 + [full(c) for c in consts],
        out_specs=[row() for _ in range(6)],
        out_shape=[jax.ShapeDtypeStruct((t, wd), dt) for dt in out_dtypes],
        compiler_params=_cparams("parallel"),
    )(rw, rw, *consts)


def _rwkv_chunk_kernel(r_ref, k_ref, v_ref, lw_ref, a_ref, kkp_ref, kap_ref, rkp_ref,
                       rp_ref, y0_ref, bon_ref, p_ref, q_ref):
    c_len = RW_C
    n2 = 2 * c_len
    n_chunks = r_ref.shape[0] // c_len
    r2 = lax.broadcasted_iota(I32, (n2, LANES), 0)
    c2 = lax.broadcasted_iota(I32, (n2, LANES), 1)
    same = (r2 < c_len) == (c2 < HEAD_DIM)
    t_r = jnp.where(r2 < c_len, r2, r2 - c_len)
    t_c = jnp.where(c2 < HEAD_DIM, c2, c2 - HEAD_DIM)
    strict = jnp.logical_and(same, t_r > t_c)
    incl = jnp.logical_and(same, t_r >= t_c)
    eye = jnp.where(r2 == c2, 1.0, 0.0).astype(F32)
    group_ones = jnp.where(same, 1.0, 0.0).astype(BF16)
    lr = lax.broadcasted_iota(I32, (c_len, c_len), 0)
    lc = lax.broadcasted_iota(I32, (c_len, c_len), 1)
    cum_mat = jnp.where(lr >= lc, 1.0, 0.0).astype(BF16)
    kkp, kap, rkp = kkp_ref[...], kap_ref[...], rkp_ref[...]

    def stack(z):
        return jnp.where(same, jnp.concatenate([z, z], axis=0), 0.0)

    def unstack(zs):
        return zs[:c_len] + zs[c_len:]

    cs = range(n_chunks)
    rows = [pl.ds(c * c_len, c_len) for c in cs]
    r = [r_ref[rw, :].astype(F32) for rw in rows]
    kr = [k_ref[rw, :].astype(F32) for rw in rows]
    v = [v_ref[rw, :].astype(F32) for rw in rows]
    lw = [lw_ref[rw, :] for rw in rows]
    a = [a_ref[rw, :].astype(F32) for rw in rows]
    kk = [x * kkp for x in kr]
    cum = [_mm_sum_lhs(cum_mat, x) for x in lw]
    ssq = [_dg((x * x).astype(BF16), group_ones) for x in kk]
    km = [kr[c] * (1.0 + (a[c] - 1.0) * kap) for c in cs]
    bsum = [_dg((r[c] * km[c] * rkp).astype(BF16), group_ones) for c in cs]
    kk = [kk[c] / jnp.maximum(jnp.sqrt(ssq[c]), 1e-12) for c in cs]
    bv = [kk[c] * a[c] for c in cs]
    last = [x[c_len - 1:c_len, :] for x in cum]
    rt = [stack(r[c] * jnp.exp(cum[c])) for c in cs]
    at_b = [stack(-kk[c] * jnp.exp(cum[c] - lw[c])).astype(BF16) for c in cs]
    rt_b = [x.astype(BF16) for x in rt]
    g_inv = [jnp.exp(-x) for x in cum]
    btkt = [jnp.concatenate([stack(bv[c] * g_inv[c]), stack(km[c] * g_inv[c])], axis=0).astype(BF16) for c in cs]
    m_a = [_dg(at_b[c], btkt[c], NT) for c in cs]
    m_r = [_dg(rt_b[c], btkt[c], NT) for c in cs]
    vs_b = [stack(x).astype(BF16) for x in v]
    g_end = [jnp.exp(last[c] - cum[c]) for c in cs]
    bd_b = [stack(bv[c] * g_end[c]).astype(BF16) for c in cs]
    kd_b = [stack(km[c] * g_end[c]).astype(BF16) for c in cs]
    m_ab = [jnp.where(strict, x[:, :n2], 0.0) for x in m_a]
    m_ak = [jnp.where(strict, x[:, n2:], 0.0).astype(BF16) for x in m_a]
    m_rb = [jnp.where(incl, x[:, :n2], 0.0).astype(BF16) for x in m_r]
    m_rk = [jnp.where(incl, x[:, n2:], 0.0).astype(BF16) for x in m_r]
    akv = [_dg(m_ak[c], vs_b[c]) for c in cs]
    rkv = [_dg(m_rk[c], vs_b[c]) for c in cs]
    kdv = [_dg(kd_b[c], vs_b[c], TN) for c in cs]
    tinv = [eye + x for x in m_ab]
    npow = [x.astype(BF16) for x in m_ab]
    for _ in range(5):
        npow = [_dg(x, x).astype(BF16) for x in npow]
        tinv = [tinv[c] + _dg(tinv[c].astype(BF16), npow[c]) for c in cs]
    au_b = [_dg(tinv[c].astype(BF16), jnp.concatenate([at_b[c], akv[c].astype(BF16)], axis=1)).astype(BF16)
            for c in cs]
    ry = [_dg(m_rb[c], au_b[c]) for c in cs]
    pq = [_dg(bd_b[c], au_b[c], TN) for c in cs]
    for c in cs:
        prow = pl.ds(c * n2, n2)
        rp_ref[rows[c], :] = unstack(rt[c] + ry[c][:, :LANES]).astype(rp_ref.dtype)
        y0_ref[rows[c], :] = unstack(ry[c][:, LANES:] + rkv[c])
        p_ref[prow, :] = (eye * jnp.exp(last[c]) + pq[c][:, :LANES]).astype(p_ref.dtype)
        q_ref[prow, :] = pq[c][:, LANES:] + kdv[c]
        bon_ref[rows[c], :] = (bsum[c] * v[c]).astype(bon_ref.dtype)


def _rwkv_chunk(r, k, v, lw, a, k_k, k_a, r_k):
    t = r.shape[0]
    rb = min(RW_C * RW_NCH, t)
    n_pairs = MIX_WIDTH // LANES
    row = lambda: pl.BlockSpec((rb, LANES), lambda i, hp: (i, hp))
    par = lambda: pl.BlockSpec((1, LANES), lambda i, hp: (0, hp))
    mat = lambda: pl.BlockSpec((2 * rb, LANES), lambda i, hp: (i, hp))
    return pl.pallas_call(
        _rwkv_chunk_kernel,
        grid=(t // rb, n_pairs),
        in_specs=[row() for _ in range(5)] + [par() for _ in range(3)],
        out_specs=[row(), row(), row(), mat(), mat()],
        out_shape=[jax.ShapeDtypeStruct((t, MIX_WIDTH), BF16), jax.ShapeDtypeStruct((t, MIX_WIDTH), F32),
                   jax.ShapeDtypeStruct((t, MIX_WIDTH), BF16), jax.ShapeDtypeStruct((2 * t, MIX_WIDTH), BF16),
                   jax.ShapeDtypeStruct((2 * t, MIX_WIDTH), F32)],
        compiler_params=_cparams("parallel", "parallel"),
    )(r, k, v, lw, a, k_k.reshape(1, -1), k_a.reshape(1, -1), r_k.reshape(1, -1))


def _rwkv_state_kernel(rp_ref, y0_ref, p_ref, q_ref, o_ref, st_ref):
    @pl.when(pl.program_id(0) == 0)
    def _():
        st_ref[...] = jnp.zeros_like(st_ref)

    c_len = RW_C
    pairs = range(MIX_WIDTH // LANES)
    lanes = [slice(hp * LANES, (hp + 1) * LANES) for hp in pairs]
    st = [st_ref[hp] for hp in pairs]
    for c in range(rp_ref.shape[0] // c_len):
        rows = slice(c * c_len, (c + 1) * c_len)
        mrows = slice(2 * c * c_len, 2 * (c + 1) * c_len)
        st_b = [x.astype(BF16) for x in st]
        st = [_dg(p_ref[mrows, lanes[hp]], st_b[hp]) + q_ref[mrows, lanes[hp]] for hp in pairs]
        for hp in pairs:
            o_ref[rows, lanes[hp]] = _dg(rp_ref[rows, lanes[hp]], st_b[hp]) + y0_ref[rows, lanes[hp]]
    for hp in pairs:
        st_ref[hp] = st[hp]


def _rwkv_state(rp, y0, p, q):
    t = rp.shape[0]
    rb = min(RW_C * RW_NCS, t)
    wd = MIX_WIDTH
    row = lambda: pl.BlockSpec((rb, wd), lambda i: (i, 0))
    mat = lambda: pl.BlockSpec((2 * rb, wd), lambda i: (i, 0))
    return pl.pallas_call(
        _rwkv_state_kernel,
        grid=(t // rb,),
        in_specs=[row(), row(), mat(), mat()],
        out_specs=row(),
        out_shape=jax.ShapeDtypeStruct((t, wd), F32),
        scratch_shapes=[pltpu.VMEM((wd // LANES, LANES, LANES), F32)],
        compiler_params=_cparams("arbitrary"),
    )(rp, y0, p, q)


def _mix_kernel(ya_ref, y_ref, bon_ref, g_ref, gate_ref, x_ref, gnw_ref, gnb_ref, wa_ref, wb_ref, wo_ref,
                lnw_ref, lnb_ref, x1_ref, x1b_ref):
    r2 = lax.broadcasted_iota(I32, (LANES, LANES), 0)
    c2 = lax.broadcasted_iota(I32, (LANES, LANES), 1)
    group_ones = jnp.where((r2 < HEAD_DIM) == (c2 < HEAD_DIM), 1.0, 0.0).astype(BF16)
    inv_n = 1.0 / HEAD_DIM
    pieces = []
    for hp in range(MIX_WIDTH // LANES):
        ls = slice(hp * LANES, (hp + 1) * LANES)
        y = y_ref[:, ls]
        d = y - _dg(y.astype(BF16), group_ones) * inv_n
        var = _dg((d * d).astype(BF16), group_ones) * inv_n
        yn = d * lax.rsqrt(var + GN_EPS) * gnw_ref[:, ls] + gnb_ref[:, ls]
        pieces.append(((yn + bon_ref[:, ls]) * g_ref[:, ls]).astype(BF16))
    ya = _dg(ya_ref[...], wa_ref[...])
    yb = _dg(jnp.concatenate(pieces, axis=1), wb_ref[...])
    ga = _sigmoid(gate_ref[:, :D_MODEL].astype(F32))
    gb = _sigmoid(gate_ref[:, D_MODEL:].astype(F32))
    mixed = _dg((ga * ya + gb * yb).astype(BF16), wo_ref[...])
    x1 = _layer_norm(DEEPNORM_ALPHA * x_ref[...] + mixed, lnw_ref[...], lnb_ref[...])
    x1_ref[...] = x1
    x1b_ref[...] = x1.astype(BF16)


def _const_spec(x):
    return pl.BlockSpec(x.shape, lambda *_: (0,) * x.ndim, pipeline_mode=pl.Buffered(1))


def _mix(ya, y, bon, g, gates, x, gn_w, gn_b, w_up_a, w_up_b, w_o, ln_w, ln_b):
    t = x.shape[0]
    tm = min(MIX_TM, t)
    consts = (gn_w.reshape(1, -1), gn_b.reshape(1, -1), w_up_a, w_up_b, w_o, ln_w.reshape(1, -1), ln_b.reshape(1, -1))
    row = lambda w: pl.BlockSpec((tm, w), lambda i: (i, 0))
    return pl.pallas_call(
        _mix_kernel,
        grid=(t // tm,),
        in_specs=[row(MIX_WIDTH), row(MIX_WIDTH), row(MIX_WIDTH), row(MIX_WIDTH), row(2 * D_MODEL), row(D_MODEL)]
                 + [_const_spec(c) for c in consts],
        out_specs=[pl.BlockSpec((tm, D_MODEL), lambda i: (i, 0))] * 2,
        out_shape=[jax.ShapeDtypeStruct((t, D_MODEL), F32), jax.ShapeDtypeStruct((t, D_MODEL), BF16)],
        compiler_params=_cparams("parallel"),
    )(ya, y, bon, g, gates, x, *consts)


def _topk_rows(s, ids, k):
    big = jnp.asarray(1e9, F32)
    vals, idxs = [], []
    for _ in range(k):
        m = jnp.max(s, axis=0, keepdims=True)
        ix = jnp.min(jnp.where(s == m, ids, big), axis=0, keepdims=True)
        vals.append(m)
        idxs.append(ix)
        s = jnp.where(ids == ix, -jnp.inf, s)
    return jnp.concatenate(vals, axis=0), jnp.concatenate(idxs, axis=0)


def _route_kernel(x_ref, wq_ref, keys_ref, e0_ref, e1_ref, gate_ref):
    tm = x_ref.shape[0]
    topk = PEER_TOPK
    q = _dg(x_ref[...], wq_ref[...])
    qb = q.astype(BF16)
    half = N_KEYS
    key_ids = lax.broadcasted_iota(I32, (N_KEYS, tm), 0).astype(F32)
    sub8 = lax.broadcasted_iota(I32, (8, tm), 0).astype(F32)
    sub16 = lax.broadcasted_iota(I32, (topk, tm), 0).astype(F32)
    cand_ids = jnp.concatenate([sub16] + [a * topk + sub8 for a in range(1, 8)] + [(sub8 + 8.0) * topk], axis=0)
    e0s, e1s, gates = [], [], []
    for h in range(PEER_HEADS):
        tops = []
        for c in range(2):
            qs = qb[:, (2 * h + c) * half:(2 * h + c + 1) * half]
            s = _dg(keys_ref[c], qs, NT)
            tops.append(_topk_rows(s, key_ids, topk))
        (s0, i0), (s1, i1) = tops
        cand = jnp.concatenate([s0[0:1, :] + s1] + [s0[a:a + 1, :] + s1[:8, :] for a in range(1, 8)]
                               + [s0[8:, :] + s1[0:1, :]], axis=0)
        best, pos = _topk_rows(cand, cand_ids, topk)
        a_sel = jnp.floor(pos * (1.0 / topk))
        b_sel = pos - a_sel * topk
        e0 = jnp.zeros_like(pos)
        e1 = jnp.zeros_like(pos)
        for j in range(topk):
            e0 = jnp.where(a_sel == j, i0[j:j + 1, :], e0)
            e1 = jnp.where(b_sel == j, i1[j:j + 1, :], e1)
        ex = jnp.exp(best - best[0:1, :])
        gates.append(ex / jnp.sum(ex, axis=0, keepdims=True))
        e0s.append(e0)
        e1s.append(e1)
    e0_ref[...] = jnp.concatenate(e0s, axis=0).T.astype(I32)
    e1_ref[...] = jnp.concatenate(e1s, axis=0).T.astype(I32)
    gate_ref[...] = jnp.concatenate(gates, axis=0).T


def _route(x1b, w_q, sub_keys):
    t = x1b.shape[0]
    tm = min(ROUTE_TM, t)
    out = lambda: pl.BlockSpec((tm, N_SEL), lambda i: (i, 0))
    return pl.pallas_call(
        _route_kernel,
        grid=(t // tm,),
        in_specs=[pl.BlockSpec((tm, D_MODEL), lambda i: (i, 0)), _const_spec(w_q), _const_spec(sub_keys)],
        out_specs=[out(), out(), out()],
        out_shape=[jax.ShapeDtypeStruct((t, N_SEL), I32), jax.ShapeDtypeStruct((t, N_SEL), I32),
                   jax.ShapeDtypeStruct((t, N_SEL), F32)],
        compiler_params=_cparams("parallel"),
    )(x1b, w_q, sub_keys)


def _gelu_exact(x):
    return 0.5 * x * (1.0 + lax.erf(x * (2.0 ** -0.5)))


def _peer_score_kernel(x_ref, u_ref, e0_ref, e1_ref, gate_ref, o_ref, acc_ref):
    j = pl.program_id(1)

    @pl.when(j == 0)
    def _():
        acc_ref[...] = jnp.zeros_like(acc_ref)

    x = x_ref[...]
    e0, e1 = e0_ref[...], e1_ref[...]
    blocks = u_ref.shape[0] // N_KEYS
    per_dot = PH_SUB // N_KEYS
    acc = acc_ref[...]
    for s in range(u_ref.shape[0] // PH_SUB):
        h = _dg(x, u_ref[s * PH_SUB:(s + 1) * PH_SUB, :].astype(BF16), NT)
        for b in range(per_dot):
            picked = jnp.take_along_axis(h[:, b * N_KEYS:(b + 1) * N_KEYS], e1, axis=1)
            acc = acc + jnp.where(e0 == j * blocks + s * per_dot + b, picked, 0.0)
    acc_ref[...] = acc

    @pl.when(j == pl.num_programs(1) - 1)
    def _():
        o_ref[...] = _gelu_exact(acc_ref[...]) * gate_ref[...]


def _peer_score(x1b, u_bf16, e0, e1, gate):
    t = x1b.shape[0]
    tm = min(PH_TM, t)
    tok = lambda: pl.BlockSpec((tm, N_SEL), lambda i, j: (i, 0))
    return pl.pallas_call(
        _peer_score_kernel,
        grid=(t // tm, N_EXPERTS // PH_TN),
        in_specs=[pl.BlockSpec((tm, D_MODEL), lambda i, j: (i, 0)),
                  pl.BlockSpec((PH_TN, D_MODEL), lambda i, j: (j, 0)), tok(), tok(), tok()],
        out_specs=tok(),
        out_shape=jax.ShapeDtypeStruct((t, N_SEL), F32),
        scratch_shapes=[pltpu.VMEM((tm, N_SEL), F32)],
        compiler_params=_cparams("parallel", "arbitrary"),
    )(x1b, u_bf16, e0, e1, gate)


def _peer_gate_kernel(e0_ref, e1_ref, act_ref, o_ref):
    sub = lax.broadcasted_iota(I32, (N_KEYS, N_SEL), 0)

    def body(grp, carry):
        mats = []
        for u in range(PG_GRP):
            row = pl.ds(grp * PG_GRP + u, 1)
            left = jnp.where(e0_ref[row, :] == sub, act_ref[row, :], 0.0).astype(BF16)
            right = jnp.where(e1_ref[row, :] == sub, 1.0, 0.0).astype(BF16)
            mats.append(_dg(left, right, NT))
        o_ref[grp] = jnp.swapaxes(jnp.stack(mats, axis=0), 0, 1).astype(o_ref.dtype)
        return carry

    lax.fori_loop(0, o_ref.shape[0], body, 0)


def _peer_gate(e0, e1, act):
    t = e0.shape[0]
    tt = min(PG_TT, t)
    tok = lambda: pl.BlockSpec((tt, N_SEL), lambda i: (i, 0))
    return pl.pallas_call(
        _peer_gate_kernel,
        grid=(t // tt,),
        in_specs=[tok(), tok(), tok()],
        out_specs=pl.BlockSpec((tt // PG_GRP, N_KEYS, PG_GRP, N_KEYS), lambda i: (i, 0, 0, 0)),
        out_shape=jax.ShapeDtypeStruct((t // PG_GRP, N_KEYS, PG_GRP, N_KEYS), BF16),
        compiler_params=_cparams("parallel"),
    )(e0, e1, act)


def _peer_value_kernel(g_ref, v_ref, o_ref):
    @pl.when(pl.program_id(1) == 0)
    def _():
        o_ref[...] = jnp.zeros_like(o_ref)

    tm = o_ref.shape[0]
    lhs = jnp.concatenate([g_ref[:, e, :, :].reshape(tm, N_KEYS) for e in range(g_ref.shape[1])], axis=1)
    o_ref[...] += _dg(lhs, v_ref[...])


def _peer_value(g4, v_bf16):
    t = g4.shape[0] * PG_GRP
    tm = min(PV_TM, t)
    return pl.pallas_call(
        _peer_value_kernel,
        grid=(t // tm, N_KEYS // PV_NE0),
        in_specs=[pl.BlockSpec((tm // PG_GRP, PV_NE0, PG_GRP, N_KEYS), lambda i, k: (i, k, 0, 0)),
                  pl.BlockSpec((PV_NE0 * N_KEYS, D_MODEL), lambda i, k: (k, 0))],
        out_specs=pl.BlockSpec((tm, D_MODEL), lambda i, k: (i, 0)),
        out_shape=jax.ShapeDtypeStruct((t, D_MODEL), F32),
        compiler_params=_cparams("parallel", "arbitrary"),
    )(g4, v_bf16)


def _final_kernel(x1_ref, x1b_ref, ffn_ref, p_ref, wg_ref, wp_ref, lnw_ref, lnb_ref, o_ref):
    ple = _sigmoid(_dg(x1b_ref[...], wg_ref[...])) * _dg(p_ref[...].astype(BF16), wp_ref[...])
    o_ref[...] = _layer_norm(DEEPNORM_ALPHA * x1_ref[...] + ffn_ref[...] + ple, lnw_ref[...], lnb_ref[...])


def _final(x1, x1b, ffn, p, w_gate, w_proj, ln_w, ln_b):
    t = x1.shape[0]
    tm = min(FIN_TM, t)
    consts = (w_gate, w_proj, ln_w.reshape(1, -1), ln_b.reshape(1, -1))
    row = lambda w: pl.BlockSpec((tm, w), lambda i: (i, 0))
    return pl.pallas_call(
        _final_kernel,
        grid=(t // tm,),
        in_specs=[row(D_MODEL), row(D_MODEL), row(D_MODEL), row(p.shape[1])] + [_const_spec(c) for c in consts],
        out_specs=row(D_MODEL),
        out_shape=jax.ShapeDtypeStruct((t, D_MODEL), F32),
        compiler_params=_cparams("parallel"),
    )(x1, x1b, ffn, p, *consts)


def _layer(x, p, w_in, rwkv_mu, rwkv_w0, rwkv_w_lora1, rwkv_w_lora2, rwkv_a0, rwkv_a_lora1,
           rwkv_a_lora2, rwkv_g_lora1, rwkv_g_lora2, rwkv_k_k, rwkv_k_a, rwkv_r_k, rwkv_lnx_w,
           rwkv_lnx_b, w_up_a, w_up_b, w_o, ln1_w, ln1_b, peer_w_q, peer_sub_keys, peer_u, peer_v,
           ple_w_gate, ple_w_proj, ln2_w, ln2_b):
    bf = lambda w: w.astype(BF16)
    n_sb = 3 * MIX_WIDTH
    n_rw = 4 * MIX_WIDTH
    qkv = _proj(x, w_in, 0, n_sb, BF16)
    rw = _proj(x, w_in, n_sb, n_rw, BF16)
    gates = _proj(x, w_in, n_sb + n_rw, 2 * D_MODEL, BF16)
    y_a = _sb_attention(qkv)
    r, k, v, lw, a, g = _rwkv_pre(rw, rwkv_mu, rwkv_w0, rwkv_w_lora1, rwkv_w_lora2, rwkv_a0,
                                  rwkv_a_lora1, rwkv_a_lora2, rwkv_g_lora1, rwkv_g_lora2)
    rp, y0, bon, pm, qm = _rwkv_chunk(r, k, v, lw, a, rwkv_k_k, rwkv_k_a, rwkv_r_k.reshape(-1))
    y_b = _rwkv_state(rp, y0, pm, qm)
    x1, x1b = _mix(y_a, y_b, bon, g, gates, x, rwkv_lnx_w, rwkv_lnx_b, bf(w_up_a), bf(w_up_b), bf(w_o),
                   ln1_w, ln1_b)
    e0, e1, gate = _route(x1b, bf(peer_w_q), bf(peer_sub_keys))
    act = _peer_score(x1b, peer_u, e0, e1, gate)
    ffn = _peer_value(_peer_gate(e0, e1, act), bf(peer_v))
    return _final(x1, x1b, ffn, p, bf(ple_w_gate), bf(ple_w_proj), ln2_w, ln2_b)


def kernel(x, p, w_in, rwkv_mu, rwkv_w0, rwkv_w_lora1, rwkv_w_lora2, rwkv_a0, rwkv_a_lora1, rwkv_a_lora2, rwkv_g_lora1, rwkv_g_lora2, rwkv_k_k, rwkv_k_a, rwkv_r_k, rwkv_lnx_w, rwkv_lnx_b, w_up_a, w_up_b, w_o, ln1_w, ln1_b, peer_w_q, peer_sub_keys, peer_u, peer_v, ple_w_gate, ple_w_proj, ln2_w, ln2_b):
    bsz, t, d = x.shape
    depth = w_in.shape[0]
    xs = x.reshape(bsz * t, d)
    assert bsz == 1, "token shift / attention / scan treat the flattened rows as one sequence"
    for i in range(depth):
        xs = _layer(xs, p[i].reshape(bsz * t, -1), w_in[i], rwkv_mu[i], rwkv_w0[i], rwkv_w_lora1[i],
                    rwkv_w_lora2[i], rwkv_a0[i], rwkv_a_lora1[i], rwkv_a_lora2[i], rwkv_g_lora1[i],
                    rwkv_g_lora2[i], rwkv_k_k[i], rwkv_k_a[i], rwkv_r_k[i], rwkv_lnx_w[i], rwkv_lnx_b[i],
                    w_up_a[i], w_up_b[i], w_o[i], ln1_w[i], ln1_b[i], peer_w_q[i], peer_sub_keys[i],
                    peer_u[i], peer_v[i], ple_w_gate[i], ple_w_proj[i], ln2_w[i], ln2_b[i])
    return xs.reshape(bsz, t, d)
```

```python
import functools

import jax
import jax.numpy as jnp
from jax import lax
from jax.experimental import pallas as pl
from jax.experimental.pallas import tpu as pltpu

F32 = jnp.float32
BF16 = jnp.bfloat16
I32 = jnp.int32

D_MODEL = 2048
HEAD_DIM = 64
N_HEADS = 16
MIX_WIDTH = N_HEADS * HEAD_DIM
PEER_HEADS = 8
PEER_TOPK = 16
N_KEYS = 128
N_EXPERTS = N_KEYS * N_KEYS
N_SEL = PEER_HEADS * PEER_TOPK
GN_EPS = 64e-5
LN_EPS = 1e-5
DEEPNORM_ALPHA = 2.0 ** 0.25

LANES = 128
VMEM_LIMIT = 56 * 1024 * 1024

MM_TM, MM_TN = 1024, 1024
SB_BLK = 128
SB_SUBS = 4
SB_WIN = 2 * LANES
SB_DONE = -88.0
RW_TM = 256
RW_C = 64
RW_NCH = 8
RW_NCS = 8
MIX_TM = 256
ROUTE_TM = 256
PH_TM, PH_TN = 1024, 2048
PH_SUB = 256
PG_TT = 128
PG_GRP = 16
PV_TM, PV_NE0 = 1024, 16
FIN_TM = 512

assert 2 * RW_C == LANES and RW_C == HEAD_DIM

NN = (((1,), (0,)), ((), ()))
NT = (((1,), (1,)), ((), ()))
TN = (((0,), (0,)), ((), ()))


def _cparams(*sem):
    return pltpu.CompilerParams(dimension_semantics=tuple(sem), vmem_limit_bytes=VMEM_LIMIT)


def _dg(a, b, dims=NN):
    return lax.dot_general(a, b, dims, preferred_element_type=F32)


def _split2(x):
    hi = x.astype(BF16)
    lo = (x - hi.astype(F32)).astype(BF16)
    return hi, lo


def _mm1(a, b, dims=NN):
    return _dg(a.astype(BF16), b.astype(BF16), dims)


def _mm_sum_lhs(m_bf16, x):
    hi, lo = _split2(x)
    return _dg(m_bf16, hi) + _dg(m_bf16, lo)


def _softplus(z):
    return jnp.maximum(z, 0.0) + jnp.log(1.0 + jnp.exp(-jnp.abs(z)))


def _sigmoid(z):
    return 0.5 * jnp.tanh(0.5 * z) + 0.5


def _layer_norm(x, g, b):
    mu = jnp.mean(x, axis=-1, keepdims=True)
    d = x - mu
    var = jnp.mean(d * d, axis=-1, keepdims=True)
    return d * lax.rsqrt(var + LN_EPS) * g + b


def _proj_kernel(a_ref, b_ref, o_ref, bw_ref):
    @pl.when(pl.program_id(1) == 0)
    def _():
        bw_ref[...] = b_ref[...].astype(BF16)

    o_ref[...] = _dg(a_ref[...].astype(BF16), bw_ref[...]).astype(o_ref.dtype)


def _proj(a, b, col0, n, out_dtype):
    m, k = a.shape
    tm, tn = min(MM_TM, m), MM_TN
    assert col0 % tn == 0 and n % tn == 0
    off = col0 // tn
    return pl.pallas_call(
        _proj_kernel,
        grid=(n // tn, m // tm),
        in_specs=[pl.BlockSpec((tm, k), lambda j, i: (i, 0)),
                  pl.BlockSpec((k, tn), lambda j, i: (0, j + off))],
        out_specs=pl.BlockSpec((tm, tn), lambda j, i: (i, j)),
        out_shape=jax.ShapeDtypeStruct((m, n), out_dtype),
        scratch_shapes=[pltpu.VMEM((k, tn), BF16)],
        compiler_params=_cparams("parallel", "arbitrary"),
    )(a, b)


def _sb_kernel(q_ref, k_ref, v_ref, o_ref, acc_ref, carry_ref):
    i = pl.program_id(1)
    blk = SB_BLK
    n_sub = q_ref.shape[0] // blk
    win = SB_WIN
    lane = lax.broadcasted_iota(I32, (1, LANES), 1)
    row = lax.broadcasted_iota(I32, (blk, win), 0)
    col = lax.broadcasted_iota(I32, (blk, win), 1)
    r_io = lax.broadcasted_iota(I32, (win, win), 0)
    c_io = lax.broadcasted_iota(I32, (win, win), 1)
    later_mat = jnp.where(r_io > c_io, 1.0, 0.0).astype(BF16)
    scale = jnp.asarray(HEAD_DIM ** -0.5, BF16)
    chains = [(s, h) for s in range(n_sub) for h in range(2)]
    qh = []
    for s, h in chains:
        q = q_ref[s * blk:(s + 1) * blk, :]
        qh.append(jnp.where((lane < HEAD_DIM) == (h == 0), q, jnp.zeros_like(q)) * scale)
    first_end = [(i * n_sub + s + 1) * blk for s in range(n_sub)]
    acc_ref[...] = jnp.zeros_like(acc_ref)
    carry_ref[...] = jnp.zeros_like(carry_ref)

    def step(it):
        kwin, vwin, valid = [], [], []
        for s in range(n_sub):
            end = first_end[s] - it * win
            start = pl.multiple_of(jnp.maximum(end - win, 0), blk)
            kwin.append(k_ref[pl.ds(start, win), :])
            vwin.append(v_ref[pl.ds(start, win), :])
            valid.append((start + col) < jnp.minimum(first_end[s] - blk + row, end))
        z = [_dg(qh[c], kwin[s], NT) for c, (s, h) in enumerate(chains)]
        sp = [_softplus(x) for x in z]
        log_keep = [jnp.where(valid[s], -sp[c], 0.0).astype(BF16) for c, (s, h) in enumerate(chains)]
        carry = [carry_ref[c] for c in range(len(chains))]
        log_later = [_dg(log_keep[c], later_mat) + jnp.concatenate([carry[c]] * (win // LANES), axis=1)
                     for c in range(len(chains))]
        w = [jnp.where(valid[s], jnp.exp(z[c] - sp[c] + log_later[c]), 0.0).astype(BF16)
             for c, (s, h) in enumerate(chains)]
        pv = [_dg(w[c], vwin[s]) for c, (s, h) in enumerate(chains)]
        for c in range(len(chains)):
            acc_ref[c] += pv[c]
            carry_ref[c] = jnp.broadcast_to(log_later[c][:, 0:1] + log_keep[c][:, 0:1], (blk, LANES))
        return it + 1

    def cond(it):
        alive = None
        for s in range(n_sub):
            more = jnp.logical_and(first_end[s] - it * win > 0, jnp.max(carry_ref[2 * s:2 * s + 2]) > SB_DONE)
            alive = more if alive is None else jnp.logical_or(alive, more)
        return alive

    lax.while_loop(cond, step, step(0))
    for s in range(n_sub):
        o_ref[s * blk:(s + 1) * blk, :] = jnp.where(lane < HEAD_DIM, acc_ref[2 * s], acc_ref[2 * s + 1]).astype(o_ref.dtype)


def _sb_attention(qkv):
    t = qkv.shape[0]
    blk = SB_BLK * SB_SUBS
    n_pairs = MIX_WIDTH // LANES
    return pl.pallas_call(
        _sb_kernel,
        grid=(n_pairs, t // blk),
        in_specs=[pl.BlockSpec((blk, LANES), lambda hp, i: (i, hp)),
                  pl.BlockSpec((t, LANES), lambda hp, i: (0, n_pairs + hp)),
                  pl.BlockSpec((t, LANES), lambda hp, i: (0, 2 * n_pairs + hp))],
        out_specs=pl.BlockSpec((blk, LANES), lambda hp, i: (i, hp)),
        out_shape=jax.ShapeDtypeStruct((t, MIX_WIDTH), BF16),
        scratch_shapes=[pltpu.VMEM((2 * SB_SUBS, SB_BLK, LANES), F32), pltpu.VMEM((2 * SB_SUBS, SB_BLK, LANES), F32)],
        compiler_params=_cparams("parallel", "parallel"),
    )(qkv, qkv, qkv)


def _rwkv_pre_kernel(cur_ref, prev_ref, mu_ref, w0_ref, wl1_ref, wl2_ref, a0_ref, al1_ref, al2_ref,
                     gl1_ref, gl2_ref, r_ref, k_ref, v_ref, lw_ref, a_ref, g_ref):
    i = pl.program_id(0)
    tm = cur_ref.shape[0]
    wd = MIX_WIDTH
    first_row = lax.broadcasted_iota(I32, (tm, wd), 0) == 0
    keep_prev = jnp.where(i == 0, 0.0, 1.0)

    n_prev = prev_ref.shape[0]

    def shifted(col):
        z = cur_ref[:, col * wd:(col + 1) * wd].astype(F32)
        last = prev_ref[n_prev - 1:n_prev, col * wd:(col + 1) * wd].astype(F32) * keep_prev
        prev = jnp.where(first_row, last, pltpu.roll(z, 1, 0))
        return z, prev - z

    z, d = shifted(0)
    r_ref[...] = (z + d * mu_ref[0:1, :]).astype(r_ref.dtype)
    z, d = shifted(1)
    k_ref[...] = (z + d * mu_ref[1:2, :]).astype(k_ref.dtype)
    z, d = shifted(2)
    v_ref[...] = (z + d * mu_ref[2:3, :]).astype(v_ref.dtype)
    z, d = shifted(3)
    xw = z + d * mu_ref[3:4, :]
    xa = z + d * mu_ref[4:5, :]
    xg = z + d * mu_ref[5:6, :]
    w = w0_ref[...] + _mm1(jnp.tanh(_mm1(xw, wl1_ref[...])), wl2_ref[...])
    w = -_softplus(-w) - 0.5
    lw_ref[...] = -jnp.exp(w)
    a_ref[...] = _sigmoid(a0_ref[...] + _mm1(_mm1(xa, al1_ref[...]), al2_ref[...])).astype(a_ref.dtype)
    g_ref[...] = _mm1(_sigmoid(_mm1(xg, gl1_ref[...])), gl2_ref[...]).astype(g_ref.dtype)


def _pad_to(x, axis, size):
    pad = [(0, 0)] * x.ndim
    pad[axis] = (0, size - x.shape[axis])
    return jnp.pad(x, pad)


def _rwkv_pre(rw, mu, w0, wl1, wl2, a0, al1, al2, gl1, gl2):
    t = rw.shape[0]
    tm = min(RW_TM, t)
    wd = MIX_WIDTH
    lo = LANES

    def lora_pair(l1, l2):
        n = -(-l1.shape[1] // lo) * lo
        return _pad_to(l1, 1, n).astype(BF16), _pad_to(l2, 0, n).astype(BF16)

    wl1, wl2 = lora_pair(wl1, wl2)
    al1, al2 = lora_pair(al1, al2)
    gl1, gl2 = lora_pair(gl1, gl2)
    full = lambda x: pl.BlockSpec(x.shape, lambda i: (0,) * x.ndim)
    row = lambda: pl.BlockSpec((tm, wd), lambda i: (i, 0))
    consts = (mu, w0.reshape(1, wd), wl1, wl2, a0.reshape(1, wd), al1, al2, gl1, gl2)
    n_prev = 32 // rw.dtype.itemsize
    out_dtypes = (BF16, BF16, BF16, F32, BF16, BF16)
    return pl.pallas_call(
        _rwkv_pre_kernel,
        grid=(t // tm,),
        in_specs=[pl.BlockSpec((tm, 4 * wd), lambda i: (i, 0)),
                  pl.BlockSpec((n_prev, 4 * wd), lambda i: (jnp.maximum(i * (tm // n_prev) - 1, 0), 0))]
                 + [full(c) for c in consts],
        out_specs=[row() for _ in range(6)],
        out_shape=[jax.ShapeDtypeStruct((t, wd), dt) for dt in out_dtypes],
        compiler_params=_cparams("parallel"),
    )(rw, rw, *consts)


def _rwkv_chunk_kernel(r_ref, k_ref, v_ref, lw_ref, a_ref, kkp_ref, kap_ref, rkp_ref,
                       rp_ref, y0_ref, bon_ref, p_ref, q_ref):
    c_len = RW_C
    n2 = 2 * c_len
    n_chunks = r_ref.shape[0] // c_len
    r2 = lax.broadcasted_iota(I32, (n2, LANES), 0)
    c2 = lax.broadcasted_iota(I32, (n2, LANES), 1)
    same = (r2 < c_len) == (c2 < HEAD_DIM)
    t_r = jnp.where(r2 < c_len, r2, r2 - c_len)
    t_c = jnp.where(c2 < HEAD_DIM, c2, c2 - HEAD_DIM)
    strict = jnp.logical_and(same, t_r > t_c)
    incl = jnp.logical_and(same, t_r >= t_c)
    eye = jnp.where(r2 == c2, 1.0, 0.0).astype(F32)
    group_ones = jnp.where(same, 1.0, 0.0).astype(BF16)
    lr = lax.broadcasted_iota(I32, (c_len, c_len), 0)
    lc = lax.broadcasted_iota(I32, (c_len, c_len), 1)
    cum_mat = jnp.where(lr >= lc, 1.0, 0.0).astype(BF16)
    kkp, kap, rkp = kkp_ref[...], kap_ref[...], rkp_ref[...]

    def stack(z):
        return jnp.where(same, jnp.concatenate([z, z], axis=0), 0.0)

    def unstack(zs):
        return zs[:c_len] + zs[c_len:]

    cs = range(n_chunks)
    rows = [pl.ds(c * c_len, c_len) for c in cs]
    r = [r_ref[rw, :].astype(F32) for rw in rows]
    kr = [k_ref[rw, :].astype(F32) for rw in rows]
    v = [v_ref[rw, :].astype(F32) for rw in rows]
    lw = [lw_ref[rw, :] for rw in rows]
    a = [a_ref[rw, :].astype(F32) for rw in rows]
    kk = [x * kkp for x in kr]
    cum = [_mm_sum_lhs(cum_mat, x) for x in lw]
    ssq = [_dg((x * x).astype(BF16), group_ones) for x in kk]
    km = [kr[c] * (1.0 + (a[c] - 1.0) * kap) for c in cs]
    bsum = [_dg((r[c] * km[c] * rkp).astype(BF16), group_ones) for c in cs]
    kk = [kk[c] / jnp.maximum(jnp.sqrt(ssq[c]), 1e-12) for c in cs]
    bv = [kk[c] * a[c] for c in cs]
    last = [x[c_len - 1:c_len, :] for x in cum]
    rt = [stack(r[c] * jnp.exp(cum[c])) for c in cs]
    at_b = [stack(-kk[c] * jnp.exp(cum[c] - lw[c])).astype(BF16) for c in cs]
    rt_b = [x.astype(BF16) for x in rt]
    g_inv = [jnp.exp(-x) for x in cum]
    btkt = [jnp.concatenate([stack(bv[c] * g_inv[c]), stack(km[c] * g_inv[c])], axis=0).astype(BF16) for c in cs]
    m_a = [_dg(at_b[c], btkt[c], NT) for c in cs]
    m_r = [_dg(rt_b[c], btkt[c], NT) for c in cs]
    vs_b = [stack(x).astype(BF16) for x in v]
    g_end = [jnp.exp(last[c] - cum[c]) for c in cs]
    bd_b = [stack(bv[c] * g_end[c]).astype(BF16) for c in cs]
    kd_b = [stack(km[c] * g_end[c]).astype(BF16) for c in cs]
    m_ab = [jnp.where(strict, x[:, :n2], 0.0) for x in m_a]
    m_ak = [jnp.where(strict, x[:, n2:], 0.0).astype(BF16) for x in m_a]
    m_rb = [jnp.where(incl, x[:, :n2], 0.0).astype(BF16) for x in m_r]
    m_rk = [jnp.where(incl, x[:, n2:], 0.0).astype(BF16) for x in m_r]
    akv = [_dg(m_ak[c], vs_b[c]) for c in cs]
    rkv = [_dg(m_rk[c], vs_b[c]) for c in cs]
    kdv = [_dg(kd_b[c], vs_b[c], TN) for c in cs]
    tinv = [eye + x for x in m_ab]
    npow = [x.astype(BF16) for x in m_ab]
    for _ in range(5):
        npow = [_dg(x, x).astype(BF16) for x in npow]
        tinv = [tinv[c] + _dg(tinv[c].astype(BF16), npow[c]) for c in cs]
    au_b = [_dg(tinv[c].astype(BF16), jnp.concatenate([at_b[c], akv[c].astype(BF16)], axis=1)).astype(BF16)
            for c in cs]
    ry = [_dg(m_rb[c], au_b[c]) for c in cs]
    pq = [_dg(bd_b[c], au_b[c], TN) for c in cs]
    for c in cs:
        prow = pl.ds(c * n2, n2)
        rp_ref[rows[c], :] = unstack(rt[c] + ry[c][:, :LANES]).astype(rp_ref.dtype)
        y0_ref[rows[c], :] = unstack(ry[c][:, LANES:] + rkv[c])
        p_ref[prow, :] = (eye * jnp.exp(last[c]) + pq[c][:, :LANES]).astype(p_ref.dtype)
        q_ref[prow, :] = pq[c][:, LANES:] + kdv[c]
        bon_ref[rows[c], :] = (bsum[c] * v[c]).astype(bon_ref.dtype)


def _rwkv_chunk(r, k, v, lw, a, k_k, k_a, r_k):
    t = r.shape[0]
    rb = min(RW_C * RW_NCH, t)
    n_pairs = MIX_WIDTH // LANES
    row = lambda: pl.BlockSpec((rb, LANES), lambda i, hp: (i, hp))
    par = lambda: pl.BlockSpec((1, LANES), lambda i, hp: (0, hp))
    mat = lambda: pl.BlockSpec((2 * rb, LANES), lambda i, hp: (i, hp))
    return pl.pallas_call(
        _rwkv_chunk_kernel,
        grid=(t // rb, n_pairs),
        in_specs=[row() for _ in range(5)] + [par() for _ in range(3)],
        out_specs=[row(), row(), row(), mat(), mat()],
        out_shape=[jax.ShapeDtypeStruct((t, MIX_WIDTH), BF16), jax.ShapeDtypeStruct((t, MIX_WIDTH), F32),
                   jax.ShapeDtypeStruct((t, MIX_WIDTH), BF16), jax.ShapeDtypeStruct((2 * t, MIX_WIDTH), BF16),
                   jax.ShapeDtypeStruct((2 * t, MIX_WIDTH), F32)],
        compiler_params=_cparams("parallel", "parallel"),
    )(r, k, v, lw, a, k_k.reshape(1, -1), k_a.reshape(1, -1), r_k.reshape(1, -1))


def _rwkv_state_kernel(rp_ref, y0_ref, p_ref, q_ref, o_ref, st_ref):
    @pl.when(pl.program_id(0) == 0)
    def _():
        st_ref[...] = jnp.zeros_like(st_ref)

    c_len = RW_C
    pairs = range(MIX_WIDTH // LANES)
    lanes = [slice(hp * LANES, (hp + 1) * LANES) for hp in pairs]
    st = [st_ref[hp] for hp in pairs]
    for c in range(rp_ref.shape[0] // c_len):
        rows = slice(c * c_len, (c + 1) * c_len)
        mrows = slice(2 * c * c_len, 2 * (c + 1) * c_len)
        st_b = [x.astype(BF16) for x in st]
        st = [_dg(p_ref[mrows, lanes[hp]], st_b[hp]) + q_ref[mrows, lanes[hp]] for hp in pairs]
        for hp in pairs:
            o_ref[rows, lanes[hp]] = _dg(rp_ref[rows, lanes[hp]], st_b[hp]) + y0_ref[rows, lanes[hp]]
    for hp in pairs:
        st_ref[hp] = st[hp]


def _rwkv_state(rp, y0, p, q):
    t = rp.shape[0]
    rb = min(RW_C * RW_NCS, t)
    wd = MIX_WIDTH
    row = lambda: pl.BlockSpec((rb, wd), lambda i: (i, 0))
    mat = lambda: pl.BlockSpec((2 * rb, wd), lambda i: (i, 0))
    return pl.pallas_call(
        _rwkv_state_kernel,
        grid=(t // rb,),
        in_specs=[row(), row(), mat(), mat()],
        out_specs=row(),
        out_shape=jax.ShapeDtypeStruct((t, wd), F32),
        scratch_shapes=[pltpu.VMEM((wd // LANES, LANES, LANES), F32)],
        compiler_params=_cparams("arbitrary"),
    )(rp, y0, p, q)


def _mix_kernel(ya_ref, y_ref, bon_ref, g_ref, gate_ref, x_ref, gnw_ref, gnb_ref, wa_ref, wb_ref, wo_ref,
                lnw_ref, lnb_ref, x1_ref, x1b_ref):
    r2 = lax.broadcasted_iota(I32, (LANES, LANES), 0)
    c2 = lax.broadcasted_iota(I32, (LANES, LANES), 1)
    group_ones = jnp.where((r2 < HEAD_DIM) == (c2 < HEAD_DIM), 1.0, 0.0).astype(BF16)
    inv_n = 1.0 / HEAD_DIM
    pieces = []
    for hp in range(MIX_WIDTH // LANES):
        ls = slice(hp * LANES, (hp + 1) * LANES)
        y = y_ref[:, ls]
        d = y - _dg(y.astype(BF16), group_ones) * inv_n
        var = _dg((d * d).astype(BF16), group_ones) * inv_n
        yn = d * lax.rsqrt(var + GN_EPS) * gnw_ref[:, ls] + gnb_ref[:, ls]
        pieces.append(((yn + bon_ref[:, ls]) * g_ref[:, ls]).astype(BF16))
    ya = _dg(ya_ref[...], wa_ref[...])
    yb = _dg(jnp.concatenate(pieces, axis=1), wb_ref[...])
    ga = _sigmoid(gate_ref[:, :D_MODEL].astype(F32))
    gb = _sigmoid(gate_ref[:, D_MODEL:].astype(F32))
    mixed = _dg((ga * ya + gb * yb).astype(BF16), wo_ref[...])
    x1 = _layer_norm(DEEPNORM_ALPHA * x_ref[...] + mixed, lnw_ref[...], lnb_ref[...])
    x1_ref[...] = x1
    x1b_ref[...] = x1.astype(BF16)


def _const_spec(x):
    return pl.BlockSpec(x.shape, lambda *_: (0,) * x.ndim, pipeline_mode=pl.Buffered(1))


def _mix(ya, y, bon, g, gates, x, gn_w, gn_b, w_up_a, w_up_b, w_o, ln_w, ln_b):
    t = x.shape[0]
    tm = min(MIX_TM, t)
    consts = (gn_w.reshape(1, -1), gn_b.reshape(1, -1), w_up_a, w_up_b, w_o, ln_w.reshape(1, -1), ln_b.reshape(1, -1))
    row = lambda w: pl.BlockSpec((tm, w), lambda i: (i, 0))
    return pl.pallas_call(
        _mix_kernel,
        grid=(t // tm,),
        in_specs=[row(MIX_WIDTH), row(MIX_WIDTH), row(MIX_WIDTH), row(MIX_WIDTH), row(2 * D_MODEL), row(D_MODEL)]
                 + [_const_spec(c) for c in consts],
        out_specs=[pl.BlockSpec((tm, D_MODEL), lambda i: (i, 0))] * 2,
        out_shape=[jax.ShapeDtypeStruct((t, D_MODEL), F32), jax.ShapeDtypeStruct((t, D_MODEL), BF16)],
        compiler_params=_cparams("parallel"),
    )(ya, y, bon, g, gates, x, *consts)


def _topk_rows(s, ids, k):
    big = jnp.asarray(1e9, F32)
    vals, idxs = [], []
    for _ in range(k):
        m = jnp.max(s, axis=0, keepdims=True)
        ix = jnp.min(jnp.where(s == m, ids, big), axis=0, keepdims=True)
        vals.append(m)
        idxs.append(ix)
        s = jnp.where(ids == ix, -jnp.inf, s)
    return jnp.concatenate(vals, axis=0), jnp.concatenate(idxs, axis=0)


def _route_kernel(x_ref, wq_ref, keys_ref, e0_ref, e1_ref, gate_ref):
    tm = x_ref.shape[0]
    topk = PEER_TOPK
    q = _dg(x_ref[...], wq_ref[...])
    qb = q.astype(BF16)
    half = N_KEYS
    key_ids = lax.broadcasted_iota(I32, (N_KEYS, tm), 0).astype(F32)
    sub8 = lax.broadcasted_iota(I32, (8, tm), 0).astype(F32)
    sub16 = lax.broadcasted_iota(I32, (topk, tm), 0).astype(F32)
    cand_ids = jnp.concatenate([sub16] + [a * topk + sub8 for a in range(1, 8)] + [(sub8 + 8.0) * topk], axis=0)
    e0s, e1s, gates = [], [], []
    for h in range(PEER_HEADS):
        tops = []
        for c in range(2):
            qs = qb[:, (2 * h + c) * half:(2 * h + c + 1) * half]
            s = _dg(keys_ref[c], qs, NT)
            tops.append(_topk_rows(s, key_ids, topk))
        (s0, i0), (s1, i1) = tops
        cand = jnp.concatenate([s0[0:1, :] + s1] + [s0[a:a + 1, :] + s1[:8, :] for a in range(1, 8)]
                               + [s0[8:, :] + s1[0:1, :]], axis=0)
        best, pos = _topk_rows(cand, cand_ids, topk)
        a_sel = jnp.floor(pos * (1.0 / topk))
        b_sel = pos - a_sel * topk
        e0 = jnp.zeros_like(pos)
        e1 = jnp.zeros_like(pos)
        for j in range(topk):
            e0 = jnp.where(a_sel == j, i0[j:j + 1, :], e0)
            e1 = jnp.where(b_sel == j, i1[j:j + 1, :], e1)
        ex = jnp.exp(best - best[0:1, :])
        gates.append(ex / jnp.sum(ex, axis=0, keepdims=True))
        e0s.append(e0)
        e1s.append(e1)
    e0_ref[...] = jnp.concatenate(e0s, axis=0).T.astype(I32)
    e1_ref[...] = jnp.concatenate(e1s, axis=0).T.astype(I32)
    gate_ref[...] = jnp.concatenate(gates, axis=0).T


def _route(x1b, w_q, sub_keys):
    t = x1b.shape[0]
    tm = min(ROUTE_TM, t)
    out = lambda: pl.BlockSpec((tm, N_SEL), lambda i: (i, 0))
    return pl.pallas_call(
        _route_kernel,
        grid=(t // tm,),
        in_specs=[pl.BlockSpec((tm, D_MODEL), lambda i: (i, 0)), _const_spec(w_q), _const_spec(sub_keys)],
        out_specs=[out(), out(), out()],
        out_shape=[jax.ShapeDtypeStruct((t, N_SEL), I32), jax.ShapeDtypeStruct((t, N_SEL), I32),
                   jax.ShapeDtypeStruct((t, N_SEL), F32)],
        compiler_params=_cparams("parallel"),
    )(x1b, w_q, sub_keys)


def _gelu_exact(x):
    return 0.5 * x * (1.0 + lax.erf(x * (2.0 ** -0.5)))


def _peer_score_kernel(x_ref, u_ref, e0_ref, e1_ref, gate_ref, o_ref, acc_ref):
    j = pl.program_id(1)

    @pl.when(j == 0)
    def _():
        acc_ref[...] = jnp.zeros_like(acc_ref)

    x = x_ref[...]
    e0, e1 = e0_ref[...], e1_ref[...]
    blocks = u_ref.shape[0] // N_KEYS
    per_dot = PH_SUB // N_KEYS
    acc = acc_ref[...]
    for s in range(u_ref.shape[0] // PH_SUB):
        h = _dg(x, u_ref[s * PH_SUB:(s + 1) * PH_SUB, :].astype(BF16), NT)
        for b in range(per_dot):
            picked = jnp.take_along_axis(h[:, b * N_KEYS:(b + 1) * N_KEYS], e1, axis=1)
            acc = acc + jnp.where(e0 == j * blocks + s * per_dot + b, picked, 0.0)
    acc_ref[...] = acc

    @pl.when(j == pl.num_programs(1) - 1)
    def _():
        o_ref[...] = _gelu_exact(acc_ref[...]) * gate_ref[...]


def _peer_score(x1b, u_bf16, e0, e1, gate):
    t = x1b.shape[0]
    tm = min(PH_TM, t)
    tok = lambda: pl.BlockSpec((tm, N_SEL), lambda i, j: (i, 0))
    return pl.pallas_call(
        _peer_score_kernel,
        grid=(t // tm, N_EXPERTS // PH_TN),
        in_specs=[pl.BlockSpec((tm, D_MODEL), lambda i, j: (i, 0)),
                  pl.BlockSpec((PH_TN, D_MODEL), lambda i, j: (j, 0)), tok(), tok(), tok()],
        out_specs=tok(),
        out_shape=jax.ShapeDtypeStruct((t, N_SEL), F32),
        scratch_shapes=[pltpu.VMEM((tm, N_SEL), F32)],
        compiler_params=_cparams("parallel", "arbitrary"),
    )(x1b, u_bf16, e0, e1, gate)


def _peer_gate_kernel(e0_ref, e1_ref, act_ref, o_ref):
    sub = lax.broadcasted_iota(I32, (N_KEYS, N_SEL), 0)

    def body(grp, carry):
        mats = []
        for u in range(PG_GRP):
            row = pl.ds(grp * PG_GRP + u, 1)
            left = jnp.where(e0_ref[row, :] == sub, act_ref[row, :], 0.0).astype(BF16)
            right = jnp.where(e1_ref[row, :] == sub, 1.0, 0.0).astype(BF16)
            mats.append(_dg(left, right, NT))
        o_ref[grp] = jnp.swapaxes(jnp.stack(mats, axis=0), 0, 1).astype(o_ref.dtype)
        return carry

    lax.fori_loop(0, o_ref.shape[0], body, 0)


def _peer_gate(e0, e1, act):
    t = e0.shape[0]
    tt = min(PG_TT, t)
    tok = lambda: pl.BlockSpec((tt, N_SEL), lambda i: (i, 0))
    return pl.pallas_call(
        _peer_gate_kernel,
        grid=(t // tt,),
        in_specs=[tok(), tok(), tok()],
        out_specs=pl.BlockSpec((tt // PG_GRP, N_KEYS, PG_GRP, N_KEYS), lambda i: (i, 0, 0, 0)),
        out_shape=jax.ShapeDtypeStruct((t // PG_GRP, N_KEYS, PG_GRP, N_KEYS), BF16),
        compiler_params=_cparams("parallel"),
    )(e0, e1, act)


def _peer_value_kernel(g_ref, v_ref, o_ref):
    @pl.when(pl.program_id(1) == 0)
    def _():
        o_ref[...] = jnp.zeros_like(o_ref)

    tm = o_ref.shape[0]
    lhs = jnp.concatenate([g_ref[:, e, :, :].reshape(tm, N_KEYS) for e in range(g_ref.shape[1])], axis=1)
    o_ref[...] += _dg(lhs, v_ref[...])


def _peer_value(g4, v_bf16):
    t = g4.shape[0] * PG_GRP
    tm = min(PV_TM, t)
    return pl.pallas_call(
        _peer_value_kernel,
        grid=(t // tm, N_KEYS // PV_NE0),
        in_specs=[pl.BlockSpec((tm // PG_GRP, PV_NE0, PG_GRP, N_KEYS), lambda i, k: (i, k, 0, 0)),
                  pl.BlockSpec((PV_NE0 * N_KEYS, D_MODEL), lambda i, k: (k, 0))],
        out_specs=pl.BlockSpec((tm, D_MODEL), lambda i, k: (i, 0)),
        out_shape=jax.ShapeDtypeStruct((t, D_MODEL), F32),
        compiler_params=_cparams("parallel", "arbitrary"),
    )(g4, v_bf16)


def _final_kernel(x1_ref, x1b_ref, ffn_ref, p_ref, wg_ref, wp_ref, lnw_ref, lnb_ref, o_ref):
    ple = _sigmoid(_dg(x1b_ref[...], wg_ref[...])) * _dg(p_ref[...].astype(BF16), wp_ref[...])
    o_ref[...] = _layer_norm(DEEPNORM_ALPHA * x1_ref[...] + ffn_ref[...] + ple, lnw_ref[...], lnb_ref[...])


def _final(x1, x1b, ffn, p, w_gate, w_proj, ln_w, ln_b):
    t = x1.shape[0]
    tm = min(FIN_TM, t)
    consts = (w_gate, w_proj, ln_w.reshape(1, -1), ln_b.reshape(1, -1))
    row = lambda w: pl.BlockSpec((tm, w), lambda i: (i, 0))
    return pl.pallas_call(
        _final_kernel,
        grid=(t // tm,),
        in_specs=[row(D_MODEL), row(D_MODEL), row(D_MODEL), row(p.shape[1])] + [_const_spec(c) for c in consts],
        out_specs=row(D_MODEL),
        out_shape=jax.ShapeDtypeStruct((t, D_MODEL), F32),
        compiler_params=_cparams("parallel"),
    )(x1, x1b, ffn, p, *consts)


def _layer(x, p, w_in, rwkv_mu, rwkv_w0, rwkv_w_lora1, rwkv_w_lora2, rwkv_a0, rwkv_a_lora1,
           rwkv_a_lora2, rwkv_g_lora1, rwkv_g_lora2, rwkv_k_k, rwkv_k_a, rwkv_r_k, rwkv_lnx_w,
           rwkv_lnx_b, w_up_a, w_up_b, w_o, ln1_w, ln1_b, peer_w_q, peer_sub_keys, peer_u, peer_v,
           ple_w_gate, ple_w_proj, ln2_w, ln2_b):
    bf = lambda w: w.astype(BF16)
    n_sb = 3 * MIX_WIDTH
    n_rw = 4 * MIX_WIDTH
    qkv = _proj(x, w_in, 0, n_sb, BF16)
    rw = _proj(x, w_in, n_sb, n_rw, BF16)
    gates = _proj(x, w_in, n_sb + n_rw, 2 * D_MODEL, BF16)
    y_a = _sb_attention(qkv)
    r, k, v, lw, a, g = _rwkv_pre(rw, rwkv_mu, rwkv_w0, rwkv_w_lora1, rwkv_w_lora2, rwkv_a0,
                                  rwkv_a_lora1, rwkv_a_lora2, rwkv_g_lora1, rwkv_g_lora2)
    rp, y0, bon, pm, qm = _rwkv_chunk(r, k, v, lw, a, rwkv_k_k, rwkv_k_a, rwkv_r_k.reshape(-1))
    y_b = _rwkv_state(rp, y0, pm, qm)
    x1, x1b = _mix(y_a, y_b, bon, g, gates, x, rwkv_lnx_w, rwkv_lnx_b, bf(w_up_a), bf(w_up_b), bf(w_o),
                   ln1_w, ln1_b)
    e0, e1, gate = _route(x1b, bf(peer_w_q), bf(peer_sub_keys))
    act = _peer_score(x1b, peer_u, e0, e1, gate)
    ffn = _peer_value(_peer_gate(e0, e1, act), bf(peer_v))
    return _final(x1, x1b, ffn, p, bf(ple_w_gate), bf(ple_w_proj), ln2_w, ln2_b)


def kernel(x, p, w_in, rwkv_mu, rwkv_w0, rwkv_w_lora1, rwkv_w_lora2, rwkv_a0, rwkv_a_lora1, rwkv_a_lora2, rwkv_g_lora1, rwkv_g_lora2, rwkv_k_k, rwkv_k_a, rwkv_r_k, rwkv_lnx_w, rwkv_lnx_b, w_up_a, w_up_b, w_o, ln1_w, ln1_b, peer_w_q, peer_sub_keys, peer_u, peer_v, ple_w_gate, ple_w_proj, ln2_w, ln2_b):
    bsz, t, d = x.shape
    depth = w_in.shape[0]
    xs = x.reshape(bsz * t, d)
    assert bsz == 1, "token shift / attention / scan treat the flattened rows as one sequence"
    for i in range(depth):
        xs = _layer(xs, p[i].reshape(bsz * t, -1), w_in[i], rwkv_mu[i], rwkv_w0[i], rwkv_w_lora1[i],
                    rwkv_w_lora2[i], rwkv_a0[i], rwkv_a_lora1[i], rwkv_a_lora2[i], rwkv_g_lora1[i],
                    rwkv_g_lora2[i], rwkv_k_k[i], rwkv_k_a[i], rwkv_r_k[i], rwkv_lnx_w[i], rwkv_lnx_b[i],
                    w_up_a[i], w_up_b[i], w_o[i], ln1_w[i], ln1_b[i], peer_w_q[i], peer_sub_keys[i],
                    peer_u[i], peer_v[i], ple_w_gate[i], ple_w_proj[i], ln2_w[i], ln2_b[i])
    return xs.reshape(bsz, t, d)
```

```python
import functools

import jax
import jax.numpy as jnp
from jax import lax
from jax.experimental import pallas as pl
from jax.experimental.pallas import tpu as pltpu

F32 = jnp.float32
BF16 = jnp.bfloat16
I32 = jnp.int32

D_MODEL = 2048
HEAD_DIM = 64
N_HEADS = 16
MIX_WIDTH = N_HEADS * HEAD_DIM
PEER_HEADS = 8
PEER_TOPK = 16
N_KEYS = 128
N_EXPERTS = N_KEYS * N_KEYS
N_SEL = PEER_HEADS * PEER_TOPK
GN_EPS = 64e-5
LN_EPS = 1e-5
DEEPNORM_ALPHA = 2.0 ** 0.25

LANES = 128
VMEM_LIMIT = 56 * 1024 * 1024

MM_TM, MM_TN = 1024, 1024
SB_BLK = 128
SB_SUBS = 4
SB_WIN = 2 * LANES
SB_DONE = -88.0
RW_TM = 256
RW_C = 64
RW_NCH = 8
RW_NCS = 8
MIX_TM = 256
ROUTE_TM = 512
PH_TM, PH_TN = 1024, 2048
PH_SUB = 256
PG_TT = 256
PG_GRP = 16
PV_TM, PV_NE0 = 1024, 16
FIN_TM = 512

assert 2 * RW_C == LANES and RW_C == HEAD_DIM

NN = (((1,), (0,)), ((), ()))
NT = (((1,), (1,)), ((), ()))
TN = (((0,), (0,)), ((), ()))


def _cparams(*sem):
    return pltpu.CompilerParams(dimension_semantics=tuple(sem), vmem_limit_bytes=VMEM_LIMIT)


def _dg(a, b, dims=NN):
    return lax.dot_general(a, b, dims, preferred_element_type=F32)


def _split2(x):
    hi = x.astype(BF16)
    lo = (x - hi.astype(F32)).astype(BF16)
    return hi, lo


def _mm1(a, b, dims=NN):
    return _dg(a.astype(BF16), b.astype(BF16), dims)


def _mm_sum_lhs(m_bf16, x):
    hi, lo = _split2(x)
    return _dg(m_bf16, hi) + _dg(m_bf16, lo)


def _softplus(z):
    return jnp.maximum(z, 0.0) + jnp.log(1.0 + jnp.exp(-jnp.abs(z)))


def _sigmoid(z):
    return 0.5 * jnp.tanh(0.5 * z) + 0.5


def _layer_norm(x, g, b):
    mu = jnp.mean(x, axis=-1, keepdims=True)
    d = x - mu
    var = jnp.mean(d * d, axis=-1, keepdims=True)
    return d * lax.rsqrt(var + LN_EPS) * g + b


def _proj_kernel(a_ref, b_ref, o_ref, bw_ref):
    @pl.when(pl.program_id(1) == 0)
    def _():
        bw_ref[...] = b_ref[...].astype(BF16)

    o_ref[...] = _dg(a_ref[...].astype(BF16), bw_ref[...]).astype(o_ref.dtype)


def _proj(a, b, col0, n, out_dtype):
    m, k = a.shape
    tm, tn = min(MM_TM, m), MM_TN
    assert col0 % tn == 0 and n % tn == 0
    off = col0 // tn
    return pl.pallas_call(
        _proj_kernel,
        grid=(n // tn, m // tm),
        in_specs=[pl.BlockSpec((tm, k), lambda j, i: (i, 0)),
                  pl.BlockSpec((k, tn), lambda j, i: (0, j + off))],
        out_specs=pl.BlockSpec((tm, tn), lambda j, i: (i, j)),
        out_shape=jax.ShapeDtypeStruct((m, n), out_dtype),
        scratch_shapes=[pltpu.VMEM((k, tn), BF16)],
        compiler_params=_cparams("parallel", "arbitrary"),
    )(a, b)


def _sb_kernel(q_ref, k_ref, v_ref, o_ref, acc_ref, carry_ref):
    i = pl.program_id(1)
    blk = SB_BLK
    n_sub = q_ref.shape[0] // blk
    win = SB_WIN
    lane = lax.broadcasted_iota(I32, (1, LANES), 1)
    row = lax.broadcasted_iota(I32, (blk, win), 0)
    col = lax.broadcasted_iota(I32, (blk, win), 1)
    r_io = lax.broadcasted_iota(I32, (win, win), 0)
    c_io = lax.broadcasted_iota(I32, (win, win), 1)
    later_mat = jnp.where(r_io > c_io, 1.0, 0.0).astype(BF16)
    scale = jnp.asarray(HEAD_DIM ** -0.5, BF16)
    chains = [(s, h) for s in range(n_sub) for h in range(2)]
    qh = []
    for s, h in chains:
        q = q_ref[s * blk:(s + 1) * blk, :]
        qh.append(jnp.where((lane < HEAD_DIM) == (h == 0), q, jnp.zeros_like(q)) * scale)
    first_end = [(i * n_sub + s + 1) * blk for s in range(n_sub)]
    acc_ref[...] = jnp.zeros_like(acc_ref)
    carry_ref[...] = jnp.zeros_like(carry_ref)

    def step(it):
        kwin, vwin, valid = [], [], []
        for s in range(n_sub):
            end = first_end[s] - it * win
            start = pl.multiple_of(jnp.maximum(end - win, 0), blk)
            kwin.append(k_ref[pl.ds(start, win), :])
            vwin.append(v_ref[pl.ds(start, win), :])
            valid.append((start + col) < jnp.minimum(first_end[s] - blk + row, end))
        z = [_dg(qh[c], kwin[s], NT) for c, (s, h) in enumerate(chains)]
        sp = [_softplus(x) for x in z]
        log_keep = [jnp.where(valid[s], -sp[c], 0.0).astype(BF16) for c, (s, h) in enumerate(chains)]
        carry = [carry_ref[c] for c in range(len(chains))]
        log_later = [_dg(log_keep[c], later_mat) + jnp.concatenate([carry[c]] * (win // LANES), axis=1)
                     for c in range(len(chains))]
        w = [jnp.where(valid[s], jnp.exp(z[c] - sp[c] + log_later[c]), 0.0).astype(BF16)
             for c, (s, h) in enumerate(chains)]
        pv = [_dg(w[c], vwin[s]) for c, (s, h) in enumerate(chains)]
        for c in range(len(chains)):
            acc_ref[c] += pv[c]
            carry_ref[c] = jnp.broadcast_to(log_later[c][:, 0:1] + log_keep[c][:, 0:1], (blk, LANES))
        return it + 1

    def cond(it):
        alive = None
        for s in range(n_sub):
            more = jnp.logical_and(first_end[s] - it * win > 0, jnp.max(carry_ref[2 * s:2 * s + 2]) > SB_DONE)
            alive = more if alive is None else jnp.logical_or(alive, more)
        return alive

    lax.while_loop(cond, step, step(0))
    for s in range(n_sub):
        o_ref[s * blk:(s + 1) * blk, :] = jnp.where(lane < HEAD_DIM, acc_ref[2 * s], acc_ref[2 * s + 1]).astype(o_ref.dtype)


def _sb_attention(qkv):
    t = qkv.shape[0]
    blk = SB_BLK * SB_SUBS
    n_pairs = MIX_WIDTH // LANES
    return pl.pallas_call(
        _sb_kernel,
        grid=(n_pairs, t // blk),
        in_specs=[pl.BlockSpec((blk, LANES), lambda hp, i: (i, hp)),
                  pl.BlockSpec((t, LANES), lambda hp, i: (0, n_pairs + hp)),
                  pl.BlockSpec((t, LANES), lambda hp, i: (0, 2 * n_pairs + hp))],
        out_specs=pl.BlockSpec((blk, LANES), lambda hp, i: (i, hp)),
        out_shape=jax.ShapeDtypeStruct((t, MIX_WIDTH), BF16),
        scratch_shapes=[pltpu.VMEM((2 * SB_SUBS, SB_BLK, LANES), F32), pltpu.VMEM((2 * SB_SUBS, SB_BLK, LANES), F32)],
        compiler_params=_cparams("parallel", "parallel"),
    )(qkv, qkv, qkv)


def _rwkv_pre_kernel(cur_ref, prev_ref, mu_ref, w0_ref, wl1_ref, wl2_ref, a0_ref, al1_ref, al2_ref,
                     gl1_ref, gl2_ref, r_ref, k_ref, v_ref, lw_ref, a_ref, g_ref):
    i = pl.program_id(0)
    tm = cur_ref.shape[0]
    wd = MIX_WIDTH
    first_row = lax.broadcasted_iota(I32, (tm, wd), 0) == 0
    keep_prev = jnp.where(i == 0, 0.0, 1.0)

    n_prev = prev_ref.shape[0]

    def shifted(col):
        z = cur_ref[:, col * wd:(col + 1) * wd].astype(F32)
        last = prev_ref[n_prev - 1:n_prev, col * wd:(col + 1) * wd].astype(F32) * keep_prev
        prev = jnp.where(first_row, last, pltpu.roll(z, 1, 0))
        return z, prev - z

    z, d = shifted(0)
    r_ref[...] = (z + d * mu_ref[0:1, :]).astype(r_ref.dtype)
    z, d = shifted(1)
    k_ref[...] = (z + d * mu_ref[1:2, :]).astype(k_ref.dtype)
    z, d = shifted(2)
    v_ref[...] = (z + d * mu_ref[2:3, :]).astype(v_ref.dtype)
    z, d = shifted(3)
    xw = z + d * mu_ref[3:4, :]
    xa = z + d * mu_ref[4:5, :]
    xg = z + d * mu_ref[5:6, :]
    w = w0_ref[...] + _mm1(jnp.tanh(_mm1(xw, wl1_ref[...])), wl2_ref[...])
    w = -_softplus(-w) - 0.5
    lw_ref[...] = -jnp.exp(w)
    a_ref[...] = _sigmoid(a0_ref[...] + _mm1(_mm1(xa, al1_ref[...]), al2_ref[...])).astype(a_ref.dtype)
    g_ref[...] = _mm1(_sigmoid(_mm1(xg, gl1_ref[...])), gl2_ref[...]).astype(g_ref.dtype)


def _pad_to(x, axis, size):
    pad = [(0, 0)] * x.ndim
    pad[axis] = (0, size - x.shape[axis])
    return jnp.pad(x, pad)


def _rwkv_pre(rw, mu, w0, wl1, wl2, a0, al1, al2, gl1, gl2):
    t = rw.shape[0]
    tm = min(RW_TM, t)
    wd = MIX_WIDTH
    lo = LANES

    def lora_pair(l1, l2):
        n = -(-l1.shape[1] // lo) * lo
        return _pad_to(l1, 1, n).astype(BF16), _pad_to(l2, 0, n).astype(BF16)

    wl1, wl2 = lora_pair(wl1, wl2)
    al1, al2 = lora_pair(al1, al2)
    gl1, gl2 = lora_pair(gl1, gl2)
    full = lambda x: pl.BlockSpec(x.shape, lambda i: (0,) * x.ndim)
    row = lambda: pl.BlockSpec((tm, wd), lambda i: (i, 0))
    consts = (mu, w0.reshape(1, wd), wl1, wl2, a0.reshape(1, wd), al1, al2, gl1, gl2)
    n_prev = 32 // rw.dtype.itemsize
    out_dtypes = (BF16, BF16, BF16, F32, BF16, BF16)
    return pl.pallas_call(
        _rwkv_pre_kernel,
        grid=(t // tm,),
        in_specs=[pl.BlockSpec((tm, 4 * wd), lambda i: (i, 0)),
                  pl.BlockSpec((n_prev, 4 * wd), lambda i: (jnp.maximum(i * (tm // n_prev) - 1, 0), 0))]
                 + [full(c) for c in consts],
        out_specs=[row() for _ in range(6)],
        out_shape=[jax.ShapeDtypeStruct((t, wd), dt) for dt in out_dtypes],
        compiler_params=_cparams("parallel"),
    )(rw, rw, *consts)


def _rwkv_chunk_kernel(r_ref, k_ref, v_ref, lw_ref, a_ref, kkp_ref, kap_ref, rkp_ref,
                       rp_ref, y0_ref, bon_ref, p_ref, q_ref):
    c_len = RW_C
    n2 = 2 * c_len
    n_chunks = r_ref.shape[0] // c_len
    r2 = lax.broadcasted_iota(I32, (n2, LANES), 0)
    c2 = lax.broadcasted_iota(I32, (n2, LANES), 1)
    same = (r2 < c_len) == (c2 < HEAD_DIM)
    t_r = jnp.where(r2 < c_len, r2, r2 - c_len)
    t_c = jnp.where(c2 < HEAD_DIM, c2, c2 - HEAD_DIM)
    strict = jnp.logical_and(same, t_r > t_c)
    incl = jnp.logical_and(same, t_r >= t_c)
    eye = jnp.where(r2 == c2, 1.0, 0.0).astype(F32)
    group_ones = jnp.where(same, 1.0, 0.0).astype(BF16)
    lr = lax.broadcasted_iota(I32, (c_len, c_len), 0)
    lc = lax.broadcasted_iota(I32, (c_len, c_len), 1)
    cum_mat = jnp.where(lr >= lc, 1.0, 0.0).astype(BF16)
    kkp, kap, rkp = kkp_ref[...], kap_ref[...], rkp_ref[...]

    def stack(z):
        return jnp.where(same, jnp.concatenate([z, z], axis=0), 0.0)

    def unstack(zs):
        return zs[:c_len] + zs[c_len:]

    cs = range(n_chunks)
    rows = [pl.ds(c * c_len, c_len) for c in cs]
    r = [r_ref[rw, :].astype(F32) for rw in rows]
    kr = [k_ref[rw, :].astype(F32) for rw in rows]
    v = [v_ref[rw, :].astype(F32) for rw in rows]
    lw = [lw_ref[rw, :] for rw in rows]
    a = [a_ref[rw, :].astype(F32) for rw in rows]
    kk = [x * kkp for x in kr]
    cum = [_mm_sum_lhs(cum_mat, x) for x in lw]
    ssq = [_dg((x * x).astype(BF16), group_ones) for x in kk]
    km = [kr[c] * (1.0 + (a[c] - 1.0) * kap) for c in cs]
    bsum = [_dg((r[c] * km[c] * rkp).astype(BF16), group_ones) for c in cs]
    kk = [kk[c] / jnp.maximum(jnp.sqrt(ssq[c]), 1e-12) for c in cs]
    bv = [kk[c] * a[c] for c in cs]
    last = [x[c_len - 1:c_len, :] for x in cum]
    rt = [stack(r[c] * jnp.exp(cum[c])) for c in cs]
    at_b = [stack(-kk[c] * jnp.exp(cum[c] - lw[c])).astype(BF16) for c in cs]
    rt_b = [x.astype(BF16) for x in rt]
    g_inv = [jnp.exp(-x) for x in cum]
    btkt = [jnp.concatenate([stack(bv[c] * g_inv[c]), stack(km[c] * g_inv[c])], axis=0).astype(BF16) for c in cs]
    m_a = [_dg(at_b[c], btkt[c], NT) for c in cs]
    m_r = [_dg(rt_b[c], btkt[c], NT) for c in cs]
    vs_b = [stack(x).astype(BF16) for x in v]
    g_end = [jnp.exp(last[c] - cum[c]) for c in cs]
    bd_b = [stack(bv[c] * g_end[c]).astype(BF16) for c in cs]
    kd_b = [stack(km[c] * g_end[c]).astype(BF16) for c in cs]
    m_ab = [jnp.where(strict, x[:, :n2], 0.0) for x in m_a]
    m_ak = [jnp.where(strict, x[:, n2:], 0.0).astype(BF16) for x in m_a]
    m_rb = [jnp.where(incl, x[:, :n2], 0.0).astype(BF16) for x in m_r]
    m_rk = [jnp.where(incl, x[:, n2:], 0.0).astype(BF16) for x in m_r]
    akv = [_dg(m_ak[c], vs_b[c]) for c in cs]
    rkv = [_dg(m_rk[c], vs_b[c]) for c in cs]
    kdv = [_dg(kd_b[c], vs_b[c], TN) for c in cs]
    tinv = [eye + x for x in m_ab]
    npow = [x.astype(BF16) for x in m_ab]
    for _ in range(5):
        npow = [_dg(x, x).astype(BF16) for x in npow]
        tinv = [tinv[c] + _dg(tinv[c].astype(BF16), npow[c]) for c in cs]
    au_b = [_dg(tinv[c].astype(BF16), jnp.concatenate([at_b[c], akv[c].astype(BF16)], axis=1)).astype(BF16)
            for c in cs]
    ry = [_dg(m_rb[c], au_b[c]) for c in cs]
    pq = [_dg(bd_b[c], au_b[c], TN) for c in cs]
    for c in cs:
        prow = pl.ds(c * n2, n2)
        rp_ref[rows[c], :] = unstack(rt[c] + ry[c][:, :LANES]).astype(rp_ref.dtype)
        y0_ref[rows[c], :] = unstack(ry[c][:, LANES:] + rkv[c])
        p_ref[prow, :] = (eye * jnp.exp(last[c]) + pq[c][:, :LANES]).astype(p_ref.dtype)
        q_ref[prow, :] = pq[c][:, LANES:] + kdv[c]
        bon_ref[rows[c], :] = (bsum[c] * v[c]).astype(bon_ref.dtype)


def _rwkv_chunk(r, k, v, lw, a, k_k, k_a, r_k):
    t = r.shape[0]
    rb = min(RW_C * RW_NCH, t)
    n_pairs = MIX_WIDTH // LANES
    row = lambda: pl.BlockSpec((rb, LANES), lambda i, hp: (i, hp))
    par = lambda: pl.BlockSpec((1, LANES), lambda i, hp: (0, hp))
    mat = lambda: pl.BlockSpec((2 * rb, LANES), lambda i, hp: (i, hp))
    return pl.pallas_call(
        _rwkv_chunk_kernel,
        grid=(t // rb, n_pairs),
        in_specs=[row() for _ in range(5)] + [par() for _ in range(3)],
        out_specs=[row(), row(), row(), mat(), mat()],
        out_shape=[jax.ShapeDtypeStruct((t, MIX_WIDTH), BF16), jax.ShapeDtypeStruct((t, MIX_WIDTH), F32),
                   jax.ShapeDtypeStruct((t, MIX_WIDTH), BF16), jax.ShapeDtypeStruct((2 * t, MIX_WIDTH), BF16),
                   jax.ShapeDtypeStruct((2 * t, MIX_WIDTH), F32)],
        compiler_params=_cparams("parallel", "parallel"),
    )(r, k, v, lw, a, k_k.reshape(1, -1), k_a.reshape(1, -1), r_k.reshape(1, -1))


def _rwkv_state_kernel(rp_ref, y0_ref, p_ref, q_ref, o_ref, st_ref):
    @pl.when(pl.program_id(0) == 0)
    def _():
        st_ref[...] = jnp.zeros_like(st_ref)

    c_len = RW_C
    pairs = range(MIX_WIDTH // LANES)
    lanes = [slice(hp * LANES, (hp + 1) * LANES) for hp in pairs]
    st = [st_ref[hp] for hp in pairs]
    for c in range(rp_ref.shape[0] // c_len):
        rows = slice(c * c_len, (c + 1) * c_len)
        mrows = slice(2 * c * c_len, 2 * (c + 1) * c_len)
        st_b = [x.astype(BF16) for x in st]
        st = [_dg(p_ref[mrows, lanes[hp]], st_b[hp]) + q_ref[mrows, lanes[hp]] for hp in pairs]
        for hp in pairs:
            o_ref[rows, lanes[hp]] = _dg(rp_ref[rows, lanes[hp]], st_b[hp]) + y0_ref[rows, lanes[hp]]
    for hp in pairs:
        st_ref[hp] = st[hp]


def _rwkv_state(rp, y0, p, q):
    t = rp.shape[0]
    rb = min(RW_C * RW_NCS, t)
    wd = MIX_WIDTH
    row = lambda: pl.BlockSpec((rb, wd), lambda i: (i, 0))
    mat = lambda: pl.BlockSpec((2 * rb, wd), lambda i: (i, 0))
    return pl.pallas_call(
        _rwkv_state_kernel,
        grid=(t // rb,),
        in_specs=[row(), row(), mat(), mat()],
        out_specs=row(),
        out_shape=jax.ShapeDtypeStruct((t, wd), F32),
        scratch_shapes=[pltpu.VMEM((wd // LANES, LANES, LANES), F32)],
        compiler_params=_cparams("arbitrary"),
    )(rp, y0, p, q)


def _mix_kernel(ya_ref, y_ref, bon_ref, g_ref, gate_ref, x_ref, gnw_ref, gnb_ref, wa_ref, wb_ref, wo_ref,
                lnw_ref, lnb_ref, x1_ref, x1b_ref):
    r2 = lax.broadcasted_iota(I32, (LANES, LANES), 0)
    c2 = lax.broadcasted_iota(I32, (LANES, LANES), 1)
    group_ones = jnp.where((r2 < HEAD_DIM) == (c2 < HEAD_DIM), 1.0, 0.0).astype(BF16)
    inv_n = 1.0 / HEAD_DIM
    pieces = []
    for hp in range(MIX_WIDTH // LANES):
        ls = slice(hp * LANES, (hp + 1) * LANES)
        y = y_ref[:, ls]
        d = y - _dg(y.astype(BF16), group_ones) * inv_n
        var = _dg((d * d).astype(BF16), group_ones) * inv_n
        yn = d * lax.rsqrt(var + GN_EPS) * gnw_ref[:, ls] + gnb_ref[:, ls]
        pieces.append(((yn + bon_ref[:, ls]) * g_ref[:, ls]).astype(BF16))
    ya = _dg(ya_ref[...], wa_ref[...])
    yb = _dg(jnp.concatenate(pieces, axis=1), wb_ref[...])
    ga = _sigmoid(gate_ref[:, :D_MODEL].astype(F32))
    gb = _sigmoid(gate_ref[:, D_MODEL:].astype(F32))
    mixed = _dg((ga * ya + gb * yb).astype(BF16), wo_ref[...])
    x1 = _layer_norm(DEEPNORM_ALPHA * x_ref[...] + mixed, lnw_ref[...], lnb_ref[...])
    x1_ref[...] = x1
    x1b_ref[...] = x1.astype(BF16)


def _const_spec(x):
    return pl.BlockSpec(x.shape, lambda *_: (0,) * x.ndim, pipeline_mode=pl.Buffered(1))


def _mix(ya, y, bon, g, gates, x, gn_w, gn_b, w_up_a, w_up_b, w_o, ln_w, ln_b):
    t = x.shape[0]
    tm = min(MIX_TM, t)
    consts = (gn_w.reshape(1, -1), gn_b.reshape(1, -1), w_up_a, w_up_b, w_o, ln_w.reshape(1, -1), ln_b.reshape(1, -1))
    row = lambda w: pl.BlockSpec((tm, w), lambda i: (i, 0))
    return pl.pallas_call(
        _mix_kernel,
        grid=(t // tm,),
        in_specs=[row(MIX_WIDTH), row(MIX_WIDTH), row(MIX_WIDTH), row(MIX_WIDTH), row(2 * D_MODEL), row(D_MODEL)]
                 + [_const_spec(c) for c in consts],
        out_specs=[pl.BlockSpec((tm, D_MODEL), lambda i: (i, 0))] * 2,
        out_shape=[jax.ShapeDtypeStruct((t, D_MODEL), F32), jax.ShapeDtypeStruct((t, D_MODEL), BF16)],
        compiler_params=_cparams("parallel"),
    )(ya, y, bon, g, gates, x, *consts)


def _topk_rows(s, ids, k):
    big = jnp.asarray(1e9, F32)
    vals, idxs = [], []
    for _ in range(k):
        m = jnp.max(s, axis=0, keepdims=True)
        ix = jnp.min(jnp.where(s == m, ids, big), axis=0, keepdims=True)
        vals.append(m)
        idxs.append(ix)
        s = jnp.where(ids == ix, -jnp.inf, s)
    return jnp.concatenate(vals, axis=0), jnp.concatenate(idxs, axis=0)


def _route_kernel(x_ref, wq_ref, keys_ref, e0_ref, e1_ref, gate_ref):
    tm = x_ref.shape[0]
    topk = PEER_TOPK
    q = _dg(x_ref[...], wq_ref[...])
    qb = q.astype(BF16)
    half = N_KEYS
    key_ids = lax.broadcasted_iota(I32, (N_KEYS, tm), 0).astype(F32)
    sub8 = lax.broadcasted_iota(I32, (8, tm), 0).astype(F32)
    sub16 = lax.broadcasted_iota(I32, (topk, tm), 0).astype(F32)
    cand_ids = jnp.concatenate([sub16] + [a * topk + sub8 for a in range(1, 8)] + [(sub8 + 8.0) * topk], axis=0)
    e0s, e1s, gates = [], [], []
    for h in range(PEER_HEADS):
        tops = []
        for c in range(2):
            qs = qb[:, (2 * h + c) * half:(2 * h + c + 1) * half]
            s = _dg(keys_ref[c], qs, NT)
            tops.append(_topk_rows(s, key_ids, topk))
        (s0, i0), (s1, i1) = tops
        cand = jnp.concatenate([s0[0:1, :] + s1] + [s0[a:a + 1, :] + s1[:8, :] for a in range(1, 8)]
                               + [s0[8:, :] + s1[0:1, :]], axis=0)
        best, pos = _topk_rows(cand, cand_ids, topk)
        a_sel = jnp.floor(pos * (1.0 / topk))
        b_sel = pos - a_sel * topk
        e0 = jnp.zeros_like(pos)
        e1 = jnp.zeros_like(pos)
        for j in range(topk):
            e0 = jnp.where(a_sel == j, i0[j:j + 1, :], e0)
            e1 = jnp.where(b_sel == j, i1[j:j + 1, :], e1)
        ex = jnp.exp(best - best[0:1, :])
        gates.append(ex / jnp.sum(ex, axis=0, keepdims=True))
        e0s.append(e0)
        e1s.append(e1)
    e0_ref[...] = jnp.concatenate(e0s, axis=0).T.astype(I32)
    e1_ref[...] = jnp.concatenate(e1s, axis=0).T.astype(I32)
    gate_ref[...] = jnp.concatenate(gates, axis=0).T


def _route(x1b, w_q, sub_keys):
    t = x1b.shape[0]
    tm = min(ROUTE_TM, t)
    out = lambda: pl.BlockSpec((tm, N_SEL), lambda i: (i, 0))
    return pl.pallas_call(
        _route_kernel,
        grid=(t // tm,),
        in_specs=[pl.BlockSpec((tm, D_MODEL), lambda i: (i, 0)), _const_spec(w_q), _const_spec(sub_keys)],
        out_specs=[out(), out(), out()],
        out_shape=[jax.ShapeDtypeStruct((t, N_SEL), I32), jax.ShapeDtypeStruct((t, N_SEL), I32),
                   jax.ShapeDtypeStruct((t, N_SEL), F32)],
        compiler_params=_cparams("parallel"),
    )(x1b, w_q, sub_keys)


def _gelu_exact(x):
    return 0.5 * x * (1.0 + lax.erf(x * (2.0 ** -0.5)))


def _peer_score_kernel(x_ref, u_ref, e0_ref, e1_ref, gate_ref, o_ref, acc_ref):
    j = pl.program_id(1)

    @pl.when(j == 0)
    def _():
        acc_ref[...] = jnp.zeros_like(acc_ref)

    x = x_ref[...]
    e0, e1 = e0_ref[...], e1_ref[...]
    blocks = u_ref.shape[0] // N_KEYS
    per_dot = PH_SUB // N_KEYS
    acc = acc_ref[...]
    for s in range(u_ref.shape[0] // PH_SUB):
        h = _dg(x, u_ref[s * PH_SUB:(s + 1) * PH_SUB, :].astype(BF16), NT)
        for b in range(per_dot):
            picked = jnp.take_along_axis(h[:, b * N_KEYS:(b + 1) * N_KEYS], e1, axis=1)
            acc = acc + jnp.where(e0 == j * blocks + s * per_dot + b, picked, 0.0)
    acc_ref[...] = acc

    @pl.when(j == pl.num_programs(1) - 1)
    def _():
        o_ref[...] = _gelu_exact(acc_ref[...]) * gate_ref[...]


def _peer_score(x1b, u_bf16, e0, e1, gate):
    t = x1b.shape[0]
    tm = min(PH_TM, t)
    tok = lambda: pl.BlockSpec((tm, N_SEL), lambda i, j: (i, 0))
    return pl.pallas_call(
        _peer_score_kernel,
        grid=(t // tm, N_EXPERTS // PH_TN),
        in_specs=[pl.BlockSpec((tm, D_MODEL), lambda i, j: (i, 0)),
                  pl.BlockSpec((PH_TN, D_MODEL), lambda i, j: (j, 0)), tok(), tok(), tok()],
        out_specs=tok(),
        out_shape=jax.ShapeDtypeStruct((t, N_SEL), F32),
        scratch_shapes=[pltpu.VMEM((tm, N_SEL), F32)],
        compiler_params=_cparams("parallel", "arbitrary"),
    )(x1b, u_bf16, e0, e1, gate)


def _peer_gate_kernel(e0_ref, e1_ref, act_ref, o_ref):
    sub = lax.broadcasted_iota(I32, (N_KEYS, N_SEL), 0)

    def body(grp, carry):
        mats = []
        for u in range(PG_GRP):
            row = pl.ds(grp * PG_GRP + u, 1)
            left = jnp.where(e0_ref[row, :] == sub, act_ref[row, :], 0.0).astype(BF16)
            right = jnp.where(e1_ref[row, :] == sub, 1.0, 0.0).astype(BF16)
            mats.append(_dg(left, right, NT))
        o_ref[grp] = jnp.swapaxes(jnp.stack(mats, axis=0), 0, 1).astype(o_ref.dtype)
        return carry

    lax.fori_loop(0, o_ref.shape[0], body, 0)


def _peer_gate(e0, e1, act):
    t = e0.shape[0]
    tt = min(PG_TT, t)
    tok = lambda: pl.BlockSpec((tt, N_SEL), lambda i: (i, 0))
    return pl.pallas_call(
        _peer_gate_kernel,
        grid=(t // tt,),
        in_specs=[tok(), tok(), tok()],
        out_specs=pl.BlockSpec((tt // PG_GRP, N_KEYS, PG_GRP, N_KEYS), lambda i: (i, 0, 0, 0)),
        out_shape=jax.ShapeDtypeStruct((t // PG_GRP, N_KEYS, PG_GRP, N_KEYS), BF16),
        compiler_params=_cparams("parallel"),
    )(e0, e1, act)


def _peer_value_kernel(g_ref, v_ref, o_ref):
    @pl.when(pl.program_id(1) == 0)
    def _():
        o_ref[...] = jnp.zeros_like(o_ref)

    tm = o_ref.shape[0]
    lhs = jnp.concatenate([g_ref[:, e, :, :].reshape(tm, N_KEYS) for e in range(g_ref.shape[1])], axis=1)
    o_ref[...] += _dg(lhs, v_ref[...])


def _peer_value(g4, v_bf16):
    t = g4.shape[0] * PG_GRP
    tm = min(PV_TM, t)
    return pl.pallas_call(
        _peer_value_kernel,
        grid=(t // tm, N_KEYS // PV_NE0),
        in_specs=[pl.BlockSpec((tm // PG_GRP, PV_NE0, PG_GRP, N_KEYS), lambda i, k: (i, k, 0, 0)),
                  pl.BlockSpec((PV_NE0 * N_KEYS, D_MODEL), lambda i, k: (k, 0))],
        out_specs=pl.BlockSpec((tm, D_MODEL), lambda i, k: (i, 0)),
        out_shape=jax.ShapeDtypeStruct((t, D_MODEL), F32),
        compiler_params=_cparams("parallel", "arbitrary"),
    )(g4, v_bf16)


def _final_kernel(x1_ref, x1b_ref, ffn_ref, p_ref, wg_ref, wp_ref, lnw_ref, lnb_ref, o_ref):
    ple = _sigmoid(_dg(x1b_ref[...], wg_ref[...])) * _dg(p_ref[...].astype(BF16), wp_ref[...])
    o_ref[...] = _layer_norm(DEEPNORM_ALPHA * x1_ref[...] + ffn_ref[...] + ple, lnw_ref[...], lnb_ref[...])


def _final(x1, x1b, ffn, p, w_gate, w_proj, ln_w, ln_b):
    t = x1.shape[0]
    tm = min(FIN_TM, t)
    consts = (w_gate, w_proj, ln_w.reshape(1, -1), ln_b.reshape(1, -1))
    row = lambda w: pl.BlockSpec((tm, w), lambda i: (i, 0))
    return pl.pallas_call(
        _final_kernel,
        grid=(t // tm,),
        in_specs=[row(D_MODEL), row(D_MODEL), row(D_MODEL), row(p.shape[1])] + [_const_spec(c) for c in consts],
        out_specs=row(D_MODEL),
        out_shape=jax.ShapeDtypeStruct((t, D_MODEL), F32),
        compiler_params=_cparams("parallel"),
    )(x1, x1b, ffn, p, *consts)


def _layer(x, p, w_in, rwkv_mu, rwkv_w0, rwkv_w_lora1, rwkv_w_lora2, rwkv_a0, rwkv_a_lora1,
           rwkv_a_lora2, rwkv_g_lora1, rwkv_g_lora2, rwkv_k_k, rwkv_k_a, rwkv_r_k, rwkv_lnx_w,
           rwkv_lnx_b, w_up_a, w_up_b, w_o, ln1_w, ln1_b, peer_w_q, peer_sub_keys, peer_u, peer_v,
           ple_w_gate, ple_w_proj, ln2_w, ln2_b):
    bf = lambda w: w.astype(BF16)
    n_sb = 3 * MIX_WIDTH
    n_rw = 4 * MIX_WIDTH
    qkv = _proj(x, w_in, 0, n_sb, BF16)
    rw = _proj(x, w_in, n_sb, n_rw, BF16)
    gates = _proj(x, w_in, n_sb + n_rw, 2 * D_MODEL, BF16)
    y_a = _sb_attention(qkv)
    r, k, v, lw, a, g = _rwkv_pre(rw, rwkv_mu, rwkv_w0, rwkv_w_lora1, rwkv_w_lora2, rwkv_a0,
                                  rwkv_a_lora1, rwkv_a_lora2, rwkv_g_lora1, rwkv_g_lora2)
    rp, y0, bon, pm, qm = _rwkv_chunk(r, k, v, lw, a, rwkv_k_k, rwkv_k_a, rwkv_r_k.reshape(-1))
    y_b = _rwkv_state(rp, y0, pm, qm)
    x1, x1b = _mix(y_a, y_b, bon, g, gates, x, rwkv_lnx_w, rwkv_lnx_b, bf(w_up_a), bf(w_up_b), bf(w_o),
                   ln1_w, ln1_b)
    e0, e1, gate = _route(x1b, bf(peer_w_q), bf(peer_sub_keys))
    act = _peer_score(x1b, peer_u, e0, e1, gate)
    ffn = _peer_value(_peer_gate(e0, e1, act), bf(peer_v))
    return _final(x1, x1b, ffn, p, bf(ple_w_gate), bf(ple_w_proj), ln2_w, ln2_b)


def kernel(x, p, w_in, rwkv_mu, rwkv_w0, rwkv_w_lora1, rwkv_w_lora2, rwkv_a0, rwkv_a_lora1, rwkv_a_lora2, rwkv_g_lora1, rwkv_g_lora2, rwkv_k_k, rwkv_k_a, rwkv_r_k, rwkv_lnx_w, rwkv_lnx_b, w_up_a, w_up_b, w_o, ln1_w, ln1_b, peer_w_q, peer_sub_keys, peer_u, peer_v, ple_w_gate, ple_w_proj, ln2_w, ln2_b):
    bsz, t, d = x.shape
    depth = w_in.shape[0]
    xs = x.reshape(bsz * t, d)
    assert bsz == 1, "token shift / attention / scan treat the flattened rows as one sequence"
    for i in range(depth):
        xs = _layer(xs, p[i].reshape(bsz * t, -1), w_in[i], rwkv_mu[i], rwkv_w0[i], rwkv_w_lora1[i],
                    rwkv_w_lora2[i], rwkv_a0[i], rwkv_a_lora1[i], rwkv_a_lora2[i], rwkv_g_lora1[i],
                    rwkv_g_lora2[i], rwkv_k_k[i], rwkv_k_a[i], rwkv_r_k[i], rwkv_lnx_w[i], rwkv_lnx_b[i],
                    w_up_a[i], w_up_b[i], w_o[i], ln1_w[i], ln1_b[i], peer_w_q[i], peer_sub_keys[i],
                    peer_u[i], peer_v[i], ple_w_gate[i], ple_w_proj[i], ln2_w[i], ln2_b[i])
    return xs.reshape(bsz, t, d)
```

```python
import functools

import jax
import jax.numpy as jnp
from jax import lax
from jax.experimental import pallas as pl
from jax.experimental.pallas import tpu as pltpu

F32 = jnp.float32
BF16 = jnp.bfloat16
I32 = jnp.int32

D_MODEL = 2048
HEAD_DIM = 64
N_HEADS = 16
MIX_WIDTH = N_HEADS * HEAD_DIM
PEER_HEADS = 8
PEER_TOPK = 16
N_KEYS = 128
N_EXPERTS = N_KEYS * N_KEYS
N_SEL = PEER_HEADS * PEER_TOPK
GN_EPS = 64e-5
LN_EPS = 1e-5
DEEPNORM_ALPHA = 2.0 ** 0.25

LANES = 128
VMEM_LIMIT = 56 * 1024 * 1024

MM_TM, MM_TN = 1024, 1024
SB_BLK = 128
SB_SUBS = 4
SB_WIN = 2 * LANES
SB_DONE = -88.0
RW_TM = 256
RW_C = 64
RW_NCH = 16
RW_NCS = 8
MIX_TM = 256
ROUTE_TM = 256
PH_TM, PH_TN = 1024, 2048
PH_SUB = 256
PG_TT = 128
PG_GRP = 16
PV_TM, PV_NE0 = 1024, 16
FIN_TM = 512

assert 2 * RW_C == LANES and RW_C == HEAD_DIM

NN = (((1,), (0,)), ((), ()))
NT = (((1,), (1,)), ((), ()))
TN = (((0,), (0,)), ((), ()))


def _cparams(*sem):
    return pltpu.CompilerParams(dimension_semantics=tuple(sem), vmem_limit_bytes=VMEM_LIMIT)


def _dg(a, b, dims=NN):
    return lax.dot_general(a, b, dims, preferred_element_type=F32)


def _split2(x):
    hi = x.astype(BF16)
    lo = (x - hi.astype(F32)).astype(BF16)
    return hi, lo


def _mm1(a, b, dims=NN):
    return _dg(a.astype(BF16), b.astype(BF16), dims)


def _mm_sum_lhs(m_bf16, x):
    hi, lo = _split2(x)
    return _dg(m_bf16, hi) + _dg(m_bf16, lo)


def _softplus(z):
    return jnp.maximum(z, 0.0) + jnp.log(1.0 + jnp.exp(-jnp.abs(z)))


def _sigmoid(z):
    return 0.5 * jnp.tanh(0.5 * z) + 0.5


def _layer_norm(x, g, b):
    mu = jnp.mean(x, axis=-1, keepdims=True)
    d = x - mu
    var = jnp.mean(d * d, axis=-1, keepdims=True)
    return d * lax.rsqrt(var + LN_EPS) * g + b


def _proj_kernel(a_ref, b_ref, o_ref, bw_ref):
    @pl.when(pl.program_id(1) == 0)
    def _():
        bw_ref[...] = b_ref[...].astype(BF16)

    o_ref[...] = _dg(a_ref[...].astype(BF16), bw_ref[...]).astype(o_ref.dtype)


def _proj(a, b, col0, n, out_dtype):
    m, k = a.shape
    tm, tn = min(MM_TM, m), MM_TN
    assert col0 % tn == 0 and n % tn == 0
    off = col0 // tn
    return pl.pallas_call(
        _proj_kernel,
        grid=(n // tn, m // tm),
        in_specs=[pl.BlockSpec((tm, k), lambda j, i: (i, 0)),
                  pl.BlockSpec((k, tn), lambda j, i: (0, j + off))],
        out_specs=pl.BlockSpec((tm, tn), lambda j, i: (i, j)),
        out_shape=jax.ShapeDtypeStruct((m, n), out_dtype),
        scratch_shapes=[pltpu.VMEM((k, tn), BF16)],
        compiler_params=_cparams("parallel", "arbitrary"),
    )(a, b)


def _sb_kernel(q_ref, k_ref, v_ref, o_ref, acc_ref, carry_ref):
    i = pl.program_id(1)
    blk = SB_BLK
    n_sub = q_ref.shape[0] // blk
    win = SB_WIN
    lane = lax.broadcasted_iota(I32, (1, LANES), 1)
    row = lax.broadcasted_iota(I32, (blk, win), 0)
    col = lax.broadcasted_iota(I32, (blk, win), 1)
    r_io = lax.broadcasted_iota(I32, (win, win), 0)
    c_io = lax.broadcasted_iota(I32, (win, win), 1)
    later_mat = jnp.where(r_io > c_io, 1.0, 0.0).astype(BF16)
    scale = jnp.asarray(HEAD_DIM ** -0.5, BF16)
    chains = [(s, h) for s in range(n_sub) for h in range(2)]
    qh = []
    for s, h in chains:
        q = q_ref[s * blk:(s + 1) * blk, :]
        qh.append(jnp.where((lane < HEAD_DIM) == (h == 0), q, jnp.zeros_like(q)) * scale)
    first_end = [(i * n_sub + s + 1) * blk for s in range(n_sub)]
    acc_ref[...] = jnp.zeros_like(acc_ref)
    carry_ref[...] = jnp.zeros_like(carry_ref)

    def step(it):
        kwin, vwin, valid = [], [], []
        for s in range(n_sub):
            end = first_end[s] - it * win
            start = pl.multiple_of(jnp.maximum(end - win, 0), blk)
            kwin.append(k_ref[pl.ds(start, win), :])
            vwin.append(v_ref[pl.ds(start, win), :])
            valid.append((start + col) < jnp.minimum(first_end[s] - blk + row, end))
        z = [_dg(qh[c], kwin[s], NT) for c, (s, h) in enumerate(chains)]
        sp = [_softplus(x) for x in z]
        log_keep = [jnp.where(valid[s], -sp[c], 0.0).astype(BF16) for c, (s, h) in enumerate(chains)]
        carry = [carry_ref[c] for c in range(len(chains))]
        log_later = [_dg(log_keep[c], later_mat) + jnp.concatenate([carry[c]] * (win // LANES), axis=1)
                     for c in range(len(chains))]
        w = [jnp.where(valid[s], jnp.exp(z[c] - sp[c] + log_later[c]), 0.0).astype(BF16)
             for c, (s, h) in enumerate(chains)]
        pv = [_dg(w[c], vwin[s]) for c, (s, h) in enumerate(chains)]
        for c in range(len(chains)):
            acc_ref[c] += pv[c]
            carry_ref[c] = jnp.broadcast_to(log_later[c][:, 0:1] + log_keep[c][:, 0:1], (blk, LANES))
        return it + 1

    def cond(it):
        alive = None
        for s in range(n_sub):
            more = jnp.logical_and(first_end[s] - it * win > 0, jnp.max(carry_ref[2 * s:2 * s + 2]) > SB_DONE)
            alive = more if alive is None else jnp.logical_or(alive, more)
        return alive

    lax.while_loop(cond, step, step(0))
    for s in range(n_sub):
        o_ref[s * blk:(s + 1) * blk, :] = jnp.where(lane < HEAD_DIM, acc_ref[2 * s], acc_ref[2 * s + 1]).astype(o_ref.dtype)


def _sb_attention(qkv):
    t = qkv.shape[0]
    blk = SB_BLK * SB_SUBS
    n_pairs = MIX_WIDTH // LANES
    return pl.pallas_call(
        _sb_kernel,
        grid=(n_pairs, t // blk),
        in_specs=[pl.BlockSpec((blk, LANES), lambda hp, i: (i, hp)),
                  pl.BlockSpec((t, LANES), lambda hp, i: (0, n_pairs + hp)),
                  pl.BlockSpec((t, LANES), lambda hp, i: (0, 2 * n_pairs + hp))],
        out_specs=pl.BlockSpec((blk, LANES), lambda hp, i: (i, hp)),
        out_shape=jax.ShapeDtypeStruct((t, MIX_WIDTH), BF16),
        scratch_shapes=[pltpu.VMEM((2 * SB_SUBS, SB_BLK, LANES), F32), pltpu.VMEM((2 * SB_SUBS, SB_BLK, LANES), F32)],
        compiler_params=_cparams("parallel", "parallel"),
    )(qkv, qkv, qkv)


def _rwkv_pre_kernel(cur_ref, prev_ref, mu_ref, w0_ref, wl1_ref, wl2_ref, a0_ref, al1_ref, al2_ref,
                     gl1_ref, gl2_ref, r_ref, k_ref, v_ref, lw_ref, a_ref, g_ref):
    i = pl.program_id(0)
    tm = cur_ref.shape[0]
    wd = MIX_WIDTH
    first_row = lax.broadcasted_iota(I32, (tm, wd), 0) == 0
    keep_prev = jnp.where(i == 0, 0.0, 1.0)

    n_prev = prev_ref.shape[0]

    def shifted(col):
        z = cur_ref[:, col * wd:(col + 1) * wd].astype(F32)
        last = prev_ref[n_prev - 1:n_prev, col * wd:(col + 1) * wd].astype(F32) * keep_prev
        prev = jnp.where(first_row, last, pltpu.roll(z, 1, 0))
        return z, prev - z

    z, d = shifted(0)
    r_ref[...] = (z + d * mu_ref[0:1, :]).astype(r_ref.dtype)
    z, d = shifted(1)
    k_ref[...] = (z + d * mu_ref[1:2, :]).astype(k_ref.dtype)
    z, d = shifted(2)
    v_ref[...] = (z + d * mu_ref[2:3, :]).astype(v_ref.dtype)
    z, d = shifted(3)
    xw = z + d * mu_ref[3:4, :]
    xa = z + d * mu_ref[4:5, :]
    xg = z + d * mu_ref[5:6, :]
    w = w0_ref[...] + _mm1(jnp.tanh(_mm1(xw, wl1_ref[...])), wl2_ref[...])
    w = -_softplus(-w) - 0.5
    lw_ref[...] = -jnp.exp(w)
    a_ref[...] = _sigmoid(a0_ref[...] + _mm1(_mm1(xa, al1_ref[...]), al2_ref[...])).astype(a_ref.dtype)
    g_ref[...] = _mm1(_sigmoid(_mm1(xg, gl1_ref[...])), gl2_ref[...]).astype(g_ref.dtype)


def _pad_to(x, axis, size):
    pad = [(0, 0)] * x.ndim
    pad[axis] = (0, size - x.shape[axis])
    return jnp.pad(x, pad)


def _rwkv_pre(rw, mu, w0, wl1, wl2, a0, al1, al2, gl1, gl2):
    t = rw.shape[0]
    tm = min(RW_TM, t)
    wd = MIX_WIDTH
    lo = LANES

    def lora_pair(l1, l2):
        n = -(-l1.shape[1] // lo) * lo
        return _pad_to(l1, 1, n).astype(BF16), _pad_to(l2, 0, n).astype(BF16)

    wl1, wl2 = lora_pair(wl1, wl2)
    al1, al2 = lora_pair(al1, al2)
    gl1, gl2 = lora_pair(gl1, gl2)
    full = lambda x: pl.BlockSpec(x.shape, lambda i: (0,) * x.ndim)
    row = lambda: pl.BlockSpec((tm, wd), lambda i: (i, 0))
    consts = (mu, w0.reshape(1, wd), wl1, wl2, a0.reshape(1, wd), al1, al2, gl1, gl2)
    n_prev = 32 // rw.dtype.itemsize
    out_dtypes = (BF16, BF16, BF16, F32, BF16, BF16)
    return pl.pallas_call(
        _rwkv_pre_kernel,
        grid=(t // tm,),
        in_specs=[pl.BlockSpec((tm, 4 * wd), lambda i: (i, 0)),
                  pl.BlockSpec((n_prev, 4 * wd), lambda i: (jnp.maximum(i * (tm // n_prev) - 1, 0), 0))]
                 + [full(c) for c in consts],
        out_specs=[row() for _ in range(6)],
        out_shape=[jax.ShapeDtypeStruct((t, wd), dt) for dt in out_dtypes],
        compiler_params=_cparams("parallel"),
    )(rw, rw, *consts)


def _rwkv_chunk_kernel(r_ref, k_ref, v_ref, lw_ref, a_ref, kkp_ref, kap_ref, rkp_ref,
                       rp_ref, y0_ref, bon_ref, p_ref, q_ref):
    c_len = RW_C
    n2 = 2 * c_len
    n_chunks = r_ref.shape[0] // c_len
    r2 = lax.broadcasted_iota(I32, (n2, LANES), 0)
    c2 = lax.broadcasted_iota(I32, (n2, LANES), 1)
    same = (r2 < c_len) == (c2 < HEAD_DIM)
    t_r = jnp.where(r2 < c_len, r2, r2 - c_len)
    t_c = jnp.where(c2 < HEAD_DIM, c2, c2 - HEAD_DIM)
    strict = jnp.logical_and(same, t_r > t_c)
    incl = jnp.logical_and(same, t_r >= t_c)
    eye = jnp.where(r2 == c2, 1.0, 0.0).astype(F32)
    group_ones = jnp.where(same, 1.0, 0.0).astype(BF16)
    lr = lax.broadcasted_iota(I32, (c_len, c_len), 0)
    lc = lax.broadcasted_iota(I32, (c_len, c_len), 1)
    cum_mat = jnp.where(lr >= lc, 1.0, 0.0).astype(BF16)
    kkp, kap, rkp = kkp_ref[...], kap_ref[...], rkp_ref[...]

    def stack(z):
        return jnp.where(same, jnp.concatenate([z, z], axis=0), 0.0)

    def unstack(zs):
        return zs[:c_len] + zs[c_len:]

    cs = range(n_chunks)
    rows = [pl.ds(c * c_len, c_len) for c in cs]
    r = [r_ref[rw, :].astype(F32) for rw in rows]
    kr = [k_ref[rw, :].astype(F32) for rw in rows]
    v = [v_ref[rw, :].astype(F32) for rw in rows]
    lw = [lw_ref[rw, :] for rw in rows]
    a = [a_ref[rw, :].astype(F32) for rw in rows]
    kk = [x * kkp for x in kr]
    cum = [_mm_sum_lhs(cum_mat, x) for x in lw]
    ssq = [_dg((x * x).astype(BF16), group_ones) for x in kk]
    km = [kr[c] * (1.0 + (a[c] - 1.0) * kap) for c in cs]
    bsum = [_dg((r[c] * km[c] * rkp).astype(BF16), group_ones) for c in cs]
    kk = [kk[c] / jnp.maximum(jnp.sqrt(ssq[c]), 1e-12) for c in cs]
    bv = [kk[c] * a[c] for c in cs]
    last = [x[c_len - 1:c_len, :] for x in cum]
    rt = [stack(r[c] * jnp.exp(cum[c])) for c in cs]
    at_b = [stack(-kk[c] * jnp.exp(cum[c] - lw[c])).astype(BF16) for c in cs]
    rt_b = [x.astype(BF16) for x in rt]
    g_inv = [jnp.exp(-x) for x in cum]
    btkt = [jnp.concatenate([stack(bv[c] * g_inv[c]), stack(km[c] * g_inv[c])], axis=0).astype(BF16) for c in cs]
    m_a = [_dg(at_b[c], btkt[c], NT) for c in cs]
    m_r = [_dg(rt_b[c], btkt[c], NT) for c in cs]
    vs_b = [stack(x).astype(BF16) for x in v]
    g_end = [jnp.exp(last[c] - cum[c]) for c in cs]
    bd_b = [stack(bv[c] * g_end[c]).astype(BF16) for c in cs]
    kd_b = [stack(km[c] * g_end[c]).astype(BF16) for c in cs]
    m_ab = [jnp.where(strict, x[:, :n2], 0.0) for x in m_a]
    m_ak = [jnp.where(strict, x[:, n2:], 0.0).astype(BF16) for x in m_a]
    m_rb = [jnp.where(incl, x[:, :n2], 0.0).astype(BF16) for x in m_r]
    m_rk = [jnp.where(incl, x[:, n2:], 0.0).astype(BF16) for x in m_r]
    akv = [_dg(m_ak[c], vs_b[c]) for c in cs]
    rkv = [_dg(m_rk[c], vs_b[c]) for c in cs]
    kdv = [_dg(kd_b[c], vs_b[c], TN) for c in cs]
    tinv = [eye + x for x in m_ab]
    npow = [x.astype(BF16) for x in m_ab]
    for _ in range(5):
        npow = [_dg(x, x).astype(BF16) for x in npow]
        tinv = [tinv[c] + _dg(tinv[c].astype(BF16), npow[c]) for c in cs]
    au_b = [_dg(tinv[c].astype(BF16), jnp.concatenate([at_b[c], akv[c].astype(BF16)], axis=1)).astype(BF16)
            for c in cs]
    ry = [_dg(m_rb[c], au_b[c]) for c in cs]
    pq = [_dg(bd_b[c], au_b[c], TN) for c in cs]
    for c in cs:
        prow = pl.ds(c * n2, n2)
        rp_ref[rows[c], :] = unstack(rt[c] + ry[c][:, :LANES]).astype(rp_ref.dtype)
        y0_ref[rows[c], :] = unstack(ry[c][:, LANES:] + rkv[c])
        p_ref[prow, :] = (eye * jnp.exp(last[c]) + pq[c][:, :LANES]).astype(p_ref.dtype)
        q_ref[prow, :] = pq[c][:, LANES:] + kdv[c]
        bon_ref[rows[c], :] = (bsum[c] * v[c]).astype(bon_ref.dtype)


def _rwkv_chunk(r, k, v, lw, a, k_k, k_a, r_k):
    t = r.shape[0]
    rb = min(RW_C * RW_NCH, t)
    n_pairs = MIX_WIDTH // LANES
    row = lambda: pl.BlockSpec((rb, LANES), lambda i, hp: (i, hp))
    par = lambda: pl.BlockSpec((1, LANES), lambda i, hp: (0, hp))
    mat = lambda: pl.BlockSpec((2 * rb, LANES), lambda i, hp: (i, hp))
    return pl.pallas_call(
        _rwkv_chunk_kernel,
        grid=(t // rb, n_pairs),
        in_specs=[row() for _ in range(5)] + [par() for _ in range(3)],
        out_specs=[row(), row(), row(), mat(), mat()],
        out_shape=[jax.ShapeDtypeStruct((t, MIX_WIDTH), BF16), jax.ShapeDtypeStruct((t, MIX_WIDTH), F32),
                   jax.ShapeDtypeStruct((t, MIX_WIDTH), BF16), jax.ShapeDtypeStruct((2 * t, MIX_WIDTH), BF16),
                   jax.ShapeDtypeStruct((2 * t, MIX_WIDTH), F32)],
        compiler_params=_cparams("parallel", "parallel"),
    )(r, k, v, lw, a, k_k.reshape(1, -1), k_a.reshape(1, -1), r_k.reshape(1, -1))


def _rwkv_state_kernel(rp_ref, y0_ref, p_ref, q_ref, o_ref, st_ref):
    @pl.when(pl.program_id(0) == 0)
    def _():
        st_ref[...] = jnp.zeros_like(st_ref)

    c_len = RW_C
    pairs = range(MIX_WIDTH // LANES)
    lanes = [slice(hp * LANES, (hp + 1) * LANES) for hp in pairs]
    st = [st_ref[hp] for hp in pairs]
    for c in range(rp_ref.shape[0] // c_len):
        rows = slice(c * c_len, (c + 1) * c_len)
        mrows = slice(2 * c * c_len, 2 * (c + 1) * c_len)
        st_b = [x.astype(BF16) for x in st]
        st = [_dg(p_ref[mrows, lanes[hp]], st_b[hp]) + q_ref[mrows, lanes[hp]] for hp in pairs]
        for hp in pairs:
            o_ref[rows, lanes[hp]] = _dg(rp_ref[rows, lanes[hp]], st_b[hp]) + y0_ref[rows, lanes[hp]]
    for hp in pairs:
        st_ref[hp] = st[hp]


def _rwkv_state(rp, y0, p, q):
    t = rp.shape[0]
    rb = min(RW_C * RW_NCS, t)
    wd = MIX_WIDTH
    row = lambda: pl.BlockSpec((rb, wd), lambda i: (i, 0))
    mat = lambda: pl.BlockSpec((2 * rb, wd), lambda i: (i, 0))
    return pl.pallas_call(
        _rwkv_state_kernel,
        grid=(t // rb,),
        in_specs=[row(), row(), mat(), mat()],
        out_specs=row(),
        out_shape=jax.ShapeDtypeStruct((t, wd), F32),
        scratch_shapes=[pltpu.VMEM((wd // LANES, LANES, LANES), F32)],
        compiler_params=_cparams("arbitrary"),
    )(rp, y0, p, q)


def _mix_kernel(ya_ref, y_ref, bon_ref, g_ref, gate_ref, x_ref, gnw_ref, gnb_ref, wa_ref, wb_ref, wo_ref,
                lnw_ref, lnb_ref, x1_ref, x1b_ref):
    r2 = lax.broadcasted_iota(I32, (LANES, LANES), 0)
    c2 = lax.broadcasted_iota(I32, (LANES, LANES), 1)
    group_ones = jnp.where((r2 < HEAD_DIM) == (c2 < HEAD_DIM), 1.0, 0.0).astype(BF16)
    inv_n = 1.0 / HEAD_DIM
    pieces = []
    for hp in range(MIX_WIDTH // LANES):
        ls = slice(hp * LANES, (hp + 1) * LANES)
        y = y_ref[:, ls]
        d = y - _dg(y.astype(BF16), group_ones) * inv_n
        var = _dg((d * d).astype(BF16), group_ones) * inv_n
        yn = d * lax.rsqrt(var + GN_EPS) * gnw_ref[:, ls] + gnb_ref[:, ls]
        pieces.append(((yn + bon_ref[:, ls]) * g_ref[:, ls]).astype(BF16))
    ya = _dg(ya_ref[...], wa_ref[...])
    yb = _dg(jnp.concatenate(pieces, axis=1), wb_ref[...])
    ga = _sigmoid(gate_ref[:, :D_MODEL].astype(F32))
    gb = _sigmoid(gate_ref[:, D_MODEL:].astype(F32))
    mixed = _dg((ga * ya + gb * yb).astype(BF16), wo_ref[...])
    x1 = _layer_norm(DEEPNORM_ALPHA * x_ref[...] + mixed, lnw_ref[...], lnb_ref[...])
    x1_ref[...] = x1
    x1b_ref[...] = x1.astype(BF16)


def _const_spec(x):
    return pl.BlockSpec(x.shape, lambda *_: (0,) * x.ndim, pipeline_mode=pl.Buffered(1))


def _mix(ya, y, bon, g, gates, x, gn_w, gn_b, w_up_a, w_up_b, w_o, ln_w, ln_b):
    t = x.shape[0]
    tm = min(MIX_TM, t)
    consts = (gn_w.reshape(1, -1), gn_b.reshape(1, -1), w_up_a, w_up_b, w_o, ln_w.reshape(1, -1), ln_b.reshape(1, -1))
    row = lambda w: pl.BlockSpec((tm, w), lambda i: (i, 0))
    return pl.pallas_call(
        _mix_kernel,
        grid=(t // tm,),
        in_specs=[row(MIX_WIDTH), row(MIX_WIDTH), row(MIX_WIDTH), row(MIX_WIDTH), row(2 * D_MODEL), row(D_MODEL)]
                 + [_const_spec(c) for c in consts],
        out_specs=[pl.BlockSpec((tm, D_MODEL), lambda i: (i, 0))] * 2,
        out_shape=[jax.ShapeDtypeStruct((t, D_MODEL), F32), jax.ShapeDtypeStruct((t, D_MODEL), BF16)],
        compiler_params=_cparams("parallel"),
    )(ya, y, bon, g, gates, x, *consts)


def _topk_rows(s, ids, k):
    big = jnp.asarray(1e9, F32)
    vals, idxs = [], []
    for _ in range(k):
        m = jnp.max(s, axis=0, keepdims=True)
        ix = jnp.min(jnp.where(s == m, ids, big), axis=0, keepdims=True)
        vals.append(m)
        idxs.append(ix)
        s = jnp.where(ids == ix, -jnp.inf, s)
    return jnp.concatenate(vals, axis=0), jnp.concatenate(idxs, axis=0)


def _route_kernel(x_ref, wq_ref, keys_ref, e0_ref, e1_ref, gate_ref):
    tm = x_ref.shape[0]
    topk = PEER_TOPK
    q = _dg(x_ref[...], wq_ref[...])
    qb = q.astype(BF16)
    half = N_KEYS
    key_ids = lax.broadcasted_iota(I32, (N_KEYS, tm), 0).astype(F32)
    sub8 = lax.broadcasted_iota(I32, (8, tm), 0).astype(F32)
    sub16 = lax.broadcasted_iota(I32, (topk, tm), 0).astype(F32)
    cand_ids = jnp.concatenate([sub16] + [a * topk + sub8 for a in range(1, 8)] + [(sub8 + 8.0) * topk], axis=0)
    e0s, e1s, gates = [], [], []
    for h in range(PEER_HEADS):
        tops = []
        for c in range(2):
            qs = qb[:, (2 * h + c) * half:(2 * h + c + 1) * half]
            s = _dg(keys_ref[c], qs, NT)
            tops.append(_topk_rows(s, key_ids, topk))
        (s0, i0), (s1, i1) = tops
        cand = jnp.concatenate([s0[0:1, :] + s1] + [s0[a:a + 1, :] + s1[:8, :] for a in range(1, 8)]
                               + [s0[8:, :] + s1[0:1, :]], axis=0)
        best, pos = _topk_rows(cand, cand_ids, topk)
        a_sel = jnp.floor(pos * (1.0 / topk))
        b_sel = pos - a_sel * topk
        e0 = jnp.zeros_like(pos)
        e1 = jnp.zeros_like(pos)
        for j in range(topk):
            e0 = jnp.where(a_sel == j, i0[j:j + 1, :], e0)
            e1 = jnp.where(b_sel == j, i1[j:j + 1, :], e1)
        ex = jnp.exp(best - best[0:1, :])
        gates.append(ex / jnp.sum(ex, axis=0, keepdims=True))
        e0s.append(e0)
        e1s.append(e1)
    e0_ref[...] = jnp.concatenate(e0s, axis=0).T.astype(I32)
    e1_ref[...] = jnp.concatenate(e1s, axis=0).T.astype(I32)
    gate_ref[...] = jnp.concatenate(gates, axis=0).T


def _route(x1b, w_q, sub_keys):
    t = x1b.shape[0]
    tm = min(ROUTE_TM, t)
    out = lambda: pl.BlockSpec((tm, N_SEL), lambda i: (i, 0))
    return pl.pallas_call(
        _route_kernel,
        grid=(t // tm,),
        in_specs=[pl.BlockSpec((tm, D_MODEL), lambda i: (i, 0)), _const_spec(w_q), _const_spec(sub_keys)],
        out_specs=[out(), out(), out()],
        out_shape=[jax.ShapeDtypeStruct((t, N_SEL), I32), jax.ShapeDtypeStruct((t, N_SEL), I32),
                   jax.ShapeDtypeStruct((t, N_SEL), F32)],
        compiler_params=_cparams("parallel"),
    )(x1b, w_q, sub_keys)


def _gelu_exact(x):
    return 0.5 * x * (1.0 + lax.erf(x * (2.0 ** -0.5)))


def _peer_score_kernel(x_ref, u_ref, e0_ref, e1_ref, gate_ref, o_ref, acc_ref):
    j = pl.program_id(1)

    @pl.when(j == 0)
    def _():
        acc_ref[...] = jnp.zeros_like(acc_ref)

    x = x_ref[...]
    e0, e1 = e0_ref[...], e1_ref[...]
    blocks = u_ref.shape[0] // N_KEYS
    per_dot = PH_SUB // N_KEYS
    acc = acc_ref[...]
    for s in range(u_ref.shape[0] // PH_SUB):
        h = _dg(x, u_ref[s * PH_SUB:(s + 1) * PH_SUB, :].astype(BF16), NT)
        for b in range(per_dot):
            picked = jnp.take_along_axis(h[:, b * N_KEYS:(b + 1) * N_KEYS], e1, axis=1)
            acc = acc + jnp.where(e0 == j * blocks + s * per_dot + b, picked, 0.0)
    acc_ref[...] = acc

    @pl.when(j == pl.num_programs(1) - 1)
    def _():
        o_ref[...] = _gelu_exact(acc_ref[...]) * gate_ref[...]


def _peer_score(x1b, u_bf16, e0, e1, gate):
    t = x1b.shape[0]
    tm = min(PH_TM, t)
    tok = lambda: pl.BlockSpec((tm, N_SEL), lambda i, j: (i, 0))
    return pl.pallas_call(
        _peer_score_kernel,
        grid=(t // tm, N_EXPERTS // PH_TN),
        in_specs=[pl.BlockSpec((tm, D_MODEL), lambda i, j: (i, 0)),
                  pl.BlockSpec((PH_TN, D_MODEL), lambda i, j: (j, 0)), tok(), tok(), tok()],
        out_specs=tok(),
        out_shape=jax.ShapeDtypeStruct((t, N_SEL), F32),
        scratch_shapes=[pltpu.VMEM((tm, N_SEL), F32)],
        compiler_params=_cparams("parallel", "arbitrary"),
    )(x1b, u_bf16, e0, e1, gate)


def _peer_gate_kernel(e0_ref, e1_ref, act_ref, o_ref):
    sub = lax.broadcasted_iota(I32, (N_KEYS, N_SEL), 0)

    def body(grp, carry):
        mats = []
        for u in range(PG_GRP):
            row = pl.ds(grp * PG_GRP + u, 1)
            left = jnp.where(e0_ref[row, :] == sub, act_ref[row, :], 0.0).astype(BF16)
            right = jnp.where(e1_ref[row, :] == sub, 1.0, 0.0).astype(BF16)
            mats.append(_dg(left, right, NT))
        o_ref[grp] = jnp.swapaxes(jnp.stack(mats, axis=0), 0, 1).astype(o_ref.dtype)
        return carry

    lax.fori_loop(0, o_ref.shape[0], body, 0)


def _peer_gate(e0, e1, act):
    t = e0.shape[0]
    tt = min(PG_TT, t)
    tok = lambda: pl.BlockSpec((tt, N_SEL), lambda i: (i, 0))
    return pl.pallas_call(
        _peer_gate_kernel,
        grid=(t // tt,),
        in_specs=[tok(), tok(), tok()],
        out_specs=pl.BlockSpec((tt // PG_GRP, N_KEYS, PG_GRP, N_KEYS), lambda i: (i, 0, 0, 0)),
        out_shape=jax.ShapeDtypeStruct((t // PG_GRP, N_KEYS, PG_GRP, N_KEYS), BF16),
        compiler_params=_cparams("parallel"),
    )(e0, e1, act)


def _peer_value_kernel(g_ref, v_ref, o_ref):
    @pl.when(pl.program_id(1) == 0)
    def _():
        o_ref[...] = jnp.zeros_like(o_ref)

    tm = o_ref.shape[0]
    lhs = jnp.concatenate([g_ref[:, e, :, :].reshape(tm, N_KEYS) for e in range(g_ref.shape[1])], axis=1)
    o_ref[...] += _dg(lhs, v_ref[...])


def _peer_value(g4, v_bf16):
    t = g4.shape[0] * PG_GRP
    tm = min(PV_TM, t)
    return pl.pallas_call(
        _peer_value_kernel,
        grid=(t // tm, N_KEYS // PV_NE0),
        in_specs=[pl.BlockSpec((tm // PG_GRP, PV_NE0, PG_GRP, N_KEYS), lambda i, k: (i, k, 0, 0)),
                  pl.BlockSpec((PV_NE0 * N_KEYS, D_MODEL), lambda i, k: (k, 0))],
        out_specs=pl.BlockSpec((tm, D_MODEL), lambda i, k: (i, 0)),
        out_shape=jax.ShapeDtypeStruct((t, D_MODEL), F32),
        compiler_params=_cparams("parallel", "arbitrary"),
    )(g4, v_bf16)


def _final_kernel(x1_ref, x1b_ref, ffn_ref, p_ref, wg_ref, wp_ref, lnw_ref, lnb_ref, o_ref):
    ple = _sigmoid(_dg(x1b_ref[...], wg_ref[...])) * _dg(p_ref[...].astype(BF16), wp_ref[...])
    o_ref[...] = _layer_norm(DEEPNORM_ALPHA * x1_ref[...] + ffn_ref[...] + ple, lnw_ref[...], lnb_ref[...])


def _final(x1, x1b, ffn, p, w_gate, w_proj, ln_w, ln_b):
    t = x1.shape[0]
    tm = min(FIN_TM, t)
    consts = (w_gate, w_proj, ln_w.reshape(1, -1), ln_b.reshape(1, -1))
    row = lambda w: pl.BlockSpec((tm, w), lambda i: (i, 0))
    return pl.pallas_call(
        _final_kernel,
        grid=(t // tm,),
        in_specs=[row(D_MODEL), row(D_MODEL), row(D_MODEL), row(p.shape[1])] + [_const_spec(c) for c in consts],
        out_specs=row(D_MODEL),
        out_shape=jax.ShapeDtypeStruct((t, D_MODEL), F32),
        compiler_params=_cparams("parallel"),
    )(x1, x1b, ffn, p, *consts)


def _layer(x, p, w_in, rwkv_mu, rwkv_w0, rwkv_w_lora1, rwkv_w_lora2, rwkv_a0, rwkv_a_lora1,
           rwkv_a_lora2, rwkv_g_lora1, rwkv_g_lora2, rwkv_k_k, rwkv_k_a, rwkv_r_k, rwkv_lnx_w,
           rwkv_lnx_b, w_up_a, w_up_b, w_o, ln1_w, ln1_b, peer_w_q, peer_sub_keys, peer_u, peer_v,
           ple_w_gate, ple_w_proj, ln2_w, ln2_b):
    bf = lambda w: w.astype(BF16)
    n_sb = 3 * MIX_WIDTH
    n_rw = 4 * MIX_WIDTH
    qkv = _proj(x, w_in, 0, n_sb, BF16)
    rw = _proj(x, w_in, n_sb, n_rw, BF16)
    gates = _proj(x, w_in, n_sb + n_rw, 2 * D_MODEL, BF16)
    y_a = _sb_attention(qkv)
    r, k, v, lw, a, g = _rwkv_pre(rw, rwkv_mu, rwkv_w0, rwkv_w_lora1, rwkv_w_lora2, rwkv_a0,
                                  rwkv_a_lora1, rwkv_a_lora2, rwkv_g_lora1, rwkv_g_lora2)
    rp, y0, bon, pm, qm = _rwkv_chunk(r, k, v, lw, a, rwkv_k_k, rwkv_k_a, rwkv_r_k.reshape(-1))
    y_b = _rwkv_state(rp, y0, pm, qm)
    x1, x1b = _mix(y_a, y_b, bon, g, gates, x, rwkv_lnx_w, rwkv_lnx_b, bf(w_up_a), bf(w_up_b), bf(w_o),
                   ln1_w, ln1_b)
    e0, e1, gate = _route(x1b, bf(peer_w_q), bf(peer_sub_keys))
    act = _peer_score(x1b, peer_u, e0, e1, gate)
    ffn = _peer_value(_peer_gate(e0, e1, act), bf(peer_v))
    return _final(x1, x1b, ffn, p, bf(ple_w_gate), bf(ple_w_proj), ln2_w, ln2_b)


def kernel(x, p, w_in, rwkv_mu, rwkv_w0, rwkv_w_lora1, rwkv_w_lora2, rwkv_a0, rwkv_a_lora1, rwkv_a_lora2, rwkv_g_lora1, rwkv_g_lora2, rwkv_k_k, rwkv_k_a, rwkv_r_k, rwkv_lnx_w, rwkv_lnx_b, w_up_a, w_up_b, w_o, ln1_w, ln1_b, peer_w_q, peer_sub_keys, peer_u, peer_v, ple_w_gate, ple_w_proj, ln2_w, ln2_b):
    bsz, t, d = x.shape
    depth = w_in.shape[0]
    xs = x.reshape(bsz * t, d)
    assert bsz == 1, "token shift / attention / scan treat the flattened rows as one sequence"
    for i in range(depth):
        xs = _layer(xs, p[i].reshape(bsz * t, -1), w_in[i], rwkv_mu[i], rwkv_w0[i], rwkv_w_lora1[i],
                    rwkv_w_lora2[i], rwkv_a0[i], rwkv_a_lora1[i], rwkv_a_lora2[i], rwkv_g_lora1[i],
                    rwkv_g_lora2[i], rwkv_k_k[i], rwkv_k_a[i], rwkv_r_k[i], rwkv_lnx_w[i], rwkv_lnx_b[i],
                    w_up_a[i], w_up_b[i], w_o[i], ln1_w[i], ln1_b[i], peer_w_q[i], peer_sub_keys[i],
                    peer_u[i], peer_v[i], ple_w_gate[i], ple_w_proj[i], ln2_w[i], ln2_b[i])
    return xs.reshape(bsz, t, d)
```

```python
import functools

import jax
import jax.numpy as jnp
from jax import lax
from jax.experimental import pallas as pl
from jax.experimental.pallas import tpu as pltpu

F32 = jnp.float32
BF16 = jnp.bfloat16
I32 = jnp.int32

D_MODEL = 2048
HEAD_DIM = 64
N_HEADS = 16
MIX_WIDTH = N_HEADS * HEAD_DIM
PEER_HEADS = 8
PEER_TOPK = 16
N_KEYS = 128
N_EXPERTS = N_KEYS * N_KEYS
N_SEL = PEER_HEADS * PEER_TOPK
GN_EPS = 64e-5
LN_EPS = 1e-5
DEEPNORM_ALPHA = 2.0 ** 0.25

LANES = 128
VMEM_LIMIT = 56 * 1024 * 1024

MM_TM, MM_TN = 1024, 1024
SB_BLK = 128
SB_SUBS = 4
SB_WIN = 2 * LANES
SB_DONE = -88.0
RW_TM = 256
RW_C = 64
RW_NCH = 16
RW_NCS = 8
MIX_TM = 256
ROUTE_TM = 256
PH_TM, PH_TN = 1024, 2048
PH_SUB = 256
PG_TT = 128
PG_GRP = 16
PV_TM, PV_NE0 = 1024, 16
FIN_TM = 512

assert 2 * RW_C == LANES and RW_C == HEAD_DIM

NN = (((1,), (0,)), ((), ()))
NT = (((1,), (1,)), ((), ()))
TN = (((0,), (0,)), ((), ()))


def _cparams(*sem):
    return pltpu.CompilerParams(dimension_semantics=tuple(sem), vmem_limit_bytes=VMEM_LIMIT)


def _dg(a, b, dims=NN):
    return lax.dot_general(a, b, dims, preferred_element_type=F32)


def _split2(x):
    hi = x.astype(BF16)
    lo = (x - hi.astype(F32)).astype(BF16)
    return hi, lo


def _mm1(a, b, dims=NN):
    return _dg(a.astype(BF16), b.astype(BF16), dims)


def _mm_sum_lhs(m_bf16, x):
    hi, lo = _split2(x)
    return _dg(m_bf16, hi) + _dg(m_bf16, lo)


def _softplus(z):
    return jnp.maximum(z, 0.0) + jnp.log(1.0 + jnp.exp(-jnp.abs(z)))


def _sigmoid(z):
    return 0.5 * jnp.tanh(0.5 * z) + 0.5


def _layer_norm(x, g, b):
    mu = jnp.mean(x, axis=-1, keepdims=True)
    d = x - mu
    var = jnp.mean(d * d, axis=-1, keepdims=True)
    return d * lax.rsqrt(var + LN_EPS) * g + b


def _proj_kernel(a_ref, b_ref, o_ref, bw_ref):
    @pl.when(pl.program_id(1) == 0)
    def _():
        bw_ref[...] = b_ref[...].astype(BF16)

    o_ref[...] = _dg(a_ref[...].astype(BF16), bw_ref[...]).astype(o_ref.dtype)


def _proj(a, b, col0, n, out_dtype):
    m, k = a.shape
    tm, tn = min(MM_TM, m), MM_TN
    assert col0 % tn == 0 and n % tn == 0
    off = col0 // tn
    return pl.pallas_call(
        _proj_kernel,
        grid=(n // tn, m // tm),
        in_specs=[pl.BlockSpec((tm, k), lambda j, i: (i, 0)),
                  pl.BlockSpec((k, tn), lambda j, i: (0, j + off))],
        out_specs=pl.BlockSpec((tm, tn), lambda j, i: (i, j)),
        out_shape=jax.ShapeDtypeStruct((m, n), out_dtype),
        scratch_shapes=[pltpu.VMEM((k, tn), BF16)],
        compiler_params=_cparams("parallel", "arbitrary"),
    )(a, b)


def _sb_kernel(q_ref, k_ref, v_ref, o_ref, acc_ref, carry_ref):
    i = pl.program_id(1)
    blk = SB_BLK
    n_sub = q_ref.shape[0] // blk
    win = SB_WIN
    lane = lax.broadcasted_iota(I32, (1, LANES), 1)
    row = lax.broadcasted_iota(I32, (blk, win), 0)
    col = lax.broadcasted_iota(I32, (blk, win), 1)
    r_io = lax.broadcasted_iota(I32, (win, win), 0)
    c_io = lax.broadcasted_iota(I32, (win, win), 1)
    later_mat = jnp.where(r_io > c_io, 1.0, 0.0).astype(BF16)
    scale = jnp.asarray(HEAD_DIM ** -0.5, BF16)
    chains = [(s, h) for s in range(n_sub) for h in range(2)]
    qh = []
    for s, h in chains:
        q = q_ref[s * blk:(s + 1) * blk, :]
        qh.append(jnp.where((lane < HEAD_DIM) == (h == 0), q, jnp.zeros_like(q)) * scale)
    first_end = [(i * n_sub + s + 1) * blk for s in range(n_sub)]
    acc_ref[...] = jnp.zeros_like(acc_ref)
    carry_ref[...] = jnp.zeros_like(carry_ref)

    def step(it):
        kwin, vwin, valid = [], [], []
        for s in range(n_sub):
            end = first_end[s] - it * win
            start = pl.multiple_of(jnp.maximum(end - win, 0), blk)
            kwin.append(k_ref[pl.ds(start, win), :])
            vwin.append(v_ref[pl.ds(start, win), :])
            valid.append((start + col) < jnp.minimum(first_end[s] - blk + row, end))
        z = [_dg(qh[c], kwin[s], NT) for c, (s, h) in enumerate(chains)]
        sp = [_softplus(x) for x in z]
        log_keep = [jnp.where(valid[s], -sp[c], 0.0).astype(BF16) for c, (s, h) in enumerate(chains)]
        carry = [carry_ref[c] for c in range(len(chains))]
        log_later = [_dg(log_keep[c], later_mat) + jnp.concatenate([carry[c]] * (win // LANES), axis=1)
                     for c in range(len(chains))]
        w = [jnp.where(valid[s], jnp.exp(z[c] - sp[c] + log_later[c]), 0.0).astype(BF16)
             for c, (s, h) in enumerate(chains)]
        pv = [_dg(w[c], vwin[s]) for c, (s, h) in enumerate(chains)]
        for c in range(len(chains)):
            acc_ref[c] += pv[c]
            carry_ref[c] = jnp.broadcast_to(log_later[c][:, 0:1] + log_keep[c][:, 0:1], (blk, LANES))
        return it + 1

    def cond(it):
        alive = None
        for s in range(n_sub):
            more = jnp.logical_and(first_end[s] - it * win > 0, jnp.max(carry_ref[2 * s:2 * s + 2]) > SB_DONE)
            alive = more if alive is None else jnp.logical_or(alive, more)
        return alive

    lax.while_loop(cond, step, step(0))
    for s in range(n_sub):
        o_ref[s * blk:(s + 1) * blk, :] = jnp.where(lane < HEAD_DIM, acc_ref[2 * s], acc_ref[2 * s + 1]).astype(o_ref.dtype)


def _sb_attention(qkv):
    t = qkv.shape[0]
    blk = SB_BLK * SB_SUBS
    n_pairs = MIX_WIDTH // LANES
    return pl.pallas_call(
        _sb_kernel,
        grid=(n_pairs, t // blk),
        in_specs=[pl.BlockSpec((blk, LANES), lambda hp, i: (i, hp)),
                  pl.BlockSpec((t, LANES), lambda hp, i: (0, n_pairs + hp)),
                  pl.BlockSpec((t, LANES), lambda hp, i: (0, 2 * n_pairs + hp))],
        out_specs=pl.BlockSpec((blk, LANES), lambda hp, i: (i, hp)),
        out_shape=jax.ShapeDtypeStruct((t, MIX_WIDTH), BF16),
        scratch_shapes=[pltpu.VMEM((2 * SB_SUBS, SB_BLK, LANES), F32), pltpu.VMEM((2 * SB_SUBS, SB_BLK, LANES), F32)],
        compiler_params=_cparams("parallel", "parallel"),
    )(qkv, qkv, qkv)


def _rwkv_pre_kernel(cur_ref, prev_ref, mu_ref, w0_ref, wl1_ref, wl2_ref, a0_ref, al1_ref, al2_ref,
                     gl1_ref, gl2_ref, r_ref, k_ref, v_ref, lw_ref, a_ref, g_ref):
    i = pl.program_id(0)
    tm = cur_ref.shape[0]
    wd = MIX_WIDTH
    first_row = lax.broadcasted_iota(I32, (tm, wd), 0) == 0
    keep_prev = jnp.where(i == 0, 0.0, 1.0)

    n_prev = prev_ref.shape[0]

    def shifted(col):
        z = cur_ref[:, col * wd:(col + 1) * wd].astype(F32)
        last = prev_ref[n_prev - 1:n_prev, col * wd:(col + 1) * wd].astype(F32) * keep_prev
        prev = jnp.where(first_row, last, pltpu.roll(z, 1, 0))
        return z, prev - z

    z, d = shifted(0)
    r_ref[...] = (z + d * mu_ref[0:1, :]).astype(r_ref.dtype)
    z, d = shifted(1)
    k_ref[...] = (z + d * mu_ref[1:2, :]).astype(k_ref.dtype)
    z, d = shifted(2)
    v_ref[...] = (z + d * mu_ref[2:3, :]).astype(v_ref.dtype)
    z, d = shifted(3)
    xw = z + d * mu_ref[3:4, :]
    xa = z + d * mu_ref[4:5, :]
    xg = z + d * mu_ref[5:6, :]
    w = w0_ref[...] + _mm1(jnp.tanh(_mm1(xw, wl1_ref[...])), wl2_ref[...])
    w = -_softplus(-w) - 0.5
    lw_ref[...] = -jnp.exp(w)
    a_ref[...] = _sigmoid(a0_ref[...] + _mm1(_mm1(xa, al1_ref[...]), al2_ref[...])).astype(a_ref.dtype)
    g_ref[...] = _mm1(_sigmoid(_mm1(xg, gl1_ref[...])), gl2_ref[...]).astype(g_ref.dtype)


def _pad_to(x, axis, size):
    pad = [(0, 0)] * x.ndim
    pad[axis] = (0, size - x.shape[axis])
    return jnp.pad(x, pad)


def _rwkv_pre(rw, mu, w0, wl1, wl2, a0, al1, al2, gl1, gl2):
    t = rw.shape[0]
    tm = min(RW_TM, t)
    wd = MIX_WIDTH
    lo = LANES

    def lora_pair(l1, l2):
        n = -(-l1.shape[1] // lo) * lo
        return _pad_to(l1, 1, n).astype(BF16), _pad_to(l2, 0, n).astype(BF16)

    wl1, wl2 = lora_pair(wl1, wl2)
    al1, al2 = lora_pair(al1, al2)
    gl1, gl2 = lora_pair(gl1, gl2)
    full = lambda x: pl.BlockSpec(x.shape, lambda i: (0,) * x.ndim)
    row = lambda: pl.BlockSpec((tm, wd), lambda i: (i, 0))
    consts = (mu, w0.reshape(1, wd), wl1, wl2, a0.reshape(1, wd), al1, al2, gl1, gl2)
    n_prev = 32 // rw.dtype.itemsize
    out_dtypes = (BF16, BF16, BF16, F32, BF16, BF16)
    return pl.pallas_call(
        _rwkv_pre_kernel,
        grid=(t // tm,),
        in_specs=[pl.BlockSpec((tm, 4 * wd), lambda i: (i, 0)),
                  pl.BlockSpec((n_prev, 4 * wd), lambda i: (jnp.maximum(i * (tm // n_prev) - 1, 0), 0))]
                 + [full(c) for c in consts],
        out_specs=[row() for _ in range(6)],
        out_shape=[jax.ShapeDtypeStruct((t, wd), dt) for dt in out_dtypes],
        compiler_params=_cparams("parallel"),
    )(rw, rw, *consts)


def _rwkv_chunk_kernel(r_ref, k_ref, v_ref, lw_ref, a_ref, kkp_ref, kap_ref, rkp_ref,
                       rp_ref, y0_ref, bon_ref, p_ref, q_ref):
    c_len = RW_C
    n2 = 2 * c_len
    n_chunks = r_ref.shape[0] // c_len
    r2 = lax.broadcasted_iota(I32, (n2, LANES), 0)
    c2 = lax.broadcasted_iota(I32, (n2, LANES), 1)
    same = (r2 < c_len) == (c2 < HEAD_DIM)
    t_r = jnp.where(r2 < c_len, r2, r2 - c_len)
    t_c = jnp.where(c2 < HEAD_DIM, c2, c2 - HEAD_DIM)
    strict = jnp.logical_and(same, t_r > t_c)
    incl = jnp.logical_and(same, t_r >= t_c)
    eye = jnp.where(r2 == c2, 1.0, 0.0).astype(F32)
    group_ones = jnp.where(same, 1.0, 0.0).astype(BF16)
    lr = lax.broadcasted_iota(I32, (c_len, c_len), 0)
    lc = lax.broadcasted_iota(I32, (c_len, c_len), 1)
    cum_mat = jnp.where(lr >= lc, 1.0, 0.0).astype(BF16)
    kkp, kap, rkp = kkp_ref[...], kap_ref[...], rkp_ref[...]

    def stack(z):
        return jnp.where(same, jnp.concatenate([z, z], axis=0), 0.0)

    def unstack(zs):
        return zs[:c_len] + zs[c_len:]

    cs = range(n_chunks)
    rows = [pl.ds(c * c_len, c_len) for c in cs]
    r = [r_ref[rw, :].astype(F32) for rw in rows]
    kr = [k_ref[rw, :].astype(F32) for rw in rows]
    v = [v_ref[rw, :].astype(F32) for rw in rows]
    lw = [lw_ref[rw, :] for rw in rows]
    a = [a_ref[rw, :].astype(F32) for rw in rows]
    kk = [x * kkp for x in kr]
    cum = [_mm_sum_lhs(cum_mat, x) for x in lw]
    ssq = [_dg((x * x).astype(BF16), group_ones) for x in kk]
    km = [kr[c] * (1.0 + (a[c] - 1.0) * kap) for c in cs]
    bsum = [_dg((r[c] * km[c] * rkp).astype(BF16), group_ones) for c in cs]
    kk = [kk[c] / jnp.maximum(jnp.sqrt(ssq[c]), 1e-12) for c in cs]
    bv = [kk[c] * a[c] for c in cs]
    last = [x[c_len - 1:c_len, :] for x in cum]
    rt = [stack(r[c] * jnp.exp(cum[c])) for c in cs]
    at_b = [stack(-kk[c] * jnp.exp(cum[c] - lw[c])).astype(BF16) for c in cs]
    rt_b = [x.astype(BF16) for x in rt]
    g_inv = [jnp.exp(-x) for x in cum]
    btkt = [jnp.concatenate([stack(bv[c] * g_inv[c]), stack(km[c] * g_inv[c])], axis=0).astype(BF16) for c in cs]
    m_a = [_dg(at_b[c], btkt[c], NT) for c in cs]
    m_r = [_dg(rt_b[c], btkt[c], NT) for c in cs]
    vs_b = [stack(x).astype(BF16) for x in v]
    g_end = [jnp.exp(last[c] - cum[c]) for c in cs]
    bd_b = [stack(bv[c] * g_end[c]).astype(BF16) for c in cs]
    kd_b = [stack(km[c] * g_end[c]).astype(BF16) for c in cs]
    m_ab = [jnp.where(strict, x[:, :n2], 0.0) for x in m_a]
    m_ak = [jnp.where(strict, x[:, n2:], 0.0).astype(BF16) for x in m_a]
    m_rb = [jnp.where(incl, x[:, :n2], 0.0).astype(BF16) for x in m_r]
    m_rk = [jnp.where(incl, x[:, n2:], 0.0).astype(BF16) for x in m_r]
    akv = [_dg(m_ak[c], vs_b[c]) for c in cs]
    rkv = [_dg(m_rk[c], vs_b[c]) for c in cs]
    kdv = [_dg(kd_b[c], vs_b[c], TN) for c in cs]
    tinv = [eye + x for x in m_ab]
    npow = [x.astype(BF16) for x in m_ab]
    for _ in range(5):
        npow = [_dg(x, x).astype(BF16) for x in npow]
        tinv = [tinv[c] + _dg(tinv[c].astype(BF16), npow[c]) for c in cs]
    au_b = [_dg(tinv[c].astype(BF16), jnp.concatenate([at_b[c], akv[c].astype(BF16)], axis=1)).astype(BF16)
            for c in cs]
    ry = [_dg(m_rb[c], au_b[c]) for c in cs]
    pq = [_dg(bd_b[c], au_b[c], TN) for c in cs]
    for c in cs:
        prow = pl.ds(c * n2, n2)
        rp_ref[rows[c], :] = unstack(rt[c] + ry[c][:, :LANES]).astype(rp_ref.dtype)
        y0_ref[rows[c], :] = unstack(ry[c][:, LANES:] + rkv[c]).astype(y0_ref.dtype)
        p_ref[prow, :] = (eye * jnp.exp(last[c]) + pq[c][:, :LANES]).astype(p_ref.dtype)
        q_ref[prow, :] = (pq[c][:, LANES:] + kdv[c]).astype(q_ref.dtype)
        bon_ref[rows[c], :] = (bsum[c] * v[c]).astype(bon_ref.dtype)


def _rwkv_chunk(r, k, v, lw, a, k_k, k_a, r_k):
    t = r.shape[0]
    rb = min(RW_C * RW_NCH, t)
    n_pairs = MIX_WIDTH // LANES
    row = lambda: pl.BlockSpec((rb, LANES), lambda i, hp: (i, hp))
    par = lambda: pl.BlockSpec((1, LANES), lambda i, hp: (0, hp))
    mat = lambda: pl.BlockSpec((2 * rb, LANES), lambda i, hp: (i, hp))
    return pl.pallas_call(
        _rwkv_chunk_kernel,
        grid=(t // rb, n_pairs),
        in_specs=[row() for _ in range(5)] + [par() for _ in range(3)],
        out_specs=[row(), row(), row(), mat(), mat()],
        out_shape=[jax.ShapeDtypeStruct((t, MIX_WIDTH), BF16), jax.ShapeDtypeStruct((t, MIX_WIDTH), BF16),
                   jax.ShapeDtypeStruct((t, MIX_WIDTH), BF16), jax.ShapeDtypeStruct((2 * t, MIX_WIDTH), BF16),
                   jax.ShapeDtypeStruct((2 * t, MIX_WIDTH), BF16)],
        compiler_params=_cparams("parallel", "parallel"),
    )(r, k, v, lw, a, k_k.reshape(1, -1), k_a.reshape(1, -1), r_k.reshape(1, -1))


def _rwkv_state_kernel(rp_ref, y0_ref, p_ref, q_ref, o_ref, st_ref):
    @pl.when(pl.program_id(0) == 0)
    def _():
        st_ref[...] = jnp.zeros_like(st_ref)

    c_len = RW_C
    pairs = range(MIX_WIDTH // LANES)
    lanes = [slice(hp * LANES, (hp + 1) * LANES) for hp in pairs]
    st = [st_ref[hp] for hp in pairs]
    for c in range(rp_ref.shape[0] // c_len):
        rows = slice(c * c_len, (c + 1) * c_len)
        mrows = slice(2 * c * c_len, 2 * (c + 1) * c_len)
        st_b = [x.astype(BF16) for x in st]
        st = [_dg(p_ref[mrows, lanes[hp]], st_b[hp]) + q_ref[mrows, lanes[hp]] for hp in pairs]
        for hp in pairs:
            o_ref[rows, lanes[hp]] = _dg(rp_ref[rows, lanes[hp]], st_b[hp]) + y0_ref[rows, lanes[hp]]
    for hp in pairs:
        st_ref[hp] = st[hp]


def _rwkv_state(rp, y0, p, q):
    t = rp.shape[0]
    rb = min(RW_C * RW_NCS, t)
    wd = MIX_WIDTH
    row = lambda: pl.BlockSpec((rb, wd), lambda i: (i, 0))
    mat = lambda: pl.BlockSpec((2 * rb, wd), lambda i: (i, 0))
    return pl.pallas_call(
        _rwkv_state_kernel,
        grid=(t // rb,),
        in_specs=[row(), row(), mat(), mat()],
        out_specs=row(),
        out_shape=jax.ShapeDtypeStruct((t, wd), F32),
        scratch_shapes=[pltpu.VMEM((wd // LANES, LANES, LANES), F32)],
        compiler_params=_cparams("arbitrary"),
    )(rp, y0, p, q)


def _mix_kernel(ya_ref, y_ref, bon_ref, g_ref, gate_ref, x_ref, gnw_ref, gnb_ref, wa_ref, wb_ref, wo_ref,
                lnw_ref, lnb_ref, x1_ref, x1b_ref):
    r2 = lax.broadcasted_iota(I32, (LANES, LANES), 0)
    c2 = lax.broadcasted_iota(I32, (LANES, LANES), 1)
    group_ones = jnp.where((r2 < HEAD_DIM) == (c2 < HEAD_DIM), 1.0, 0.0).astype(BF16)
    inv_n = 1.0 / HEAD_DIM
    pieces = []
    for hp in range(MIX_WIDTH // LANES):
        ls = slice(hp * LANES, (hp + 1) * LANES)
        y = y_ref[:, ls]
        d = y - _dg(y.astype(BF16), group_ones) * inv_n
        var = _dg((d * d).astype(BF16), group_ones) * inv_n
        yn = d * lax.rsqrt(var + GN_EPS) * gnw_ref[:, ls] + gnb_ref[:, ls]
        pieces.append(((yn + bon_ref[:, ls]) * g_ref[:, ls]).astype(BF16))
    ya = _dg(ya_ref[...], wa_ref[...])
    yb = _dg(jnp.concatenate(pieces, axis=1), wb_ref[...])
    ga = _sigmoid(gate_ref[:, :D_MODEL].astype(F32))
    gb = _sigmoid(gate_ref[:, D_MODEL:].astype(F32))
    mixed = _dg((ga * ya + gb * yb).astype(BF16), wo_ref[...])
    x1 = _layer_norm(DEEPNORM_ALPHA * x_ref[...] + mixed, lnw_ref[...], lnb_ref[...])
    x1_ref[...] = x1
    x1b_ref[...] = x1.astype(BF16)


def _const_spec(x):
    return pl.BlockSpec(x.shape, lambda *_: (0,) * x.ndim, pipeline_mode=pl.Buffered(1))


def _mix(ya, y, bon, g, gates, x, gn_w, gn_b, w_up_a, w_up_b, w_o, ln_w, ln_b):
    t = x.shape[0]
    tm = min(MIX_TM, t)
    consts = (gn_w.reshape(1, -1), gn_b.reshape(1, -1), w_up_a, w_up_b, w_o, ln_w.reshape(1, -1), ln_b.reshape(1, -1))
    row = lambda w: pl.BlockSpec((tm, w), lambda i: (i, 0))
    return pl.pallas_call(
        _mix_kernel,
        grid=(t // tm,),
        in_specs=[row(MIX_WIDTH), row(MIX_WIDTH), row(MIX_WIDTH), row(MIX_WIDTH), row(2 * D_MODEL), row(D_MODEL)]
                 + [_const_spec(c) for c in consts],
        out_specs=[pl.BlockSpec((tm, D_MODEL), lambda i: (i, 0))] * 2,
        out_shape=[jax.ShapeDtypeStruct((t, D_MODEL), F32), jax.ShapeDtypeStruct((t, D_MODEL), BF16)],
        compiler_params=_cparams("parallel"),
    )(ya, y, bon, g, gates, x, *consts)


def _topk_rows(s, ids, k):
    big = jnp.asarray(1e9, F32)
    vals, idxs = [], []
    for _ in range(k):
        m = jnp.max(s, axis=0, keepdims=True)
        ix = jnp.min(jnp.where(s == m, ids, big), axis=0, keepdims=True)
        vals.append(m)
        idxs.append(ix)
        s = jnp.where(ids == ix, -jnp.inf, s)
    return jnp.concatenate(vals, axis=0), jnp.concatenate(idxs, axis=0)


def _route_kernel(x_ref, wq_ref, keys_ref, e0_ref, e1_ref, gate_ref):
    tm = x_ref.shape[0]
    topk = PEER_TOPK
    q = _dg(x_ref[...], wq_ref[...])
    qb = q.astype(BF16)
    half = N_KEYS
    key_ids = lax.broadcasted_iota(I32, (N_KEYS, tm), 0).astype(F32)
    sub8 = lax.broadcasted_iota(I32, (8, tm), 0).astype(F32)
    sub16 = lax.broadcasted_iota(I32, (topk, tm), 0).astype(F32)
    cand_ids = jnp.concatenate([sub16] + [a * topk + sub8 for a in range(1, 8)] + [(sub8 + 8.0) * topk], axis=0)
    e0s, e1s, gates = [], [], []
    for h in range(PEER_HEADS):
        tops = []
        for c in range(2):
            qs = qb[:, (2 * h + c) * half:(2 * h + c + 1) * half]
            s = _dg(keys_ref[c], qs, NT)
            tops.append(_topk_rows(s, key_ids, topk))
        (s0, i0), (s1, i1) = tops
        cand = jnp.concatenate([s0[0:1, :] + s1] + [s0[a:a + 1, :] + s1[:8, :] for a in range(1, 8)]
                               + [s0[8:, :] + s1[0:1, :]], axis=0)
        best, pos = _topk_rows(cand, cand_ids, topk)
        a_sel = jnp.floor(pos * (1.0 / topk))
        b_sel = pos - a_sel * topk
        e0 = jnp.zeros_like(pos)
        e1 = jnp.zeros_like(pos)
        for j in range(topk):
            e0 = jnp.where(a_sel == j, i0[j:j + 1, :], e0)
            e1 = jnp.where(b_sel == j, i1[j:j + 1, :], e1)
        ex = jnp.exp(best - best[0:1, :])
        gates.append(ex / jnp.sum(ex, axis=0, keepdims=True))
        e0s.append(e0)
        e1s.append(e1)
    e0_ref[...] = jnp.concatenate(e0s, axis=0).T.astype(I32)
    e1_ref[...] = jnp.concatenate(e1s, axis=0).T.astype(I32)
    gate_ref[...] = jnp.concatenate(gates, axis=0).T


def _route(x1b, w_q, sub_keys):
    t = x1b.shape[0]
    tm = min(ROUTE_TM, t)
    out = lambda: pl.BlockSpec((tm, N_SEL), lambda i: (i, 0))
    return pl.pallas_call(
        _route_kernel,
        grid=(t // tm,),
        in_specs=[pl.BlockSpec((tm, D_MODEL), lambda i: (i, 0)), _const_spec(w_q), _const_spec(sub_keys)],
        out_specs=[out(), out(), out()],
        out_shape=[jax.ShapeDtypeStruct((t, N_SEL), I32), jax.ShapeDtypeStruct((t, N_SEL), I32),
                   jax.ShapeDtypeStruct((t, N_SEL), F32)],
        compiler_params=_cparams("parallel"),
    )(x1b, w_q, sub_keys)


def _gelu_exact(x):
    return 0.5 * x * (1.0 + lax.erf(x * (2.0 ** -0.5)))


def _peer_score_kernel(x_ref, u_ref, e0_ref, e1_ref, gate_ref, o_ref, acc_ref):
    j = pl.program_id(1)

    @pl.when(j == 0)
    def _():
        acc_ref[...] = jnp.zeros_like(acc_ref)

    x = x_ref[...]
    e0, e1 = e0_ref[...], e1_ref[...]
    blocks = u_ref.shape[0] // N_KEYS
    per_dot = PH_SUB // N_KEYS
    acc = acc_ref[...]
    for s in range(u_ref.shape[0] // PH_SUB):
        h = _dg(x, u_ref[s * PH_SUB:(s + 1) * PH_SUB, :].astype(BF16), NT)
        for b in range(per_dot):
            picked = jnp.take_along_axis(h[:, b * N_KEYS:(b + 1) * N_KEYS], e1, axis=1)
            acc = acc + jnp.where(e0 == j * blocks + s * per_dot + b, picked, 0.0)
    acc_ref[...] = acc

    @pl.when(j == pl.num_programs(1) - 1)
    def _():
        o_ref[...] = _gelu_exact(acc_ref[...]) * gate_ref[...]


def _peer_score(x1b, u_bf16, e0, e1, gate):
    t = x1b.shape[0]
    tm = min(PH_TM, t)
    tok = lambda: pl.BlockSpec((tm, N_SEL), lambda i, j: (i, 0))
    return pl.pallas_call(
        _peer_score_kernel,
        grid=(t // tm, N_EXPERTS // PH_TN),
        in_specs=[pl.BlockSpec((tm, D_MODEL), lambda i, j: (i, 0)),
                  pl.BlockSpec((PH_TN, D_MODEL), lambda i, j: (j, 0)), tok(), tok(), tok()],
        out_specs=tok(),
        out_shape=jax.ShapeDtypeStruct((t, N_SEL), F32),
        scratch_shapes=[pltpu.VMEM((tm, N_SEL), F32)],
        compiler_params=_cparams("parallel", "arbitrary"),
    )(x1b, u_bf16, e0, e1, gate)


def _peer_gate_kernel(e0_ref, e1_ref, act_ref, o_ref):
    sub = lax.broadcasted_iota(I32, (N_KEYS, N_SEL), 0)

    def body(grp, carry):
        mats = []
        for u in range(PG_GRP):
            row = pl.ds(grp * PG_GRP + u, 1)
            left = jnp.where(e0_ref[row, :] == sub, act_ref[row, :], 0.0).astype(BF16)
            right = jnp.where(e1_ref[row, :] == sub, 1.0, 0.0).astype(BF16)
            mats.append(_dg(left, right, NT))
        o_ref[grp] = jnp.swapaxes(jnp.stack(mats, axis=0), 0, 1).astype(o_ref.dtype)
        return carry

    lax.fori_loop(0, o_ref.shape[0], body, 0)


def _peer_gate(e0, e1, act):
    t = e0.shape[0]
    tt = min(PG_TT, t)
    tok = lambda: pl.BlockSpec((tt, N_SEL), lambda i: (i, 0))
    return pl.pallas_call(
        _peer_gate_kernel,
        grid=(t // tt,),
        in_specs=[tok(), tok(), tok()],
        out_specs=pl.BlockSpec((tt // PG_GRP, N_KEYS, PG_GRP, N_KEYS), lambda i: (i, 0, 0, 0)),
        out_shape=jax.ShapeDtypeStruct((t // PG_GRP, N_KEYS, PG_GRP, N_KEYS), BF16),
        compiler_params=_cparams("parallel"),
    )(e0, e1, act)


def _peer_value_kernel(g_ref, v_ref, o_ref):
    @pl.when(pl.program_id(1) == 0)
    def _():
        o_ref[...] = jnp.zeros_like(o_ref)

    tm = o_ref.shape[0]
    lhs = jnp.concatenate([g_ref[:, e, :, :].reshape(tm, N_KEYS) for e in range(g_ref.shape[1])], axis=1)
    o_ref[...] += _dg(lhs, v_ref[...])


def _peer_value(g4, v_bf16):
    t = g4.shape[0] * PG_GRP
    tm = min(PV_TM, t)
    return pl.pallas_call(
        _peer_value_kernel,
        grid=(t // tm, N_KEYS // PV_NE0),
        in_specs=[pl.BlockSpec((tm // PG_GRP, PV_NE0, PG_GRP, N_KEYS), lambda i, k: (i, k, 0, 0)),
                  pl.BlockSpec((PV_NE0 * N_KEYS, D_MODEL), lambda i, k: (k, 0))],
        out_specs=pl.BlockSpec((tm, D_MODEL), lambda i, k: (i, 0)),
        out_shape=jax.ShapeDtypeStruct((t, D_MODEL), F32),
        compiler_params=_cparams("parallel", "arbitrary"),
    )(g4, v_bf16)


def _final_kernel(x1_ref, x1b_ref, ffn_ref, p_ref, wg_ref, wp_ref, lnw_ref, lnb_ref, o_ref):
    ple = _sigmoid(_dg(x1b_ref[...], wg_ref[...])) * _dg(p_ref[...].astype(BF16), wp_ref[...])
    o_ref[...] = _layer_norm(DEEPNORM_ALPHA * x1_ref[...] + ffn_ref[...] + ple, lnw_ref[...], lnb_ref[...])


def _final(x1, x1b, ffn, p, w_gate, w_proj, ln_w, ln_b):
    t = x1.shape[0]
    tm = min(FIN_TM, t)
    consts = (w_gate, w_proj, ln_w.reshape(1, -1), ln_b.reshape(1, -1))
    row = lambda w: pl.BlockSpec((tm, w), lambda i: (i, 0))
    return pl.pallas_call(
        _final_kernel,
        grid=(t // tm,),
        in_specs=[row(D_MODEL), row(D_MODEL), row(D_MODEL), row(p.shape[1])] + [_const_spec(c) for c in consts],
        out_specs=row(D_MODEL),
        out_shape=jax.ShapeDtypeStruct((t, D_MODEL), F32),
        compiler_params=_cparams("parallel"),
    )(x1, x1b, ffn, p, *consts)


def _layer(x, p, w_in, rwkv_mu, rwkv_w0, rwkv_w_lora1, rwkv_w_lora2, rwkv_a0, rwkv_a_lora1,
           rwkv_a_lora2, rwkv_g_lora1, rwkv_g_lora2, rwkv_k_k, rwkv_k_a, rwkv_r_k, rwkv_lnx_w,
           rwkv_lnx_b, w_up_a, w_up_b, w_o, ln1_w, ln1_b, peer_w_q, peer_sub_keys, peer_u, peer_v,
           ple_w_gate, ple_w_proj, ln2_w, ln2_b):
    bf = lambda w: w.astype(BF16)
    n_sb = 3 * MIX_WIDTH
    n_rw = 4 * MIX_WIDTH
    qkv = _proj(x, w_in, 0, n_sb, BF16)
    rw = _proj(x, w_in, n_sb, n_rw, BF16)
    gates = _proj(x, w_in, n_sb + n_rw, 2 * D_MODEL, BF16)
    y_a = _sb_attention(qkv)
    r, k, v, lw, a, g = _rwkv_pre(rw, rwkv_mu, rwkv_w0, rwkv_w_lora1, rwkv_w_lora2, rwkv_a0,
                                  rwkv_a_lora1, rwkv_a_lora2, rwkv_g_lora1, rwkv_g_lora2)
    rp, y0, bon, pm, qm = _rwkv_chunk(r, k, v, lw, a, rwkv_k_k, rwkv_k_a, rwkv_r_k.reshape(-1))
    y_b = _rwkv_state(rp, y0, pm, qm)
    x1, x1b = _mix(y_a, y_b, bon, g, gates, x, rwkv_lnx_w, rwkv_lnx_b, bf(w_up_a), bf(w_up_b), bf(w_o),
                   ln1_w, ln1_b)
    e0, e1, gate = _route(x1b, bf(peer_w_q), bf(peer_sub_keys))
    act = _peer_score(x1b, peer_u, e0, e1, gate)
    ffn = _peer_value(_peer_gate(e0, e1, act), bf(peer_v))
    return _final(x1, x1b, ffn, p, bf(ple_w_gate), bf(ple_w_proj), ln2_w, ln2_b)


def kernel(x, p, w_in, rwkv_mu, rwkv_w0, rwkv_w_lora1, rwkv_w_lora2, rwkv_a0, rwkv_a_lora1, rwkv_a_lora2, rwkv_g_lora1, rwkv_g_lora2, rwkv_k_k, rwkv_k_a, rwkv_r_k, rwkv_lnx_w, rwkv_lnx_b, w_up_a, w_up_b, w_o, ln1_w, ln1_b, peer_w_q, peer_sub_keys, peer_u, peer_v, ple_w_gate, ple_w_proj, ln2_w, ln2_b):
    bsz, t, d = x.shape
    depth = w_in.shape[0]
    xs = x.reshape(bsz * t, d)
    assert bsz == 1, "token shift / attention / scan treat the flattened rows as one sequence"
    for i in range(depth):
        xs = _layer(xs, p[i].reshape(bsz * t, -1), w_in[i], rwkv_mu[i], rwkv_w0[i], rwkv_w_lora1[i],
                    rwkv_w_lora2[i], rwkv_a0[i], rwkv_a_lora1[i], rwkv_a_lora2[i], rwkv_g_lora1[i],
                    rwkv_g_lora2[i], rwkv_k_k[i], rwkv_k_a[i], rwkv_r_k[i], rwkv_lnx_w[i], rwkv_lnx_b[i],
                    w_up_a[i], w_up_b[i], w_o[i], ln1_w[i], ln1_b[i], peer_w_q[i], peer_sub_keys[i],
                    peer_u[i], peer_v[i], ple_w_gate[i], ple_w_proj[i], ln2_w[i], ln2_b[i])
    return xs.reshape(bsz, t, d)
```
